```python
import jax, jax.numpy as jnp
from jax import lax
import numpy as np

D_MODEL = 2048
BATCH = 2
SEQ = 4096
DEPTH = 2

SC_WIDTH = D_MODEL // 2
SC_GROUPS = 8
SC_KERNEL = 3
SB_HEADS = 8
SB_HEAD_DIM = 128
SB_WIDTH = SB_HEADS * SB_HEAD_DIM
MIX_WIDTH = SC_WIDTH + SB_WIDTH
IN_PROJ_WIDTH = 3 * SC_WIDTH + 3 * SB_WIDTH
Q_BLOCK = 128
FFN_DIM = 5632
CF_INNER = D_MODEL
CF_KERNEL = 31
N_EXPERTS = 8
TOP_K = 2
EXPERT_DIM = 7 * D_MODEL // 2
NORM_EPS = 1e-6
N_EVEN = (DEPTH + 1) // 2
N_ODD = DEPTH // 2

kernel_name = 'hybrid_shortconv_stickbreak_conformer_moe'


def rms_norm(x, g):
    xf = x.astype(jnp.float32)
    y = xf * lax.rsqrt(jnp.mean(xf * xf, axis=-1, keepdims=True) + NORM_EPS)
    return (y * g.astype(jnp.float32)).astype(x.dtype)


def layer_norm(x, g, b):
    xf = x.astype(jnp.float32)
    mu = jnp.mean(xf, axis=-1, keepdims=True)
    xc = xf - mu
    var = jnp.mean(xc * xc, axis=-1, keepdims=True)
    y = xc * lax.rsqrt(var + NORM_EPS)
    return (y * g.astype(jnp.float32) + b.astype(jnp.float32)).astype(x.dtype)


def causal_depthwise_conv(u, w):
    k = w.shape[0]
    return lax.conv_general_dilated(
        u, w[:, None, :].astype(u.dtype), window_strides=(1,), padding=[(k - 1, 0)],
        dimension_numbers=('NWC', 'WIO', 'NWC'), feature_group_count=u.shape[-1])


def swiglu(h, w_gate, w_up, w_down):
    return (jax.nn.silu(h @ w_gate) * (h @ w_up)) @ w_down


def stick_breaking_attention(q, k, v):
    b, s, h, dh = q.shape
    nb = s // Q_BLOCK
    scale = dh ** -0.5
    q_blocks = q.reshape(b, nb, Q_BLOCK, h, dh).swapaxes(0, 1)
    kpos = jnp.arange(s)

    def block(args):
        qb, bi = args
        z = jnp.einsum('bqhd,bkhd->bhqk', qb, k).astype(jnp.float32) * scale
        qpos = bi * Q_BLOCK + jnp.arange(Q_BLOCK)
        mask = kpos[None, :] < qpos[:, None]
        log_keep = jnp.where(mask, jax.nn.log_sigmoid(-z), 0.0)
        log_pass = lax.cumsum(log_keep, axis=3, reverse=True) - log_keep
        a = jnp.where(mask, jnp.exp(jax.nn.log_sigmoid(z) + log_pass), 0.0)
        return jnp.einsum('bhqk,bkhd->bqhd', a.astype(v.dtype), v)

    out = lax.map(block, (q_blocks, jnp.arange(nb)))
    return out.swapaxes(0, 1).reshape(b, s, h * dh)


def moe_swiglu(h, router_w, w_gate, w_up, w_down):
    b, s, d = h.shape
    xt = h.reshape(b * s, d)
    logits = (xt @ router_w).astype(jnp.float32)
    top_val, top_idx = lax.top_k(logits, TOP_K)
    top_w = jax.nn.softmax(top_val, axis=-1)
    gates = jnp.einsum('tk,tke->te', top_w, jax.nn.one_hot(top_idx, N_EXPERTS, dtype=jnp.float32))
    out = jnp.zeros_like(xt)
    for e in range(N_EXPERTS):
        out = out + gates[:, e:e + 1].astype(xt.dtype) * swiglu(xt, w_gate[e], w_up[e], w_down[e])
    return out.reshape(b, s, d)


def even_layer(x, g_mix, w_in, conv3_w, w_out, g_ffn, wg, wu, wd):
    b, s, _ = x.shape
    h = rms_norm(x, g_mix)
    p = h @ w_in
    sc_b, sc_c, sc_h, q, k, v = jnp.split(p, 6, axis=-1)
    y_sc = sc_b * causal_depthwise_conv(sc_c * sc_h, conv3_w)
    shp = (b, s, SB_HEADS, SB_HEAD_DIM)
    y_sb = stick_breaking_attention(q.reshape(shp), k.reshape(shp), v.reshape(shp))
    x = x + jnp.concatenate([y_sc, y_sb], axis=-1) @ w_out
    return x + swiglu(rms_norm(x, g_ffn), wg, wu, wd)


def odd_layer(x, g_mix, pw1_w, pw1_b, dw_w, dw_b, ln_g, ln_b, pw2_w, pw2_b,
              g_moe, router_w, mwg, mwu, mwd):
    h = rms_norm(x, g_mix)
    u = h @ pw1_w + pw1_b
    a, gte = jnp.split(u, 2, axis=-1)
    u = a * jax.nn.sigmoid(gte)
    u = causal_depthwise_conv(u, dw_w) + dw_b
    u = jax.nn.silu(layer_norm(u, ln_g, ln_b))
    x = x + u @ pw2_w + pw2_b
    return x + moe_swiglu(rms_norm(x, g_moe), router_w, mwg, mwu, mwd)


def setup_inputs(seed: int = 0) -> dict:
    key = jax.random.key(seed)
    ks = iter(jax.random.split(key, 32))
    f32 = jnp.float32

    def w(shape, fan_in):
        return jax.random.normal(next(ks), shape, f32) * fan_in ** -0.5

    def gain(shape):
        return 1.0 + 0.02 * jax.random.normal(next(ks), shape, f32)

    def bias(shape):
        return 0.01 * jax.random.normal(next(ks), shape, f32)

    return {
        'x': jax.random.normal(next(ks), (BATCH, SEQ, D_MODEL), f32),
        'norm_mix_even': gain((N_EVEN, D_MODEL)),
        'w_in_even': w((N_EVEN, D_MODEL, IN_PROJ_WIDTH), D_MODEL),
        'conv3_w': w((N_EVEN, SC_KERNEL, SC_WIDTH), SC_KERNEL),
        'w_out_even': w((N_EVEN, MIX_WIDTH, D_MODEL), MIX_WIDTH),
        'norm_ffn_even': gain((N_EVEN, D_MODEL)),
        'ffn_w_gate': w((N_EVEN, D_MODEL, FFN_DIM), D_MODEL),
        'ffn_w_up': w((N_EVEN, D_MODEL, FFN_DIM), D_MODEL),
        'ffn_w_down': w((N_EVEN, FFN_DIM, D_MODEL), FFN_DIM),
        'norm_mix_odd': gain((N_ODD, D_MODEL)),
        'cf_pw1_w': w((N_ODD, D_MODEL, 2 * CF_INNER), D_MODEL),
        'cf_pw1_b': bias((N_ODD, 2 * CF_INNER)),
        'cf_dw_w': w((N_ODD, CF_KERNEL, CF_INNER), CF_KERNEL),
        'cf_dw_b': bias((N_ODD, CF_INNER)),
        'cf_ln_g': gain((N_ODD, CF_INNER)),
        'cf_ln_b': bias((N_ODD, CF_INNER)),
        'cf_pw2_w': w((N_ODD, CF_INNER, D_MODEL), CF_INNER),
        'cf_pw2_b': bias((N_ODD, D_MODEL)),
        'norm_moe': gain((N_ODD, D_MODEL)),
        'router_w': w((N_ODD, D_MODEL, N_EXPERTS), D_MODEL),
        'moe_w_gate': w((N_ODD, N_EXPERTS, D_MODEL, EXPERT_DIM), D_MODEL),
        'moe_w_up': w((N_ODD, N_EXPERTS, D_MODEL, EXPERT_DIM), D_MODEL),
        'moe_w_down': w((N_ODD, N_EXPERTS, EXPERT_DIM, D_MODEL), EXPERT_DIM),
        'final_norm': gain((D_MODEL,)),
    }


def reference(x, norm_mix_even, w_in_even, conv3_w, w_out_even, norm_ffn_even,
              ffn_w_gate, ffn_w_up, ffn_w_down, norm_mix_odd, cf_pw1_w, cf_pw1_b,
              cf_dw_w, cf_dw_b, cf_ln_g, cf_ln_b, cf_pw2_w, cf_pw2_b, norm_moe,
              router_w, moe_w_gate, moe_w_up, moe_w_down, final_norm):
    for layer in range(DEPTH):
        i = layer // 2
        if layer % 2 == 0:
            x = even_layer(x, norm_mix_even[i], w_in_even[i], conv3_w[i], w_out_even[i],
                           norm_ffn_even[i], ffn_w_gate[i], ffn_w_up[i], ffn_w_down[i])
        else:
            x = odd_layer(x, norm_mix_odd[i], cf_pw1_w[i], cf_pw1_b[i], cf_dw_w[i], cf_dw_b[i],
                          cf_ln_g[i], cf_ln_b[i], cf_pw2_w[i], cf_pw2_b[i], norm_moe[i],
                          router_w[i], moe_w_gate[i], moe_w_up[i], moe_w_down[i])
    return rms_norm(x, final_norm)
```

```python
import functools

import jax
import jax.numpy as jnp
from jax import lax
from jax.experimental import pallas as pl
from jax.experimental.pallas import tpu as pltpu

F32 = jnp.float32
BF16 = jnp.bfloat16

NORM_EPS = 1e-6
SC_WIDTH = 1024
SC_KERNEL = 3
SB_HEADS = 8
SB_HEAD_DIM = 128
CF_KERNEL = 31
N_EXPERTS = 8
LANES = 128
MIB = 1024 * 1024

TM_PROJ = 1024
TN_PROJ = 1024
TN_GLU = 512
TM_FFN = 1024
TF_FFN = 256
TQ_ATTN = 256
TK_ATTN = 128
TM_CONV = 256
CONV_HALO = 32
SC_HALO = 16
TM_ROUTE = 512
TM_MOE = 256
TF_MOE = 1024
TN_MOE = 512
TG_GATHER = 256
TC_COMBINE = 128


def _cparams(semantics, vmem_mib):
    return pltpu.CompilerParams(dimension_semantics=semantics,
                                vmem_limit_bytes=vmem_mib * MIB)


def _rmsnorm_to(x_ref, g_ref, h_ref, chunk=128):
    g = g_ref[...]

    def body(c, carry):
        r = pl.multiple_of(c * chunk, chunk)
        x = x_ref[pl.ds(r, chunk), :]
        ms = jnp.mean(x * x, axis=-1, keepdims=True)
        h_ref[pl.ds(r, chunk), :] = (x * lax.rsqrt(ms + NORM_EPS) * g).astype(h_ref.dtype)
        return carry

    lax.fori_loop(0, x_ref.shape[0] // chunk, body, 0)


def _bdot(a, b):
    return jnp.dot(a, b, preferred_element_type=F32)


def _norm_mm_kernel(x_ref, g_ref, w_ref, o_ref, h_ref):
    @pl.when(pl.program_id(1) == 0)
    def _():
        _rmsnorm_to(x_ref, g_ref, h_ref)

    o_ref[...] = _bdot(h_ref[...], w_ref[...].astype(BF16)).astype(o_ref.dtype)


def _norm_matmul(x, g, w, out_dtype):
    m, k = x.shape
    n = w.shape[1]
    tm, tn = min(TM_PROJ, m), min(TN_PROJ, n)
    return pl.pallas_call(
        _norm_mm_kernel,
        grid=(m // tm, n // tn),
        in_specs=[pl.BlockSpec((tm, k), lambda i, j: (i, 0)),
                  pl.BlockSpec((1, k), lambda i, j: (0, 0)),
                  pl.BlockSpec((k, tn), lambda i, j: (0, j))],
        out_specs=pl.BlockSpec((tm, tn), lambda i, j: (i, j)),
        out_shape=jax.ShapeDtypeStruct((m, n), out_dtype),
        scratch_shapes=[pltpu.VMEM((tm, k), BF16)],
        compiler_params=_cparams(("parallel", "arbitrary"), 56),
        name="norm_in_proj",
    )(x, g.reshape(1, k), w)


def _norm_glu_kernel(x_ref, g_ref, wa_ref, wg_ref, ba_ref, bg_ref, o_ref, h_ref):
    @pl.when(pl.program_id(1) == 0)
    def _():
        _rmsnorm_to(x_ref, g_ref, h_ref)

    h = h_ref[...]
    a = _bdot(h, wa_ref[...].astype(BF16)) + ba_ref[...]
    gate = _bdot(h, wg_ref[...].astype(BF16)) + bg_ref[...]
    o_ref[...] = (a * jax.nn.sigmoid(gate)).astype(o_ref.dtype)


def _norm_glu(x, g, w, b):
    m, k = x.shape
    n = w.shape[1] // 2
    tm, tn = min(TM_PROJ, m), min(TN_GLU, n)
    nb = n // tn
    b2 = b.reshape(1, 2 * n)
    return pl.pallas_call(
        _norm_glu_kernel,
        grid=(m // tm, nb),
        in_specs=[pl.BlockSpec((tm, k), lambda i, j: (i, 0)),
                  pl.BlockSpec((1, k), lambda i, j: (0, 0)),
                  pl.BlockSpec((k, tn), lambda i, j: (0, j)),
                  pl.BlockSpec((k, tn), lambda i, j: (0, j + nb)),
                  pl.BlockSpec((1, tn), lambda i, j: (0, j)),
                  pl.BlockSpec((1, tn), lambda i, j: (0, j + nb))],
        out_specs=pl.BlockSpec((tm, tn), lambda i, j: (i, j)),
        out_shape=jax.ShapeDtypeStruct((m, n), F32),
        scratch_shapes=[pltpu.VMEM((tm, k), BF16)],
        compiler_params=_cparams(("parallel", "arbitrary"), 56),
        name="norm_pw1_glu",
    )(x, g.reshape(1, k), w, w, b2, b2)


def _mm_bias_res_kernel(a_ref, w_ref, b_ref, r_ref, o_ref):
    o_ref[...] = r_ref[...] + _bdot(a_ref[...], w_ref[...].astype(BF16)) + b_ref[...]


def _matmul_bias_res(a, w, b, res):
    m, k = a.shape
    n = w.shape[1]
    tm, tn = min(TM_PROJ, m), min(TN_PROJ, n)
    return pl.pallas_call(
        _mm_bias_res_kernel,
        grid=(m // tm, n // tn),
        in_specs=[pl.BlockSpec((tm, k), lambda i, j: (i, 0)),
                  pl.BlockSpec((k, tn), lambda i, j: (0, j)),
                  pl.BlockSpec((1, tn), lambda i, j: (0, j)),
                  pl.BlockSpec((tm, tn), lambda i, j: (i, j))],
        out_specs=pl.BlockSpec((tm, tn), lambda i, j: (i, j)),
        out_shape=jax.ShapeDtypeStruct((m, n), F32),
        compiler_params=_cparams(("parallel", "arbitrary"), 56),
        name="pw2_residual",
    )(a, w, b.reshape(1, n), res)


def _attn_kernel(q_ref, k_ref, v_ref, o_ref, u_ref, c_ref, acc_ref, *, tq, tk, scale):
    q0 = pl.program_id(2) * tq
    q = q_ref[...]
    c_ref[...] = jnp.zeros_like(c_ref)
    acc_ref[...] = jnp.zeros_like(acc_ref)

    jj = lax.broadcasted_iota(jnp.int32, (2 * tk, 2 * tk), 0) & (tk - 1)
    ss = lax.broadcasted_iota(jnp.int32, (2 * tk, 2 * tk), 1)
    u_ref[...] = jnp.where((ss >= tk) | (jj > ss), 1.0, 0.0).astype(BF16)

    rows = lax.broadcasted_iota(jnp.int32, (tq, tk), 0)
    cols = lax.broadcasted_iota(jnp.int32, (tq, tk), 1)

    def step(kstart, mask):
        kb = k_ref[pl.ds(kstart, tk), :]
        vb = v_ref[pl.ds(kstart, tk), :]
        z = lax.dot_general(q, kb, (((1,), (1,)), ((), ())),
                            preferred_element_type=F32) * scale
        neg_sp = -(jnp.maximum(z, 0.0) + jnp.log(1.0 + jnp.exp(-jnp.abs(z))))
        log_keep = neg_sp if mask is None else jnp.where(mask, neg_sp, 0.0)
        hi = log_keep.astype(BF16)
        lo = (log_keep - hi.astype(F32)).astype(BF16)
        cm = _bdot(jnp.concatenate([hi, lo], axis=1), u_ref[...])
        logit = z + neg_sp + c_ref[...] + cm[:, :tk]
        a = jnp.exp(logit)
        if mask is not None:
            a = jnp.where(mask, a, 0.0)
        acc_ref[...] += _bdot(a.astype(BF16), vb)
        c_ref[...] += cm[:, tk:]

    for d in reversed(range(tq // tk)):
        step(pl.multiple_of(q0 + d * tk, tk), (cols + d * tk) < rows)

    def body(j, carry):
        step(pl.multiple_of(q0 - (j + 1) * tk, tk), None)
        return carry

    lax.fori_loop(0, q0 // tk, body, 0)
    o_ref[...] = acc_ref[...].astype(o_ref.dtype)


def _stick_breaking(p, batch, seq, col_q, col_k, col_v):
    tq, tk = min(TQ_ATTN, seq), TK_ATTN
    nq = seq // tq
    dh = SB_HEAD_DIM
    kern = functools.partial(_attn_kernel, tq=tq, tk=tk, scale=dh ** -0.5)
    return pl.pallas_call(
        kern,
        grid=(batch, SB_HEADS, nq),
        in_specs=[pl.BlockSpec((tq, dh), lambda b, h, i: (b * nq + i, col_q + h)),
                  pl.BlockSpec((seq, dh), lambda b, h, i: (b, col_k + h)),
                  pl.BlockSpec((seq, dh), lambda b, h, i: (b, col_v + h))],
        out_specs=pl.BlockSpec((tq, dh), lambda b, h, i: (b * nq + i, h)),
        out_shape=jax.ShapeDtypeStruct((batch * seq, SB_HEADS * dh), BF16),
        scratch_shapes=[pltpu.VMEM((2 * tk, 2 * tk), BF16),
                        pltpu.VMEM((tq, tk), F32),
                        pltpu.VMEM((tq, dh), F32)],
        compiler_params=_cparams(("parallel", "parallel", "arbitrary"), 32),
        name="stick_breaking_attention",
    )(p, p, p)


def _mix_out_kernel(pb_ref, pc_ref, ph_ref, hc_ref, hh_ref, cw_ref, ysb_ref, w_ref, x_ref,
                    o_ref, mix_ref, u_ref, *, tm, seq, chunk):
    i = pl.program_id(0)

    @pl.when(pl.program_id(1) == 0)
    def _():
        halo = hc_ref[...].astype(F32) * hh_ref[...].astype(F32)
        u_ref[0:SC_HALO, :] = jnp.where((i * tm) % seq == 0, 0.0, halo)
        for r in range(0, tm, chunk):
            u_ref[SC_HALO + r:SC_HALO + r + chunk, :] = (
                pc_ref[r:r + chunk, :].astype(F32) * ph_ref[r:r + chunk, :].astype(F32))
        w0, w1, w2 = cw_ref[0:1, :], cw_ref[1:2, :], cw_ref[2:3, :]
        for r in range(0, tm, chunk):
            base = SC_HALO + r
            conv = (w2 * u_ref[base:base + chunk, :]
                    + w1 * u_ref[base - 1:base - 1 + chunk, :]
                    + w0 * u_ref[base - 2:base - 2 + chunk, :])
            ysc = pb_ref[r:r + chunk, :].astype(F32) * conv
            mix_ref[r:r + chunk, 0:SC_WIDTH] = ysc.astype(BF16)
        mix_ref[:, SC_WIDTH:] = ysb_ref[...]

    o_ref[...] = x_ref[...] + _bdot(mix_ref[...], w_ref[...].astype(BF16))


def _mix_out_proj(p, ysb, conv_w, w_out, x, seq):
    m = x.shape[0]
    d = w_out.shape[1]
    kdim = w_out.shape[0]
    tm, tn = min(TM_PROJ, seq), min(TN_GLU, d)
    hb = tm // SC_HALO
    kern = functools.partial(_mix_out_kernel, tm=tm, seq=seq, chunk=min(128, tm))
    return pl.pallas_call(
        kern,
        grid=(m // tm, d // tn),
        in_specs=[pl.BlockSpec((tm, SC_WIDTH), lambda i, j: (i, 0)),
                  pl.BlockSpec((tm, SC_WIDTH), lambda i, j: (i, 1)),
                  pl.BlockSpec((tm, SC_WIDTH), lambda i, j: (i, 2)),
                  pl.BlockSpec((SC_HALO, SC_WIDTH), lambda i, j: (jnp.maximum(i * hb - 1, 0), 1)),
                  pl.BlockSpec((SC_HALO, SC_WIDTH), lambda i, j: (jnp.maximum(i * hb - 1, 0), 2)),
                  pl.BlockSpec((SC_KERNEL, SC_WIDTH), lambda i, j: (0, 0)),
                  pl.BlockSpec((tm, SC_WIDTH), lambda i, j: (i, 0)),
                  pl.BlockSpec((kdim, tn), lambda i, j: (0, j)),
                  pl.BlockSpec((tm, tn), lambda i, j: (i, j))],
        out_specs=pl.BlockSpec((tm, tn), lambda i, j: (i, j)),
        out_shape=jax.ShapeDtypeStruct((m, d), F32),
        scratch_shapes=[pltpu.VMEM((tm, kdim), BF16),
                        pltpu.VMEM((tm + SC_HALO, SC_WIDTH), F32)],
        compiler_params=_cparams(("parallel", "arbitrary"), 56),
        name="shortconv_out_proj",
    )(p, p, p, p, p, conv_w, ysb, w_out, x)


def _ffn_kernel(x_ref, g_ref, wg_ref, wu_ref, wd_ref, o_ref, h_ref):
    @pl.when(pl.program_id(1) == 0)
    def _():
        _rmsnorm_to(x_ref, g_ref, h_ref)
        o_ref[...] = x_ref[...]

    h = h_ref[...]
    gate = _bdot(h, wg_ref[...].astype(BF16))
    up = _bdot(h, wu_ref[...].astype(BF16))
    act = (gate * jax.nn.sigmoid(gate) * up).astype(BF16)
    o_ref[...] += _bdot(act, wd_ref[...].astype(BF16))


def _dense_swiglu(x, g, wg, wu, wd):
    m, d = x.shape
    f = wg.shape[1]
    tm, tf = min(TM_FFN, m), min(TF_FFN, f)
    return pl.pallas_call(
        _ffn_kernel,
        grid=(m // tm, f // tf),
        in_specs=[pl.BlockSpec((tm, d), lambda i, j: (i, 0), pipeline_mode=pl.Buffered(1)),
                  pl.BlockSpec((1, d), lambda i, j: (0, 0)),
                  pl.BlockSpec((d, tf), lambda i, j: (0, j)),
                  pl.BlockSpec((d, tf), lambda i, j: (0, j)),
                  pl.BlockSpec((tf, d), lambda i, j: (j, 0))],
        out_specs=pl.BlockSpec((tm, d), lambda i, j: (i, 0)),
        out_shape=jax.ShapeDtypeStruct((m, d), F32),
        scratch_shapes=[pltpu.VMEM((tm, d), BF16)],
        compiler_params=_cparams(("parallel", "arbitrary"), 60),
        name="dense_swiglu",
    )(x, g.reshape(1, d), wg, wu, wd)


def _dwconv_ln_kernel(u_ref, halo_ref, w_ref, b_ref, lg_ref, lb_ref, o_ref, buf_ref, conv_ref,
                      *, tm, seq, rchunk):
    i = pl.program_id(0)
    buf_ref[0:CONV_HALO, :] = jnp.where((i * tm) % seq == 0, 0.0, halo_ref[...])
    buf_ref[CONV_HALO:, :] = u_ref[...]
    nlt = u_ref.shape[1] // LANES
    first = CONV_HALO - (CF_KERNEL - 1)

    def lane_tile(c, carry):
        lo = pl.multiple_of(c * LANES, LANES)
        for r in range(0, tm, rchunk):
            acc = jnp.zeros((rchunk, LANES), F32)
            for k in range(CF_KERNEL):
                acc = acc + (w_ref[k:k + 1, pl.ds(lo, LANES)]
                             * buf_ref[first + r + k:first + r + k + rchunk, pl.ds(lo, LANES)])
            conv_ref[r:r + rchunk, pl.ds(lo, LANES)] = acc
        return carry

    lax.fori_loop(0, nlt, lane_tile, 0)

    bias, lg, lb = b_ref[...], lg_ref[...], lb_ref[...]
    for r in range(0, tm, rchunk):
        y = conv_ref[r:r + rchunk, :] + bias
        mu = jnp.mean(y, axis=-1, keepdims=True)
        yc = y - mu
        var = jnp.mean(yc * yc, axis=-1, keepdims=True)
        t = yc * lax.rsqrt(var + NORM_EPS) * lg + lb
        o_ref[r:r + rchunk, :] = (t * jax.nn.sigmoid(t)).astype(o_ref.dtype)


def _dwconv_ln_silu(u, w, b, lg, lb, seq):
    m, c = u.shape
    tm = min(TM_CONV, seq)
    hb = tm // CONV_HALO
    kern = functools.partial(_dwconv_ln_kernel, tm=tm, seq=seq, rchunk=min(64, tm))
    return pl.pallas_call(
        kern,
        grid=(m // tm,),
        in_specs=[pl.BlockSpec((tm, c), lambda i: (i, 0)),
                  pl.BlockSpec((CONV_HALO, c), lambda i: (jnp.maximum(i * hb - 1, 0), 0)),
                  pl.BlockSpec((CF_KERNEL, c), lambda i: (0, 0)),
                  pl.BlockSpec((1, c), lambda i: (0, 0)),
                  pl.BlockSpec((1, c), lambda i: (0, 0)),
                  pl.BlockSpec((1, c), lambda i: (0, 0))],
        out_specs=pl.BlockSpec((tm, c), lambda i: (i, 0)),
        out_shape=jax.ShapeDtypeStruct((m, c), BF16),
        scratch_shapes=[pltpu.VMEM((tm + CONV_HALO, c), F32),
                        pltpu.VMEM((tm, c), F32)],
        compiler_params=_cparams(("parallel",), 32),
        name="dwconv_ln_silu",
    )(u, u, w, b.reshape(1, c), lg.reshape(1, c), lb.reshape(1, c))


def _router_kernel(x_ref, g_ref, rw_ref, h_ref, gates_ref, sel_ref, rank_ref, cnt_ref, carry_ref,
                   *, tm):
    @pl.when(pl.program_id(0) == 0)
    def _():
        carry_ref[...] = jnp.zeros_like(carry_ref)

    _rmsnorm_to(x_ref, g_ref, h_ref)
    logits = jnp.dot(h_ref[...], rw_ref[...], preferred_element_type=F32,
                     precision=lax.Precision.HIGHEST)
    lane = lax.broadcasted_iota(jnp.int32, logits.shape, 1).astype(F32)
    neg = jnp.float32(-jnp.inf)
    logits = jnp.where(lane < N_EXPERTS, logits, neg)
    m1 = jnp.max(logits, axis=-1, keepdims=True)
    i1 = jnp.min(jnp.where(logits == m1, lane, float(LANES)), axis=-1, keepdims=True)
    rest = jnp.where(lane == i1, neg, logits)
    m2 = jnp.max(rest, axis=-1, keepdims=True)
    i2 = jnp.min(jnp.where(rest == m2, lane, float(LANES)), axis=-1, keepdims=True)
    e2 = jnp.exp(m2 - m1)
    w1 = 1.0 / (1.0 + e2)
    w2 = e2 / (1.0 + e2)
    gates_ref[...] = jnp.where(lane == i1, w1, 0.0) + jnp.where(lane == i2, w2, 0.0)
    sel = jnp.where((lane == i1) | (lane == i2), 1.0, 0.0)
    sel_ref[...] = sel
    rr = lax.broadcasted_iota(jnp.int32, (tm, tm), 0)
    cc = lax.broadcasted_iota(jnp.int32, (tm, tm), 1)
    tri = jnp.where(cc < rr, 1.0, 0.0).astype(BF16)
    rank_ref[...] = _bdot(tri, sel.astype(BF16)) + carry_ref[...]
    carry_ref[...] += jnp.sum(sel, axis=0, keepdims=True)
    cnt_ref[...] = carry_ref[...]


def _router(x, g, router_w):
    m, d = x.shape
    tm = min(TM_ROUTE, m)
    rw = jnp.zeros((d, LANES), F32).at[:, :N_EXPERTS].set(router_w)
    kern = functools.partial(_router_kernel, tm=tm)
    wide = jax.ShapeDtypeStruct((m, LANES), F32)
    return pl.pallas_call(
        kern,
        grid=(m // tm,),
        in_specs=[pl.BlockSpec((tm, d), lambda i: (i, 0)),
                  pl.BlockSpec((1, d), lambda i: (0, 0)),
                  pl.BlockSpec((d, LANES), lambda i: (0, 0))],
        out_specs=[pl.BlockSpec((tm, d), lambda i: (i, 0)),
                   pl.BlockSpec((tm, LANES), lambda i: (i, 0)),
                   pl.BlockSpec((tm, LANES), lambda i: (i, 0)),
                   pl.BlockSpec((tm, LANES), lambda i: (i, 0)),
                   pl.BlockSpec((1, LANES), lambda i: (0, 0))],
        out_shape=[jax.ShapeDtypeStruct((m, d), F32), wide, wide, wide,
                   jax.ShapeDtypeStruct((1, LANES), F32)],
        scratch_shapes=[pltpu.VMEM((1, LANES), F32)],
        compiler_params=_cparams(("arbitrary",), 32),
        name="router_top2",
    )(x, g.reshape(1, d), rw)


def _start_row_gather(src_hbm, idx_ref, base, n, dst_ref, sem):
    def body(r, carry):
        t = idx_ref[base + r]
        pltpu.make_async_copy(src_hbm.at[pl.ds(t, 1), :], dst_ref.at[pl.ds(r, 1), :], sem).start()
        return carry

    lax.fori_loop(0, n, body, 0, unroll=8)


def _wait_row_gather(src_hbm, n, dst_ref, sem):
    def body(r, carry):
        pltpu.make_async_copy(src_hbm.at[pl.ds(0, 1), :], dst_ref.at[pl.ds(r, 1), :], sem).wait()
        return carry

    lax.fori_loop(0, n, body, 0, unroll=8)


def _group_rows_kernel(src_ref, h_hbm, o_ref, buf_ref, sem_ref, *, tg):
    i = pl.program_id(0)
    n = pl.num_programs(0)
    slot = i % 2

    @pl.when(i == 0)
    def _():
        _start_row_gather(h_hbm, src_ref, 0, tg, buf_ref.at[0], sem_ref.at[0])

    @pl.when(i + 1 < n)
    def _():
        _start_row_gather(h_hbm, src_ref, (i + 1) * tg, tg, buf_ref.at[1 - slot], sem_ref.at[1 - slot])

    _wait_row_gather(h_hbm, tg, buf_ref.at[slot], sem_ref.at[slot])
    o_ref[...] = buf_ref[slot].astype(o_ref.dtype)


def _group_rows(h, src):
    d = h.shape[1]
    r = src.shape[0]
    tg = min(TG_GATHER, r)
    kern = functools.partial(_group_rows_kernel, tg=tg)
    return pl.pallas_call(
        kern,
        grid_spec=pltpu.PrefetchScalarGridSpec(
            num_scalar_prefetch=1,
            grid=(r // tg,),
            in_specs=[pl.BlockSpec(memory_space=pl.ANY)],
            out_specs=pl.BlockSpec((tg, d), lambda i, s: (i, 0)),
            scratch_shapes=[pltpu.VMEM((2, tg, d), F32),
                            pltpu.SemaphoreType.DMA((2,))]),
        out_shape=jax.ShapeDtypeStruct((r, d), BF16),
        compiler_params=_cparams(("arbitrary",), 32),
        name="group_rows_by_expert",
    )(src, h)


def _combine_kernel(pos_ref, x_ref, g0_ref, g1_ref, fn_ref, y_hbm, o_ref, buf_ref, sem_ref,
                    *, tc, ntok):
    i = pl.program_id(0)
    n = pl.num_programs(0)
    slot = i % 2

    def start(step, s):
        _start_row_gather(y_hbm, pos_ref, step * tc, tc, buf_ref.at[s, 0], sem_ref.at[s, 0])
        _start_row_gather(y_hbm, pos_ref, ntok + step * tc, tc, buf_ref.at[s, 1], sem_ref.at[s, 1])

    @pl.when(i == 0)
    def _():
        start(0, 0)

    @pl.when(i + 1 < n)
    def _():
        start(i + 1, 1 - slot)

    _wait_row_gather(y_hbm, tc, buf_ref.at[slot, 0], sem_ref.at[slot, 0])
    _wait_row_gather(y_hbm, tc, buf_ref.at[slot, 1], sem_ref.at[slot, 1])
    x = x_ref[...] + g0_ref[...] * buf_ref[slot, 0] + g1_ref[...] * buf_ref[slot, 1]
    ms = jnp.mean(x * x, axis=-1, keepdims=True)
    o_ref[...] = x * lax.rsqrt(ms + NORM_EPS) * fn_ref[...]


def _combine_final_norm(x, y, pos, g0, g1, final_norm):
    m, d = x.shape
    tc = min(TC_COMBINE, m)
    kern = functools.partial(_combine_kernel, tc=tc, ntok=m)
    return pl.pallas_call(
        kern,
        grid_spec=pltpu.PrefetchScalarGridSpec(
            num_scalar_prefetch=1,
            grid=(m // tc,),
            in_specs=[pl.BlockSpec((tc, d), lambda i, s: (i, 0)),
                      pl.BlockSpec((tc, 1), lambda i, s: (i, 0)),
                      pl.BlockSpec((tc, 1), lambda i, s: (i, 0)),
                      pl.BlockSpec((1, d), lambda i, s: (0, 0)),
                      pl.BlockSpec(memory_space=pl.ANY)],
            out_specs=pl.BlockSpec((tc, d), lambda i, s: (i, 0)),
            scratch_shapes=[pltpu.VMEM((2, 2, tc, d), F32),
                            pltpu.SemaphoreType.DMA((2, 2))]),
        out_shape=jax.ShapeDtypeStruct((m, d), F32),
        compiler_params=_cparams(("arbitrary",), 32),
        name="moe_combine_final_norm",
    )(pos, x, g0, g1, final_norm.reshape(1, d), y)


def _expert_changed(te_ref, r):
    prev = te_ref[jnp.maximum(r - 1, 0)]
    return (r == 0) | (te_ref[r] != prev)


def _moe_up_kernel(te_ref, nu_ref, x_ref, wg_ref, wu_ref, o_ref, wgb_ref, wub_ref):
    r = pl.program_id(1)

    @pl.when(_expert_changed(te_ref, r))
    def _():
        wgb_ref[...] = wg_ref[0].astype(BF16)
        wub_ref[...] = wu_ref[0].astype(BF16)

    @pl.when(r < nu_ref[0])
    def _():
        x = x_ref[...]
        gate = _bdot(x, wgb_ref[...])
        up = _bdot(x, wub_ref[...])
        o_ref[...] = (gate * jax.nn.sigmoid(gate) * up).astype(o_ref.dtype)

    @pl.when(r >= nu_ref[0])
    def _():
        o_ref[...] = jnp.zeros_like(o_ref)


def _moe_up(xs, wg, wu, tile_expert, n_used):
    r, d = xs.shape
    f = wg.shape[2]
    tm, tf = min(TM_MOE, r), min(TF_MOE, f)

    def row(j, i, te, nu):
        return jnp.minimum(i, nu[0] - 1)

    return pl.pallas_call(
        _moe_up_kernel,
        grid_spec=pltpu.PrefetchScalarGridSpec(
            num_scalar_prefetch=2,
            grid=(f // tf, r // tm),
            in_specs=[pl.BlockSpec((tm, d), lambda j, i, te, nu: (row(j, i, te, nu), 0)),
                      pl.BlockSpec((1, d, tf), lambda j, i, te, nu: (te[i], 0, j)),
                      pl.BlockSpec((1, d, tf), lambda j, i, te, nu: (te[i], 0, j))],
            out_specs=pl.BlockSpec((tm, tf), lambda j, i, te, nu: (i, j)),
            scratch_shapes=[pltpu.VMEM((d, tf), BF16), pltpu.VMEM((d, tf), BF16)]),
        out_shape=jax.ShapeDtypeStruct((r, f), BF16),
        compiler_params=_cparams(("arbitrary", "arbitrary"), 56),
        name="moe_gate_up",
    )(tile_expert, n_used, xs, wg, wu)


def _moe_down_kernel(te_ref, nu_ref, a_ref, wd_ref, o_ref, wdb_ref):
    r = pl.program_id(1)

    @pl.when(_expert_changed(te_ref, r))
    def _():
        wdb_ref[...] = wd_ref[0].astype(BF16)

    @pl.when(r < nu_ref[0])
    def _():
        o_ref[...] = _bdot(a_ref[...], wdb_ref[...])

    @pl.when(r >= nu_ref[0])
    def _():
        o_ref[...] = jnp.zeros_like(o_ref)


def _moe_down(act, wd, tile_expert, n_used):
    r, f = act.shape
    d = wd.shape[2]
    tm, tn = min(TM_MOE, r), min(TN_MOE, d)

    def row(j, i, te, nu):
        return jnp.minimum(i, nu[0] - 1)

    return pl.pallas_call(
        _moe_down_kernel,
        grid_spec=pltpu.PrefetchScalarGridSpec(
            num_scalar_prefetch=2,
            grid=(d // tn, r // tm),
            in_specs=[pl.BlockSpec((tm, f), lambda j, i, te, nu: (row(j, i, te, nu), 0)),
                      pl.BlockSpec((1, f, tn), lambda j, i, te, nu: (te[i], 0, j))],
            out_specs=pl.BlockSpec((tm, tn), lambda j, i, te, nu: (i, j)),
            scratch_shapes=[pltpu.VMEM((f, tn), BF16)]),
        out_shape=jax.ShapeDtypeStruct((r, d), F32),
        compiler_params=_cparams(("arbitrary", "arbitrary"), 56),
        name="moe_down",
    )(tile_expert, n_used, act, wd)


def _routing_tables(gates, sel, rank, counts, tm):
    ntok = gates.shape[0]
    n_tiles = (2 * ntok) // tm + N_EXPERTS
    cnt = counts[0, :N_EXPERTS].astype(jnp.int32)
    padded = ((cnt + tm - 1) // tm) * tm
    ends = jnp.cumsum(padded)
    offs = ends - padded
    selb = sel[:, :N_EXPERTS] > 0.5
    pos = offs[None, :] + rank[:, :N_EXPERTS].astype(jnp.int32)
    e_lo = jnp.argmax(selb, axis=1)
    e_hi = N_EXPERTS - 1 - jnp.argmax(selb[:, ::-1], axis=1)
    take = lambda a, e: jnp.take_along_axis(a, e[:, None], axis=1)[:, 0]
    pos0, pos1 = take(pos, e_lo), take(pos, e_hi)
    g8 = gates[:, :N_EXPERTS]
    g0, g1 = take(g8, e_lo), take(g8, e_hi)
    tok = jnp.arange(ntok, dtype=jnp.int32)
    src = jnp.zeros((n_tiles * tm,), jnp.int32).at[pos0].set(tok).at[pos1].set(tok)
    tile_start = jnp.arange(n_tiles, dtype=jnp.int32) * tm
    tile_expert = jnp.minimum(
        jnp.sum(tile_start[:, None] >= ends[None, :], axis=1), N_EXPERTS - 1).astype(jnp.int32)
    n_used = (ends[-1] // tm).astype(jnp.int32).reshape(1)
    tile_expert = jnp.where(tile_start < ends[-1], tile_expert,
                            tile_expert[jnp.maximum(n_used[0] - 1, 0)])
    pos_all = jnp.concatenate([pos0, pos1]).astype(jnp.int32)
    return src, pos_all, g0[:, None], g1[:, None], tile_expert, n_used


def _moe_final(x, g, router_w, wg, wu, wd, final_norm):
    h, gates, sel, rank, counts = _router(x, g, router_w)
    tm = min(TM_MOE, x.shape[0])
    src, pos_all, g0, g1, tile_expert, n_used = _routing_tables(gates, sel, rank, counts, tm)
    xs = _group_rows(h, src)
    act = _moe_up(xs, wg, wu, tile_expert, n_used)
    y = _moe_down(act, wd, tile_expert, n_used)
    return _combine_final_norm(x, y, pos_all, g0, g1, final_norm)


def _forward(x, norm_mix_even, w_in_even, conv3_w, w_out_even, norm_ffn_even,
             ffn_w_gate, ffn_w_up, ffn_w_down, norm_mix_odd, cf_pw1_w, cf_pw1_b,
             cf_dw_w, cf_dw_b, cf_ln_g, cf_ln_b, cf_pw2_w, cf_pw2_b, norm_moe,
             router_w, moe_w_gate, moe_w_up, moe_w_down, final_norm):
    batch, seq, d = x.shape
    xt = x.reshape(batch * seq, d)
    ncol = SC_WIDTH // LANES
    p = _norm_matmul(xt, norm_mix_even[0], w_in_even[0], BF16)
    ysb = _stick_breaking(p, batch, seq, 3 * ncol, 3 * ncol + SB_HEADS, 3 * ncol + 2 * SB_HEADS)
    xt = _mix_out_proj(p, ysb, conv3_w[0], w_out_even[0], xt, seq)
    xt = _dense_swiglu(xt, norm_ffn_even[0], ffn_w_gate[0], ffn_w_up[0], ffn_w_down[0])
    u = _norm_glu(xt, norm_mix_odd[0], cf_pw1_w[0], cf_pw1_b[0])
    v = _dwconv_ln_silu(u, cf_dw_w[0], cf_dw_b[0], cf_ln_g[0], cf_ln_b[0], seq)
    xt = _matmul_bias_res(v, cf_pw2_w[0], cf_pw2_b[0], xt)
    out = _moe_final(xt, norm_moe[0], router_w[0], moe_w_gate[0], moe_w_up[0], moe_w_down[0],
                     final_norm)
    return out.reshape(batch, seq, d)


def kernel(x, norm_mix_even, w_in_even, conv3_w, w_out_even, norm_ffn_even, ffn_w_gate, ffn_w_up, ffn_w_down, norm_mix_odd, cf_pw1_w, cf_pw1_b, cf_dw_w, cf_dw_b, cf_ln_g, cf_ln_b, cf_pw2_w, cf_pw2_b, norm_moe, router_w, moe_w_gate, moe_w_up, moe_w_down, final_norm):
    return _forward(x, norm_mix_even, w_in_even, conv3_w, w_out_even, norm_ffn_even,
                    ffn_w_gate, ffn_w_up, ffn_w_down, norm_mix_odd, cf_pw1_w, cf_pw1_b,
                    cf_dw_w, cf_dw_b, cf_ln_g, cf_ln_b, cf_pw2_w, cf_pw2_b, norm_moe,
                    router_w, moe_w_gate, moe_w_up, moe_w_down, final_norm)
```

```python
import functools

import jax
import jax.numpy as jnp
from jax import lax
from jax.experimental import pallas as pl
from jax.experimental.pallas import tpu as pltpu

F32 = jnp.float32
BF16 = jnp.bfloat16

NORM_EPS = 1e-6
SC_WIDTH = 1024
SC_KERNEL = 3
SB_HEADS = 8
SB_HEAD_DIM = 128
CF_KERNEL = 31
N_EXPERTS = 8
LANES = 128
MIB = 1024 * 1024

TM_PROJ = 1024
TN_PROJ = 1024
TN_GLU = 512
TM_FFN = 1024
TF_FFN = 256
TQ_ATTN = 512
LOG2E = 1.4426950408889634
TM_CONV = 256
CONV_HALO = 32
SC_HALO = 16
TM_ROUTE = 512
TM_MOE = 256
TF_MOE = 1024
TN_MOE = 512
TG_GATHER = 256
TC_COMBINE = 128


def _cparams(semantics, vmem_mib):
    return pltpu.CompilerParams(dimension_semantics=semantics,
                                vmem_limit_bytes=vmem_mib * MIB)


def _rmsnorm_to(x_ref, g_ref, h_ref, chunk=128):
    g = g_ref[...]

    def body(c, carry):
        r = pl.multiple_of(c * chunk, chunk)
        x = x_ref[pl.ds(r, chunk), :]
        ms = jnp.mean(x * x, axis=-1, keepdims=True)
        h_ref[pl.ds(r, chunk), :] = (x * lax.rsqrt(ms + NORM_EPS) * g).astype(h_ref.dtype)
        return carry

    lax.fori_loop(0, x_ref.shape[0] // chunk, body, 0)


def _bdot(a, b):
    return jnp.dot(a, b, preferred_element_type=F32)


def _norm_mm_kernel(x_ref, g_ref, w_ref, cs_ref, o_ref, h_ref):
    @pl.when(pl.program_id(1) == 0)
    def _():
        _rmsnorm_to(x_ref, g_ref, h_ref)

    o_ref[...] = (_bdot(h_ref[...], w_ref[...].astype(BF16)) * cs_ref[...]).astype(o_ref.dtype)


def _norm_matmul(x, g, w, col_scale, out_dtype):
    m, k = x.shape
    n = w.shape[1]
    tm, tn = min(TM_PROJ, m), min(TN_PROJ, n)
    return pl.pallas_call(
        _norm_mm_kernel,
        grid=(m // tm, n // tn),
        in_specs=[pl.BlockSpec((tm, k), lambda i, j: (i, 0)),
                  pl.BlockSpec((1, k), lambda i, j: (0, 0)),
                  pl.BlockSpec((k, tn), lambda i, j: (0, j)),
                  pl.BlockSpec((1, tn), lambda i, j: (0, j))],
        out_specs=pl.BlockSpec((tm, tn), lambda i, j: (i, j)),
        out_shape=jax.ShapeDtypeStruct((m, n), out_dtype),
        scratch_shapes=[pltpu.VMEM((tm, k), BF16)],
        compiler_params=_cparams(("parallel", "arbitrary"), 56),
        name="norm_in_proj",
    )(x, g.reshape(1, k), w, col_scale.reshape(1, n))


def _norm_glu_kernel(x_ref, g_ref, wa_ref, wg_ref, ba_ref, bg_ref, o_ref, h_ref):
    @pl.when(pl.program_id(1) == 0)
    def _():
        _rmsnorm_to(x_ref, g_ref, h_ref)

    h = h_ref[...]
    a = _bdot(h, wa_ref[...].astype(BF16)) + ba_ref[...]
    gate = _bdot(h, wg_ref[...].astype(BF16)) + bg_ref[...]
    o_ref[...] = (a * jax.nn.sigmoid(gate)).astype(o_ref.dtype)


def _norm_glu(x, g, w, b):
    m, k = x.shape
    n = w.shape[1] // 2
    tm, tn = min(TM_PROJ, m), min(TN_GLU, n)
    nb = n // tn
    b2 = b.reshape(1, 2 * n)
    return pl.pallas_call(
        _norm_glu_kernel,
        grid=(m // tm, nb),
        in_specs=[pl.BlockSpec((tm, k), lambda i, j: (i, 0)),
                  pl.BlockSpec((1, k), lambda i, j: (0, 0)),
                  pl.BlockSpec((k, tn), lambda i, j: (0, j)),
                  pl.BlockSpec((k, tn), lambda i, j: (0, j + nb)),
                  pl.BlockSpec((1, tn), lambda i, j: (0, j)),
                  pl.BlockSpec((1, tn), lambda i, j: (0, j + nb))],
        out_specs=pl.BlockSpec((tm, tn), lambda i, j: (i, j)),
        out_shape=jax.ShapeDtypeStruct((m, n), F32),
        scratch_shapes=[pltpu.VMEM((tm, k), BF16)],
        compiler_params=_cparams(("parallel", "arbitrary"), 56),
        name="norm_pw1_glu",
    )(x, g.reshape(1, k), w, w, b2, b2)


def _mm_bias_res_kernel(a_ref, w_ref, b_ref, r_ref, o_ref):
    o_ref[...] = r_ref[...] + _bdot(a_ref[...], w_ref[...].astype(BF16)) + b_ref[...]


def _matmul_bias_res(a, w, b, res):
    m, k = a.shape
    n = w.shape[1]
    tm, tn = min(TM_PROJ, m), min(TN_PROJ, n)
    return pl.pallas_call(
        _mm_bias_res_kernel,
        grid=(m // tm, n // tn),
        in_specs=[pl.BlockSpec((tm, k), lambda i, j: (i, 0)),
                  pl.BlockSpec((k, tn), lambda i, j: (0, j)),
                  pl.BlockSpec((1, tn), lambda i, j: (0, j)),
                  pl.BlockSpec((tm, tn), lambda i, j: (i, j))],
        out_specs=pl.BlockSpec((tm, tn), lambda i, j: (i, j)),
        out_shape=jax.ShapeDtypeStruct((m, n), F32),
        compiler_params=_cparams(("parallel", "arbitrary"), 56),
        name="pw2_residual",
    )(a, w, b.reshape(1, n), res)


def _attn_kernel(q_ref, k_ref, v_ref, o_ref, u_ref, c_ref, acc_ref, nz_ref, cm_ref, *, tq):
    hk = LANES
    tk = 2 * hk
    q0 = pl.program_id(2) * tq
    q = q_ref[...]
    c_ref[...] = jnp.zeros_like(c_ref)
    acc_ref[...] = jnp.zeros_like(acc_ref)

    jj = lax.broadcasted_iota(jnp.int32, (2 * hk, 2 * hk), 0) & (hk - 1)
    ss = lax.broadcasted_iota(jnp.int32, (2 * hk, 2 * hk), 1)
    u_ref[...] = jnp.where((ss >= hk) | (jj >= ss), 1.0, 0.0).astype(BF16)

    def half_sums(lk):
        hi = lk.astype(BF16)
        lo = (lk - hi.astype(F32)).astype(BF16)
        return _bdot(jnp.concatenate([hi, lo], axis=1), u_ref[...])

    def scores(kstart, slot, mask):
        kb = k_ref[pl.ds(kstart, tk), :]
        nz = lax.dot_general(q, kb, (((1,), (1,)), ((), ())), preferred_element_type=F32)
        log_keep = jnp.minimum(nz, 0.0) - jnp.log2(1.0 + jnp.exp2(-jnp.abs(nz)))
        if mask is not None:
            log_keep = jnp.where(mask, log_keep, 0.0)
        nz_ref[slot] = nz
        cm_ref[slot, 1] = half_sums(log_keep[:, hk:])
        cm_ref[slot, 0] = half_sums(log_keep[:, :hk])

    def weights(kstart, slot, mask):
        c = c_ref[...]
        nz = nz_ref[slot]
        cm1 = cm_ref[slot, 1]
        cm0 = cm_ref[slot, 0]
        logit1 = (c + cm1[:, :hk]) - nz[:, hk:]
        c = c + cm1[:, hk:]
        logit0 = (c + cm0[:, :hk]) - nz[:, :hk]
        c_ref[...] = c + cm0[:, hk:]
        a = jnp.exp2(jnp.concatenate([logit0, logit1], axis=1))
        if mask is not None:
            a = jnp.where(mask, a, 0.0)
        acc_ref[...] += _bdot(a.astype(BF16), v_ref[pl.ds(kstart, tk), :])

    rows = lax.broadcasted_iota(jnp.int32, (tq, tk), 0)
    cols = lax.broadcasted_iota(jnp.int32, (tq, tk), 1)
    nd = tq // tk
    diag = [(pl.multiple_of(q0 + d * tk, tk), (cols + d * tk) < rows)
            for d in reversed(range(nd))]
    nb = q0 // tk

    def below(j):
        return pl.multiple_of(q0 - (j + 1) * tk, tk)

    scores(diag[0][0], 0, diag[0][1])
    for i in range(1, nd):
        scores(diag[i][0], i % 2, diag[i][1])
        weights(diag[i - 1][0], (i - 1) % 2, diag[i - 1][1])
    last_k, last_mask = diag[nd - 1]

    @pl.when(nb == 0)
    def _():
        weights(last_k, (nd - 1) % 2, last_mask)

    @pl.when(nb > 0)
    def _():
        scores(below(0), nd % 2, None)
        weights(last_k, (nd - 1) % 2, last_mask)

        def body(j, carry):
            weights(below(j), (nd + j) % 2, None)
            scores(below(j + 1), (nd + j + 1) % 2, None)
            return carry

        lax.fori_loop(0, nb - 1, body, 0)
        weights(below(nb - 1), (nd + nb - 1) % 2, None)

    o_ref[...] = acc_ref[...].astype(o_ref.dtype)


def _stick_breaking(p, batch, seq, col_q, col_k, col_v):
    tq = min(TQ_ATTN, seq)
    nq = seq // tq
    dh = SB_HEAD_DIM
    kern = functools.partial(_attn_kernel, tq=tq)
    return pl.pallas_call(
        kern,
        grid=(batch, SB_HEADS, nq),
        in_specs=[pl.BlockSpec((tq, dh), lambda b, h, i: (b * nq + i, col_q + h)),
                  pl.BlockSpec((seq, dh), lambda b, h, i: (b, col_k + h)),
                  pl.BlockSpec((seq, dh), lambda b, h, i: (b, col_v + h))],
        out_specs=pl.BlockSpec((tq, dh), lambda b, h, i: (b * nq + i, h)),
        out_shape=jax.ShapeDtypeStruct((batch * seq, SB_HEADS * dh), BF16),
        scratch_shapes=[pltpu.VMEM((2 * LANES, 2 * LANES), BF16),
                        pltpu.VMEM((tq, LANES), F32),
                        pltpu.VMEM((tq, dh), F32),
                        pltpu.VMEM((2, tq, 2 * LANES), F32),
                        pltpu.VMEM((2, 2, tq, 2 * LANES), F32)],
        compiler_params=_cparams(("parallel", "parallel", "arbitrary"), 32),
        name="stick_breaking_attention",
    )(p, p, p)


def _mix_out_kernel(pb_ref, pc_ref, ph_ref, hc_ref, hh_ref, cw_ref, ysb_ref, w_ref, x_ref,
                    o_ref, mix_ref, u_ref, *, tm, seq, chunk):
    i = pl.program_id(0)

    @pl.when(pl.program_id(1) == 0)
    def _():
        halo = hc_ref[...].astype(F32) * hh_ref[...].astype(F32)
        u_ref[0:SC_HALO, :] = jnp.where((i * tm) % seq == 0, 0.0, halo)
        for r in range(0, tm, chunk):
            u_ref[SC_HALO + r:SC_HALO + r + chunk, :] = (
                pc_ref[r:r + chunk, :].astype(F32) * ph_ref[r:r + chunk, :].astype(F32))
        w0, w1, w2 = cw_ref[0:1, :], cw_ref[1:2, :], cw_ref[2:3, :]
        for r in range(0, tm, chunk):
            base = SC_HALO + r
            conv = (w2 * u_ref[base:base + chunk, :]
                    + w1 * u_ref[base - 1:base - 1 + chunk, :]
                    + w0 * u_ref[base - 2:base - 2 + chunk, :])
            ysc = pb_ref[r:r + chunk, :].astype(F32) * conv
            mix_ref[r:r + chunk, 0:SC_WIDTH] = ysc.astype(BF16)
        mix_ref[:, SC_WIDTH:] = ysb_ref[...]

    o_ref[...] = x_ref[...] + _bdot(mix_ref[...], w_ref[...].astype(BF16))


def _mix_out_proj(p, ysb, conv_w, w_out, x, seq):
    m = x.shape[0]
    d = w_out.shape[1]
    kdim = w_out.shape[0]
    tm, tn = min(TM_PROJ, seq), min(TN_GLU, d)
    hb = tm // SC_HALO
    kern = functools.partial(_mix_out_kernel, tm=tm, seq=seq, chunk=min(128, tm))
    return pl.pallas_call(
        kern,
        grid=(m // tm, d // tn),
        in_specs=[pl.BlockSpec((tm, SC_WIDTH), lambda i, j: (i, 0)),
                  pl.BlockSpec((tm, SC_WIDTH), lambda i, j: (i, 1)),
                  pl.BlockSpec((tm, SC_WIDTH), lambda i, j: (i, 2)),
                  pl.BlockSpec((SC_HALO, SC_WIDTH), lambda i, j: (jnp.maximum(i * hb - 1, 0), 1)),
                  pl.BlockSpec((SC_HALO, SC_WIDTH), lambda i, j: (jnp.maximum(i * hb - 1, 0), 2)),
                  pl.BlockSpec((SC_KERNEL, SC_WIDTH), lambda i, j: (0, 0)),
                  pl.BlockSpec((tm, SC_WIDTH), lambda i, j: (i, 0)),
                  pl.BlockSpec((kdim, tn), lambda i, j: (0, j)),
                  pl.BlockSpec((tm, tn), lambda i, j: (i, j))],
        out_specs=pl.BlockSpec((tm, tn), lambda i, j: (i, j)),
        out_shape=jax.ShapeDtypeStruct((m, d), F32),
        scratch_shapes=[pltpu.VMEM((tm, kdim), BF16),
                        pltpu.VMEM((tm + SC_HALO, SC_WIDTH), F32)],
        compiler_params=_cparams(("parallel", "arbitrary"), 56),
        name="shortconv_out_proj",
    )(p, p, p, p, p, conv_w, ysb, w_out, x)


def _ffn_kernel(x_ref, g_ref, wg_ref, wu_ref, wd_ref, o_ref, h_ref):
    @pl.when(pl.program_id(1) == 0)
    def _():
        _rmsnorm_to(x_ref, g_ref, h_ref)
        o_ref[...] = x_ref[...]

    h = h_ref[...]
    gate = _bdot(h, wg_ref[...].astype(BF16))
    up = _bdot(h, wu_ref[...].astype(BF16))
    act = (gate * jax.nn.sigmoid(gate) * up).astype(BF16)
    o_ref[...] += _bdot(act, wd_ref[...].astype(BF16))


def _dense_swiglu(x, g, wg, wu, wd):
    m, d = x.shape
    f = wg.shape[1]
    tm, tf = min(TM_FFN, m), min(TF_FFN, f)
    return pl.pallas_call(
        _ffn_kernel,
        grid=(m // tm, f // tf),
        in_specs=[pl.BlockSpec((tm, d), lambda i, j: (i, 0), pipeline_mode=pl.Buffered(1)),
                  pl.BlockSpec((1, d), lambda i, j: (0, 0)),
                  pl.BlockSpec((d, tf), lambda i, j: (0, j)),
                  pl.BlockSpec((d, tf), lambda i, j: (0, j)),
                  pl.BlockSpec((tf, d), lambda i, j: (j, 0))],
        out_specs=pl.BlockSpec((tm, d), lambda i, j: (i, 0)),
        out_shape=jax.ShapeDtypeStruct((m, d), F32),
        scratch_shapes=[pltpu.VMEM((tm, d), BF16)],
        compiler_params=_cparams(("parallel", "arbitrary"), 60),
        name="dense_swiglu",
    )(x, g.reshape(1, d), wg, wu, wd)


def _dwconv_ln_kernel(u_ref, halo_ref, w_ref, b_ref, lg_ref, lb_ref, o_ref, buf_ref, conv_ref,
                      *, tm, seq, rchunk):
    i = pl.program_id(0)
    buf_ref[0:CONV_HALO, :] = jnp.where((i * tm) % seq == 0, 0.0, halo_ref[...])
    buf_ref[CONV_HALO:, :] = u_ref[...]
    nlt = u_ref.shape[1] // LANES
    first = CONV_HALO - (CF_KERNEL - 1)

    def lane_tile(c, carry):
        lo = pl.multiple_of(c * LANES, LANES)
        for r in range(0, tm, rchunk):
            acc = jnp.zeros((rchunk, LANES), F32)
            for k in range(CF_KERNEL):
                acc = acc + (w_ref[k:k + 1, pl.ds(lo, LANES)]
                             * buf_ref[first + r + k:first + r + k + rchunk, pl.ds(lo, LANES)])
            conv_ref[r:r + rchunk, pl.ds(lo, LANES)] = acc
        return carry

    lax.fori_loop(0, nlt, lane_tile, 0)

    bias, lg, lb = b_ref[...], lg_ref[...], lb_ref[...]
    for r in range(0, tm, rchunk):
        y = conv_ref[r:r + rchunk, :] + bias
        mu = jnp.mean(y, axis=-1, keepdims=True)
        yc = y - mu
        var = jnp.mean(yc * yc, axis=-1, keepdims=True)
        t = yc * lax.rsqrt(var + NORM_EPS) * lg + lb
        o_ref[r:r + rchunk, :] = (t * jax.nn.sigmoid(t)).astype(o_ref.dtype)


def _dwconv_ln_silu(u, w, b, lg, lb, seq):
    m, c = u.shape
    tm = min(TM_CONV, seq)
    hb = tm // CONV_HALO
    kern = functools.partial(_dwconv_ln_kernel, tm=tm, seq=seq, rchunk=min(64, tm))
    return pl.pallas_call(
        kern,
        grid=(m // tm,),
        in_specs=[pl.BlockSpec((tm, c), lambda i: (i, 0)),
                  pl.BlockSpec((CONV_HALO, c), lambda i: (jnp.maximum(i * hb - 1, 0), 0)),
                  pl.BlockSpec((CF_KERNEL, c), lambda i: (0, 0)),
                  pl.BlockSpec((1, c), lambda i: (0, 0)),
                  pl.BlockSpec((1, c), lambda i: (0, 0)),
                  pl.BlockSpec((1, c), lambda i: (0, 0))],
        out_specs=pl.BlockSpec((tm, c), lambda i: (i, 0)),
        out_shape=jax.ShapeDtypeStruct((m, c), BF16),
        scratch_shapes=[pltpu.VMEM((tm + CONV_HALO, c), F32),
                        pltpu.VMEM((tm, c), F32)],
        compiler_params=_cparams(("parallel",), 32),
        name="dwconv_ln_silu",
    )(u, u, w, b.reshape(1, c), lg.reshape(1, c), lb.reshape(1, c))


def _router_kernel(x_ref, g_ref, rw_ref, h_ref, gates_ref, sel_ref, rank_ref, cnt_ref, carry_ref,
                   *, tm):
    @pl.when(pl.program_id(0) == 0)
    def _():
        carry_ref[...] = jnp.zeros_like(carry_ref)

    _rmsnorm_to(x_ref, g_ref, h_ref)
    logits = jnp.dot(h_ref[...], rw_ref[...], preferred_element_type=F32,
                     precision=lax.Precision.HIGHEST)
    lane = lax.broadcasted_iota(jnp.int32, logits.shape, 1).astype(F32)
    neg = jnp.float32(-jnp.inf)
    logits = jnp.where(lane < N_EXPERTS, logits, neg)
    m1 = jnp.max(logits, axis=-1, keepdims=True)
    i1 = jnp.min(jnp.where(logits == m1, lane, float(LANES)), axis=-1, keepdims=True)
    rest = jnp.where(lane == i1, neg, logits)
    m2 = jnp.max(rest, axis=-1, keepdims=True)
    i2 = jnp.min(jnp.where(rest == m2, lane, float(LANES)), axis=-1, keepdims=True)
    e2 = jnp.exp(m2 - m1)
    w1 = 1.0 / (1.0 + e2)
    w2 = e2 / (1.0 + e2)
    gates_ref[...] = jnp.where(lane == i1, w1, 0.0) + jnp.where(lane == i2, w2, 0.0)
    sel = jnp.where((lane == i1) | (lane == i2), 1.0, 0.0)
    sel_ref[...] = sel
    rr = lax.broadcasted_iota(jnp.int32, (tm, tm), 0)
    cc = lax.broadcasted_iota(jnp.int32, (tm, tm), 1)
    tri = jnp.where(cc < rr, 1.0, 0.0).astype(BF16)
    rank_ref[...] = _bdot(tri, sel.astype(BF16)) + carry_ref[...]
    carry_ref[...] += jnp.sum(sel, axis=0, keepdims=True)
    cnt_ref[...] = carry_ref[...]


def _router(x, g, router_w):
    m, d = x.shape
    tm = min(TM_ROUTE, m)
    rw = jnp.zeros((d, LANES), F32).at[:, :N_EXPERTS].set(router_w)
    kern = functools.partial(_router_kernel, tm=tm)
    wide = jax.ShapeDtypeStruct((m, LANES), F32)
    return pl.pallas_call(
        kern,
        grid=(m // tm,),
        in_specs=[pl.BlockSpec((tm, d), lambda i: (i, 0)),
                  pl.BlockSpec((1, d), lambda i: (0, 0)),
                  pl.BlockSpec((d, LANES), lambda i: (0, 0))],
        out_specs=[pl.BlockSpec((tm, d), lambda i: (i, 0)),
                   pl.BlockSpec((tm, LANES), lambda i: (i, 0)),
                   pl.BlockSpec((tm, LANES), lambda i: (i, 0)),
                   pl.BlockSpec((tm, LANES), lambda i: (i, 0)),
                   pl.BlockSpec((1, LANES), lambda i: (0, 0))],
        out_shape=[jax.ShapeDtypeStruct((m, d), F32), wide, wide, wide,
                   jax.ShapeDtypeStruct((1, LANES), F32)],
        scratch_shapes=[pltpu.VMEM((1, LANES), F32)],
        compiler_params=_cparams(("arbitrary",), 32),
        name="router_top2",
    )(x, g.reshape(1, d), rw)


def _start_row_gather(src_hbm, idx_ref, base, n, dst_ref, sem):
    def body(r, carry):
        t = idx_ref[base + r]
        pltpu.make_async_copy(src_hbm.at[pl.ds(t, 1), :], dst_ref.at[pl.ds(r, 1), :], sem).start()
        return carry

    lax.fori_loop(0, n, body, 0, unroll=8)


def _wait_row_gather(src_hbm, n, dst_ref, sem):
    def body(r, carry):
        pltpu.make_async_copy(src_hbm.at[pl.ds(0, 1), :], dst_ref.at[pl.ds(r, 1), :], sem).wait()
        return carry

    lax.fori_loop(0, n, body, 0, unroll=8)


def _group_rows_kernel(src_ref, h_hbm, o_ref, buf_ref, sem_ref, *, tg):
    i = pl.program_id(0)
    n = pl.num_programs(0)
    slot = i % 2

    @pl.when(i == 0)
    def _():
        _start_row_gather(h_hbm, src_ref, 0, tg, buf_ref.at[0], sem_ref.at[0])

    @pl.when(i + 1 < n)
    def _():
        _start_row_gather(h_hbm, src_ref, (i + 1) * tg, tg, buf_ref.at[1 - slot], sem_ref.at[1 - slot])

    _wait_row_gather(h_hbm, tg, buf_ref.at[slot], sem_ref.at[slot])
    o_ref[...] = buf_ref[slot].astype(o_ref.dtype)


def _group_rows(h, src):
    d = h.shape[1]
    r = src.shape[0]
    tg = min(TG_GATHER, r)
    kern = functools.partial(_group_rows_kernel, tg=tg)
    return pl.pallas_call(
        kern,
        grid_spec=pltpu.PrefetchScalarGridSpec(
            num_scalar_prefetch=1,
            grid=(r // tg,),
            in_specs=[pl.BlockSpec(memory_space=pl.ANY)],
            out_specs=pl.BlockSpec((tg, d), lambda i, s: (i, 0)),
            scratch_shapes=[pltpu.VMEM((2, tg, d), F32),
                            pltpu.SemaphoreType.DMA((2,))]),
        out_shape=jax.ShapeDtypeStruct((r, d), BF16),
        compiler_params=_cparams(("arbitrary",), 32),
        name="group_rows_by_expert",
    )(src, h)


def _combine_kernel(pos_ref, x_ref, g0_ref, g1_ref, fn_ref, y_hbm, o_ref, buf_ref, sem_ref,
                    *, tc, ntok):
    i = pl.program_id(0)
    n = pl.num_programs(0)
    slot = i % 2

    def start(step, s):
        _start_row_gather(y_hbm, pos_ref, step * tc, tc, buf_ref.at[s, 0], sem_ref.at[s, 0])
        _start_row_gather(y_hbm, pos_ref, ntok + step * tc, tc, buf_ref.at[s, 1], sem_ref.at[s, 1])

    @pl.when(i == 0)
    def _():
        start(0, 0)

    @pl.when(i + 1 < n)
    def _():
        start(i + 1, 1 - slot)

    _wait_row_gather(y_hbm, tc, buf_ref.at[slot, 0], sem_ref.at[slot, 0])
    _wait_row_gather(y_hbm, tc, buf_ref.at[slot, 1], sem_ref.at[slot, 1])
    x = x_ref[...] + g0_ref[...] * buf_ref[slot, 0] + g1_ref[...] * buf_ref[slot, 1]
    ms = jnp.mean(x * x, axis=-1, keepdims=True)
    o_ref[...] = x * lax.rsqrt(ms + NORM_EPS) * fn_ref[...]


def _combine_final_norm(x, y, pos, g0, g1, final_norm):
    m, d = x.shape
    tc = min(TC_COMBINE, m)
    kern = functools.partial(_combine_kernel, tc=tc, ntok=m)
    return pl.pallas_call(
        kern,
        grid_spec=pltpu.PrefetchScalarGridSpec(
            num_scalar_prefetch=1,
            grid=(m // tc,),
            in_specs=[pl.BlockSpec((tc, d), lambda i, s: (i, 0)),
                      pl.BlockSpec((tc, 1), lambda i, s: (i, 0)),
                      pl.BlockSpec((tc, 1), lambda i, s: (i, 0)),
                      pl.BlockSpec((1, d), lambda i, s: (0, 0)),
                      pl.BlockSpec(memory_space=pl.ANY)],
            out_specs=pl.BlockSpec((tc, d), lambda i, s: (i, 0)),
            scratch_shapes=[pltpu.VMEM((2, 2, tc, d), F32),
                            pltpu.SemaphoreType.DMA((2, 2))]),
        out_shape=jax.ShapeDtypeStruct((m, d), F32),
        compiler_params=_cparams(("arbitrary",), 32),
        name="moe_combine_final_norm",
    )(pos, x, g0, g1, final_norm.reshape(1, d), y)


def _expert_changed(te_ref, r):
    prev = te_ref[jnp.maximum(r - 1, 0)]
    return (r == 0) | (te_ref[r] != prev)


def _moe_up_kernel(te_ref, nu_ref, x_ref, wg_ref, wu_ref, o_ref, wgb_ref, wub_ref):
    r = pl.program_id(1)

    @pl.when(_expert_changed(te_ref, r))
    def _():
        wgb_ref[...] = wg_ref[0].astype(BF16)
        wub_ref[...] = wu_ref[0].astype(BF16)

    @pl.when(r < nu_ref[0])
    def _():
        x = x_ref[...]
        gate = _bdot(x, wgb_ref[...])
        up = _bdot(x, wub_ref[...])
        o_ref[...] = (gate * jax.nn.sigmoid(gate) * up).astype(o_ref.dtype)

    @pl.when(r >= nu_ref[0])
    def _():
        o_ref[...] = jnp.zeros_like(o_ref)


def _moe_up(xs, wg, wu, tile_expert, n_used):
    r, d = xs.shape
    f = wg.shape[2]
    tm, tf = min(TM_MOE, r), min(TF_MOE, f)

    def row(j, i, te, nu):
        return jnp.minimum(i, nu[0] - 1)

    return pl.pallas_call(
        _moe_up_kernel,
        grid_spec=pltpu.PrefetchScalarGridSpec(
            num_scalar_prefetch=2,
            grid=(f // tf, r // tm),
            in_specs=[pl.BlockSpec((tm, d), lambda j, i, te, nu: (row(j, i, te, nu), 0)),
                      pl.BlockSpec((1, d, tf), lambda j, i, te, nu: (te[i], 0, j)),
                      pl.BlockSpec((1, d, tf), lambda j, i, te, nu: (te[i], 0, j))],
            out_specs=pl.BlockSpec((tm, tf), lambda j, i, te, nu: (i, j)),
            scratch_shapes=[pltpu.VMEM((d, tf), BF16), pltpu.VMEM((d, tf), BF16)]),
        out_shape=jax.ShapeDtypeStruct((r, f), BF16),
        compiler_params=_cparams(("arbitrary", "arbitrary"), 56),
        name="moe_gate_up",
    )(tile_expert, n_used, xs, wg, wu)


def _moe_down_kernel(te_ref, nu_ref, a_ref, wd_ref, o_ref, wdb_ref):
    r = pl.program_id(1)

    @pl.when(_expert_changed(te_ref, r))
    def _():
        wdb_ref[...] = wd_ref[0].astype(BF16)

    @pl.when(r < nu_ref[0])
    def _():
        o_ref[...] = _bdot(a_ref[...], wdb_ref[...])

    @pl.when(r >= nu_ref[0])
    def _():
        o_ref[...] = jnp.zeros_like(o_ref)


def _moe_down(act, wd, tile_expert, n_used):
    r, f = act.shape
    d = wd.shape[2]
    tm, tn = min(TM_MOE, r), min(TN_MOE, d)

    def row(j, i, te, nu):
        return jnp.minimum(i, nu[0] - 1)

    return pl.pallas_call(
        _moe_down_kernel,
        grid_spec=pltpu.PrefetchScalarGridSpec(
            num_scalar_prefetch=2,
            grid=(d // tn, r // tm),
            in_specs=[pl.BlockSpec((tm, f), lambda j, i, te, nu: (row(j, i, te, nu), 0)),
                      pl.BlockSpec((1, f, tn), lambda j, i, te, nu: (te[i], 0, j))],
            out_specs=pl.BlockSpec((tm, tn), lambda j, i, te, nu: (i, j)),
            scratch_shapes=[pltpu.VMEM((f, tn), BF16)]),
        out_shape=jax.ShapeDtypeStruct((r, d), F32),
        compiler_params=_cparams(("arbitrary", "arbitrary"), 56),
        name="moe_down",
    )(tile_expert, n_used, act, wd)


def _routing_tables(gates, sel, rank, counts, tm):
    ntok = gates.shape[0]
    n_tiles = (2 * ntok) // tm + N_EXPERTS
    cnt = counts[0, :N_EXPERTS].astype(jnp.int32)
    padded = ((cnt + tm - 1) // tm) * tm
    ends = jnp.cumsum(padded)
    offs = ends - padded
    selb = sel[:, :N_EXPERTS] > 0.5
    pos = offs[None, :] + rank[:, :N_EXPERTS].astype(jnp.int32)
    e_lo = jnp.argmax(selb, axis=1)
    e_hi = N_EXPERTS - 1 - jnp.argmax(selb[:, ::-1], axis=1)
    take = lambda a, e: jnp.take_along_axis(a, e[:, None], axis=1)[:, 0]
    pos0, pos1 = take(pos, e_lo), take(pos, e_hi)
    g8 = gates[:, :N_EXPERTS]
    g0, g1 = take(g8, e_lo), take(g8, e_hi)
    tok = jnp.arange(ntok, dtype=jnp.int32)
    src = jnp.zeros((n_tiles * tm,), jnp.int32).at[pos0].set(tok).at[pos1].set(tok)
    tile_start = jnp.arange(n_tiles, dtype=jnp.int32) * tm
    tile_expert = jnp.minimum(
        jnp.sum(tile_start[:, None] >= ends[None, :], axis=1), N_EXPERTS - 1).astype(jnp.int32)
    n_used = (ends[-1] // tm).astype(jnp.int32).reshape(1)
    tile_expert = jnp.where(tile_start < ends[-1], tile_expert,
                            tile_expert[jnp.maximum(n_used[0] - 1, 0)])
    pos_all = jnp.concatenate([pos0, pos1]).astype(jnp.int32)
    return src, pos_all, g0[:, None], g1[:, None], tile_expert, n_used


def _moe_final(x, g, router_w, wg, wu, wd, final_norm):
    h, gates, sel, rank, counts = _router(x, g, router_w)
    tm = min(TM_MOE, x.shape[0])
    src, pos_all, g0, g1, tile_expert, n_used = _routing_tables(gates, sel, rank, counts, tm)
    xs = _group_rows(h, src)
    act = _moe_up(xs, wg, wu, tile_expert, n_used)
    y = _moe_down(act, wd, tile_expert, n_used)
    return _combine_final_norm(x, y, pos_all, g0, g1, final_norm)


def _forward(x, norm_mix_even, w_in_even, conv3_w, w_out_even, norm_ffn_even,
             ffn_w_gate, ffn_w_up, ffn_w_down, norm_mix_odd, cf_pw1_w, cf_pw1_b,
             cf_dw_w, cf_dw_b, cf_ln_g, cf_ln_b, cf_pw2_w, cf_pw2_b, norm_moe,
             router_w, moe_w_gate, moe_w_up, moe_w_down, final_norm):
    batch, seq, d = x.shape
    xt = x.reshape(batch * seq, d)
    ncol = SC_WIDTH // LANES
    q_lo, q_hi = 3 * SC_WIDTH, 3 * SC_WIDTH + SB_HEADS * SB_HEAD_DIM
    col = jnp.arange(w_in_even.shape[2])
    col_scale = jnp.where((col >= q_lo) & (col < q_hi), -(SB_HEAD_DIM ** -0.5) * LOG2E, 1.0)
    p = _norm_matmul(xt, norm_mix_even[0], w_in_even[0], col_scale.astype(F32), BF16)
    ysb = _stick_breaking(p, batch, seq, 3 * ncol, 3 * ncol + SB_HEADS, 3 * ncol + 2 * SB_HEADS)
    xt = _mix_out_proj(p, ysb, conv3_w[0], w_out_even[0], xt, seq)
    xt = _dense_swiglu(xt, norm_ffn_even[0], ffn_w_gate[0], ffn_w_up[0], ffn_w_down[0])
    u = _norm_glu(xt, norm_mix_odd[0], cf_pw1_w[0], cf_pw1_b[0])
    v = _dwconv_ln_silu(u, cf_dw_w[0], cf_dw_b[0], cf_ln_g[0], cf_ln_b[0], seq)
    xt = _matmul_bias_res(v, cf_pw2_w[0], cf_pw2_b[0], xt)
    out = _moe_final(xt, norm_moe[0], router_w[0], moe_w_gate[0], moe_w_up[0], moe_w_down[0],
                     final_norm)
    return out.reshape(batch, seq, d)


def kernel(x, norm_mix_even, w_in_even, conv3_w, w_out_even, norm_ffn_even, ffn_w_gate, ffn_w_up, ffn_w_down, norm_mix_odd, cf_pw1_w, cf_pw1_b, cf_dw_w, cf_dw_b, cf_ln_g, cf_ln_b, cf_pw2_w, cf_pw2_b, norm_moe, router_w, moe_w_gate, moe_w_up, moe_w_down, final_norm):
    return _forward(x, norm_mix_even, w_in_even, conv3_w, w_out_even, norm_ffn_even,
                    ffn_w_gate, ffn_w_up, ffn_w_down, norm_mix_odd, cf_pw1_w, cf_pw1_b,
                    cf_dw_w, cf_dw_b, cf_ln_g, cf_ln_b, cf_pw2_w, cf_pw2_b, norm_moe,
                    router_w, moe_w_gate, moe_w_up, moe_w_down, final_norm)
```

```python
import functools

import jax
import jax.numpy as jnp
from jax import lax
from jax.experimental import pallas as pl
from jax.experimental.pallas import tpu as pltpu

F32 = jnp.float32
BF16 = jnp.bfloat16

NORM_EPS = 1e-6
SC_WIDTH = 1024
SC_KERNEL = 3
SB_HEADS = 8
SB_HEAD_DIM = 128
CF_KERNEL = 31
N_EXPERTS = 8
LANES = 128
SUBLANES = 8
MIB = 1024 * 1024

TM_PROJ = 1024
TN_PROJ = 1024
TN_GLU = 512
TM_FFN = 1024
TF_FFN = 256
TQ_ATTN = 512
LOG2E = 1.4426950408889634
TM_CONV = 256
CONV_HALO = 32
SC_HALO = 16
TM_ROUTE = 512
TM_MOE = 256
TF_MOE = 1024
TN_MOE = 512
TG_GATHER = 256
TC_COMBINE = 128


def _cparams(semantics, vmem_mib):
    return pltpu.CompilerParams(dimension_semantics=semantics,
                                vmem_limit_bytes=vmem_mib * MIB)


def _rmsnorm_to(x_ref, g_ref, h_ref, chunk=128):
    g = g_ref[...]

    def body(c, carry):
        r = pl.multiple_of(c * chunk, chunk)
        x = x_ref[pl.ds(r, chunk), :]
        ms = jnp.mean(x * x, axis=-1, keepdims=True)
        h_ref[pl.ds(r, chunk), :] = (x * lax.rsqrt(ms + NORM_EPS) * g).astype(h_ref.dtype)
        return carry

    lax.fori_loop(0, x_ref.shape[0] // chunk, body, 0)


def _bdot(a, b):
    return jnp.dot(a, b, preferred_element_type=F32)


def _norm_mm_kernel(x_ref, g_ref, w_ref, cs_ref, o_ref, h_ref):
    @pl.when(pl.program_id(1) == 0)
    def _():
        _rmsnorm_to(x_ref, g_ref, h_ref)

    o_ref[...] = (_bdot(h_ref[...], w_ref[...].astype(BF16)) * cs_ref[...]).astype(o_ref.dtype)


def _norm_matmul(x, g, w, col_scale, out_dtype):
    m, k = x.shape
    n = w.shape[1]
    tm, tn = min(TM_PROJ, m), min(TN_PROJ, n)
    return pl.pallas_call(
        _norm_mm_kernel,
        grid=(m // tm, n // tn),
        in_specs=[pl.BlockSpec((tm, k), lambda i, j: (i, 0)),
                  pl.BlockSpec((1, k), lambda i, j: (0, 0)),
                  pl.BlockSpec((k, tn), lambda i, j: (0, j)),
                  pl.BlockSpec((1, tn), lambda i, j: (0, j))],
        out_specs=pl.BlockSpec((tm, tn), lambda i, j: (i, j)),
        out_shape=jax.ShapeDtypeStruct((m, n), out_dtype),
        scratch_shapes=[pltpu.VMEM((tm, k), BF16)],
        compiler_params=_cparams(("parallel", "arbitrary"), 56),
        name="norm_in_proj",
    )(x, g.reshape(1, k), w, col_scale.reshape(1, n))


def _norm_glu_kernel(x_ref, g_ref, wa_ref, wg_ref, ba_ref, bg_ref, o_ref, h_ref):
    @pl.when(pl.program_id(1) == 0)
    def _():
        _rmsnorm_to(x_ref, g_ref, h_ref)

    h = h_ref[...]
    a = _bdot(h, wa_ref[...].astype(BF16)) + ba_ref[...]
    gate = _bdot(h, wg_ref[...].astype(BF16)) + bg_ref[...]
    o_ref[...] = (a * jax.nn.sigmoid(gate)).astype(o_ref.dtype)


def _norm_glu(x, g, w, b):
    m, k = x.shape
    n = w.shape[1] // 2
    tm, tn = min(TM_PROJ, m), min(TN_GLU, n)
    nb = n // tn
    b2 = b.reshape(1, 2 * n)
    return pl.pallas_call(
        _norm_glu_kernel,
        grid=(m // tm, nb),
        in_specs=[pl.BlockSpec((tm, k), lambda i, j: (i, 0)),
                  pl.BlockSpec((1, k), lambda i, j: (0, 0)),
                  pl.BlockSpec((k, tn), lambda i, j: (0, j)),
                  pl.BlockSpec((k, tn), lambda i, j: (0, j + nb)),
                  pl.BlockSpec((1, tn), lambda i, j: (0, j)),
                  pl.BlockSpec((1, tn), lambda i, j: (0, j + nb))],
        out_specs=pl.BlockSpec((tm, tn), lambda i, j: (i, j)),
        out_shape=jax.ShapeDtypeStruct((m, n), F32),
        scratch_shapes=[pltpu.VMEM((tm, k), BF16)],
        compiler_params=_cparams(("parallel", "arbitrary"), 56),
        name="norm_pw1_glu",
    )(x, g.reshape(1, k), w, w, b2, b2)


def _mm_bias_res_kernel(a_ref, w_ref, b_ref, r_ref, o_ref):
    o_ref[...] = r_ref[...] + _bdot(a_ref[...], w_ref[...].astype(BF16)) + b_ref[...]


def _matmul_bias_res(a, w, b, res):
    m, k = a.shape
    n = w.shape[1]
    tm, tn = min(TM_PROJ, m), min(TN_PROJ, n)
    return pl.pallas_call(
        _mm_bias_res_kernel,
        grid=(m // tm, n // tn),
        in_specs=[pl.BlockSpec((tm, k), lambda i, j: (i, 0)),
                  pl.BlockSpec((k, tn), lambda i, j: (0, j)),
                  pl.BlockSpec((1, tn), lambda i, j: (0, j)),
                  pl.BlockSpec((tm, tn), lambda i, j: (i, j))],
        out_specs=pl.BlockSpec((tm, tn), lambda i, j: (i, j)),
        out_shape=jax.ShapeDtypeStruct((m, n), F32),
        compiler_params=_cparams(("parallel", "arbitrary"), 56),
        name="pw2_residual",
    )(a, w, b.reshape(1, n), res)


def _attn_kernel(q_ref, k_ref, v_ref, o_ref, u_ref, c_ref, acc_ref, nz_ref, cm_ref, *, tq):
    hk = LANES
    tk = 2 * hk
    q0 = pl.program_id(2) * tq
    q = q_ref[...]
    c_ref[...] = jnp.zeros_like(c_ref)
    acc_ref[...] = jnp.zeros_like(acc_ref)

    jj = lax.broadcasted_iota(jnp.int32, (2 * hk, 2 * hk), 0) & (hk - 1)
    ss = lax.broadcasted_iota(jnp.int32, (2 * hk, 2 * hk), 1)
    u_ref[...] = jnp.where((ss >= hk) | (jj >= ss), 1.0, 0.0).astype(BF16)

    def half_sums(lk):
        hi = lk.astype(BF16)
        lo = (lk - hi.astype(F32)).astype(BF16)
        return _bdot(jnp.concatenate([hi, lo], axis=1), u_ref[...])

    def scores(kstart, slot, mask):
        kb = k_ref[pl.ds(kstart, tk), :]
        nz = lax.dot_general(q, kb, (((1,), (1,)), ((), ())), preferred_element_type=F32)
        log_keep = jnp.minimum(nz, 0.0) - jnp.log2(1.0 + jnp.exp2(-jnp.abs(nz)))
        if mask is not None:
            log_keep = jnp.where(mask, log_keep, 0.0)
        nz_ref[slot] = nz
        cm_ref[slot, 1] = half_sums(log_keep[:, hk:])
        cm_ref[slot, 0] = half_sums(log_keep[:, :hk])

    def weights(kstart, slot, mask):
        c = c_ref[...]
        nz = nz_ref[slot]
        cm1 = cm_ref[slot, 1]
        cm0 = cm_ref[slot, 0]
        logit1 = (c + cm1[:, :hk]) - nz[:, hk:]
        c = c + cm1[:, hk:]
        logit0 = (c + cm0[:, :hk]) - nz[:, :hk]
        c_ref[...] = c + cm0[:, hk:]
        a = jnp.exp2(jnp.concatenate([logit0, logit1], axis=1))
        if mask is not None:
            a = jnp.where(mask, a, 0.0)
        acc_ref[...] += _bdot(a.astype(BF16), v_ref[pl.ds(kstart, tk), :])

    rows = lax.broadcasted_iota(jnp.int32, (tq, tk), 0)
    cols = lax.broadcasted_iota(jnp.int32, (tq, tk), 1)
    nd = tq // tk
    diag = [(pl.multiple_of(q0 + d * tk, tk), (cols + d * tk) < rows)
            for d in reversed(range(nd))]
    nb = q0 // tk

    def below(j):
        return pl.multiple_of(q0 - (j + 1) * tk, tk)

    scores(diag[0][0], 0, diag[0][1])
    for i in range(1, nd):
        scores(diag[i][0], i % 2, diag[i][1])
        weights(diag[i - 1][0], (i - 1) % 2, diag[i - 1][1])
    last_k, last_mask = diag[nd - 1]

    @pl.when(nb == 0)
    def _():
        weights(last_k, (nd - 1) % 2, last_mask)

    @pl.when(nb > 0)
    def _():
        scores(below(0), nd % 2, None)
        weights(last_k, (nd - 1) % 2, last_mask)

        def body(j, carry):
            weights(below(j), (nd + j) % 2, None)
            scores(below(j + 1), (nd + j + 1) % 2, None)
            return carry

        lax.fori_loop(0, nb - 1, body, 0)
        weights(below(nb - 1), (nd + nb - 1) % 2, None)

    o_ref[...] = acc_ref[...].astype(o_ref.dtype)


def _stick_breaking(p, batch, seq, col_q, col_k, col_v):
    tq = min(TQ_ATTN, seq)
    nq = seq // tq
    dh = SB_HEAD_DIM
    kern = functools.partial(_attn_kernel, tq=tq)
    return pl.pallas_call(
        kern,
        grid=(batch, SB_HEADS, nq),
        in_specs=[pl.BlockSpec((tq, dh), lambda b, h, i: (b * nq + i, col_q + h)),
                  pl.BlockSpec((seq, dh), lambda b, h, i: (b, col_k + h)),
                  pl.BlockSpec((seq, dh), lambda b, h, i: (b, col_v + h))],
        out_specs=pl.BlockSpec((tq, dh), lambda b, h, i: (b * nq + i, h)),
        out_shape=jax.ShapeDtypeStruct((batch * seq, SB_HEADS * dh), BF16),
        scratch_shapes=[pltpu.VMEM((2 * LANES, 2 * LANES), BF16),
                        pltpu.VMEM((tq, LANES), F32),
                        pltpu.VMEM((tq, dh), F32),
                        pltpu.VMEM((2, tq, 2 * LANES), F32),
                        pltpu.VMEM((2, 2, tq, 2 * LANES), F32)],
        compiler_params=_cparams(("parallel", "parallel", "arbitrary"), 32),
        name="stick_breaking_attention",
    )(p, p, p)


def _mix_out_kernel(pb_ref, pc_ref, ph_ref, hc_ref, hh_ref, cw_ref, ysb_ref, w_ref, x_ref,
                    o_ref, mix_ref, u_ref, *, tm, seq, chunk):
    i = pl.program_id(0)

    @pl.when(pl.program_id(1) == 0)
    def _():
        halo = hc_ref[...].astype(F32) * hh_ref[...].astype(F32)
        u_ref[0:SC_HALO, :] = jnp.where((i * tm) % seq == 0, 0.0, halo)
        for r in range(0, tm, chunk):
            u_ref[SC_HALO + r:SC_HALO + r + chunk, :] = (
                pc_ref[r:r + chunk, :].astype(F32) * ph_ref[r:r + chunk, :].astype(F32))
        w0, w1, w2 = cw_ref[0:1, :], cw_ref[1:2, :], cw_ref[2:3, :]
        for r in range(0, tm, chunk):
            base = SC_HALO + r
            conv = (w2 * u_ref[base:base + chunk, :]
                    + w1 * u_ref[base - 1:base - 1 + chunk, :]
                    + w0 * u_ref[base - 2:base - 2 + chunk, :])
            ysc = pb_ref[r:r + chunk, :].astype(F32) * conv
            mix_ref[r:r + chunk, 0:SC_WIDTH] = ysc.astype(BF16)
        mix_ref[:, SC_WIDTH:] = ysb_ref[...]

    o_ref[...] = x_ref[...] + _bdot(mix_ref[...], w_ref[...].astype(BF16))


def _mix_out_proj(p, ysb, conv_w, w_out, x, seq):
    m = x.shape[0]
    d = w_out.shape[1]
    kdim = w_out.shape[0]
    tm, tn = min(TM_PROJ, seq), min(TN_GLU, d)
    hb = tm // SC_HALO
    kern = functools.partial(_mix_out_kernel, tm=tm, seq=seq, chunk=min(128, tm))
    return pl.pallas_call(
        kern,
        grid=(m // tm, d // tn),
        in_specs=[pl.BlockSpec((tm, SC_WIDTH), lambda i, j: (i, 0)),
                  pl.BlockSpec((tm, SC_WIDTH), lambda i, j: (i, 1)),
                  pl.BlockSpec((tm, SC_WIDTH), lambda i, j: (i, 2)),
                  pl.BlockSpec((SC_HALO, SC_WIDTH), lambda i, j: (jnp.maximum(i * hb - 1, 0), 1)),
                  pl.BlockSpec((SC_HALO, SC_WIDTH), lambda i, j: (jnp.maximum(i * hb - 1, 0), 2)),
                  pl.BlockSpec((SC_KERNEL, SC_WIDTH), lambda i, j: (0, 0)),
                  pl.BlockSpec((tm, SC_WIDTH), lambda i, j: (i, 0)),
                  pl.BlockSpec((kdim, tn), lambda i, j: (0, j)),
                  pl.BlockSpec((tm, tn), lambda i, j: (i, j))],
        out_specs=pl.BlockSpec((tm, tn), lambda i, j: (i, j)),
        out_shape=jax.ShapeDtypeStruct((m, d), F32),
        scratch_shapes=[pltpu.VMEM((tm, kdim), BF16),
                        pltpu.VMEM((tm + SC_HALO, SC_WIDTH), F32)],
        compiler_params=_cparams(("parallel", "arbitrary"), 56),
        name="shortconv_out_proj",
    )(p, p, p, p, p, conv_w, ysb, w_out, x)


def _ffn_kernel(x_ref, g_ref, wg_ref, wu_ref, wd_ref, o_ref, h_ref):
    @pl.when(pl.program_id(1) == 0)
    def _():
        _rmsnorm_to(x_ref, g_ref, h_ref)
        o_ref[...] = x_ref[...]

    h = h_ref[...]
    gate = _bdot(h, wg_ref[...].astype(BF16))
    up = _bdot(h, wu_ref[...].astype(BF16))
    act = (gate * jax.nn.sigmoid(gate) * up).astype(BF16)
    o_ref[...] += _bdot(act, wd_ref[...].astype(BF16))


def _dense_swiglu(x, g, wg, wu, wd):
    m, d = x.shape
    f = wg.shape[1]
    tm, tf = min(TM_FFN, m), min(TF_FFN, f)
    return pl.pallas_call(
        _ffn_kernel,
        grid=(m // tm, f // tf),
        in_specs=[pl.BlockSpec((tm, d), lambda i, j: (i, 0), pipeline_mode=pl.Buffered(1)),
                  pl.BlockSpec((1, d), lambda i, j: (0, 0)),
                  pl.BlockSpec((d, tf), lambda i, j: (0, j)),
                  pl.BlockSpec((d, tf), lambda i, j: (0, j)),
                  pl.BlockSpec((tf, d), lambda i, j: (j, 0))],
        out_specs=pl.BlockSpec((tm, d), lambda i, j: (i, 0)),
        out_shape=jax.ShapeDtypeStruct((m, d), F32),
        scratch_shapes=[pltpu.VMEM((tm, d), BF16)],
        compiler_params=_cparams(("parallel", "arbitrary"), 60),
        name="dense_swiglu",
    )(x, g.reshape(1, d), wg, wu, wd)


def _dwconv_ln_kernel(u_ref, halo_ref, w_ref, b_ref, lg_ref, lb_ref, o_ref, buf_ref, conv_ref,
                      *, tm, seq, rchunk):
    i = pl.program_id(0)
    buf_ref[0:CONV_HALO, :] = jnp.where((i * tm) % seq == 0, 0.0, halo_ref[...])
    buf_ref[CONV_HALO:, :] = u_ref[...]
    nlt = u_ref.shape[1] // LANES
    first = CONV_HALO - (CF_KERNEL - 1)

    def lane_tile(c, carry):
        lanes = pl.ds(pl.multiple_of(c * LANES, LANES), LANES)
        for r in range(0, tm, rchunk):
            acc = None
            for m in range(SUBLANES):
                rows = rchunk + (SUBLANES if m else 0)
                g = None
                for k in range(CF_KERNEL):
                    if (first + k) % SUBLANES != m:
                        continue
                    base = r + first + k - m
                    term = w_ref[k:k + 1, lanes] * buf_ref[base:base + rows, lanes]
                    g = term if g is None else g + term
                if m:
                    g = pltpu.roll(g, rows - m, axis=0)[:rchunk]
                acc = g if acc is None else acc + g
            conv_ref[r:r + rchunk, lanes] = acc
        return carry

    lax.fori_loop(0, nlt, lane_tile, 0)

    bias, lg, lb = b_ref[...], lg_ref[...], lb_ref[...]
    for r in range(0, tm, rchunk):
        y = conv_ref[r:r + rchunk, :] + bias
        mu = jnp.mean(y, axis=-1, keepdims=True)
        yc = y - mu
        var = jnp.mean(yc * yc, axis=-1, keepdims=True)
        t = yc * lax.rsqrt(var + NORM_EPS) * lg + lb
        o_ref[r:r + rchunk, :] = (t * jax.nn.sigmoid(t)).astype(o_ref.dtype)


def _dwconv_ln_silu(u, w, b, lg, lb, seq):
    m, c = u.shape
    tm = min(TM_CONV, seq)
    hb = tm // CONV_HALO
    kern = functools.partial(_dwconv_ln_kernel, tm=tm, seq=seq, rchunk=min(128, tm))
    return pl.pallas_call(
        kern,
        grid=(m // tm,),
        in_specs=[pl.BlockSpec((tm, c), lambda i: (i, 0)),
                  pl.BlockSpec((CONV_HALO, c), lambda i: (jnp.maximum(i * hb - 1, 0), 0)),
                  pl.BlockSpec((CF_KERNEL, c), lambda i: (0, 0)),
                  pl.BlockSpec((1, c), lambda i: (0, 0)),
                  pl.BlockSpec((1, c), lambda i: (0, 0)),
                  pl.BlockSpec((1, c), lambda i: (0, 0))],
        out_specs=pl.BlockSpec((tm, c), lambda i: (i, 0)),
        out_shape=jax.ShapeDtypeStruct((m, c), BF16),
        scratch_shapes=[pltpu.VMEM((tm + CONV_HALO, c), F32),
                        pltpu.VMEM((tm, c), F32)],
        compiler_params=_cparams(("parallel",), 32),
        name="dwconv_ln_silu",
    )(u, u, w, b.reshape(1, c), lg.reshape(1, c), lb.reshape(1, c))


def _router_kernel(x_ref, g_ref, rw_ref, h_ref, gates_ref, sel_ref, rank_ref, cnt_ref, carry_ref,
                   *, tm):
    @pl.when(pl.program_id(0) == 0)
    def _():
        carry_ref[...] = jnp.zeros_like(carry_ref)

    _rmsnorm_to(x_ref, g_ref, h_ref)
    logits = jnp.dot(h_ref[...], rw_ref[...], preferred_element_type=F32,
                     precision=lax.Precision.HIGHEST)
    lane = lax.broadcasted_iota(jnp.int32, logits.shape, 1).astype(F32)
    neg = jnp.float32(-jnp.inf)
    logits = jnp.where(lane < N_EXPERTS, logits, neg)
    m1 = jnp.max(logits, axis=-1, keepdims=True)
    i1 = jnp.min(jnp.where(logits == m1, lane, float(LANES)), axis=-1, keepdims=True)
    rest = jnp.where(lane == i1, neg, logits)
    m2 = jnp.max(rest, axis=-1, keepdims=True)
    i2 = jnp.min(jnp.where(rest == m2, lane, float(LANES)), axis=-1, keepdims=True)
    e2 = jnp.exp(m2 - m1)
    w1 = 1.0 / (1.0 + e2)
    w2 = e2 / (1.0 + e2)
    gates_ref[...] = jnp.where(lane == i1, w1, 0.0) + jnp.where(lane == i2, w2, 0.0)
    sel = jnp.where((lane == i1) | (lane == i2), 1.0, 0.0)
    sel_ref[...] = sel
    rr = lax.broadcasted_iota(jnp.int32, (tm, tm), 0)
    cc = lax.broadcasted_iota(jnp.int32, (tm, tm), 1)
    tri = jnp.where(cc < rr, 1.0, 0.0).astype(BF16)
    rank_ref[...] = _bdot(tri, sel.astype(BF16)) + carry_ref[...]
    carry_ref[...] += jnp.sum(sel, axis=0, keepdims=True)
    cnt_ref[...] = carry_ref[...]


def _router(x, g, router_w):
    m, d = x.shape
    tm = min(TM_ROUTE, m)
    rw = jnp.zeros((d, LANES), F32).at[:, :N_EXPERTS].set(router_w)
    kern = functools.partial(_router_kernel, tm=tm)
    wide = jax.ShapeDtypeStruct((m, LANES), F32)
    return pl.pallas_call(
        kern,
        grid=(m // tm,),
        in_specs=[pl.BlockSpec((tm, d), lambda i: (i, 0)),
                  pl.BlockSpec((1, d), lambda i: (0, 0)),
                  pl.BlockSpec((d, LANES), lambda i: (0, 0))],
        out_specs=[pl.BlockSpec((tm, d), lambda i: (i, 0)),
                   pl.BlockSpec((tm, LANES), lambda i: (i, 0)),
                   pl.BlockSpec((tm, LANES), lambda i: (i, 0)),
                   pl.BlockSpec((tm, LANES), lambda i: (i, 0)),
                   pl.BlockSpec((1, LANES), lambda i: (0, 0))],
        out_shape=[jax.ShapeDtypeStruct((m, d), F32), wide, wide, wide,
                   jax.ShapeDtypeStruct((1, LANES), F32)],
        scratch_shapes=[pltpu.VMEM((1, LANES), F32)],
        compiler_params=_cparams(("arbitrary",), 32),
        name="router_top2",
    )(x, g.reshape(1, d), rw)


def _start_row_gather(src_hbm, idx_ref, base, n, dst_ref, sem):
    def body(r, carry):
        t = idx_ref[base + r]
        pltpu.make_async_copy(src_hbm.at[pl.ds(t, 1), :], dst_ref.at[pl.ds(r, 1), :], sem).start()
        return carry

    lax.fori_loop(0, n, body, 0, unroll=8)


def _wait_row_gather(src_hbm, n, dst_ref, sem):
    def body(r, carry):
        pltpu.make_async_copy(src_hbm.at[pl.ds(0, 1), :], dst_ref.at[pl.ds(r, 1), :], sem).wait()
        return carry

    lax.fori_loop(0, n, body, 0, unroll=8)


def _group_rows_kernel(src_ref, h_hbm, o_ref, buf_ref, sem_ref, *, tg):
    i = pl.program_id(0)
    n = pl.num_programs(0)
    slot = i % 2

    @pl.when(i == 0)
    def _():
        _start_row_gather(h_hbm, src_ref, 0, tg, buf_ref.at[0], sem_ref.at[0])

    @pl.when(i + 1 < n)
    def _():
        _start_row_gather(h_hbm, src_ref, (i + 1) * tg, tg, buf_ref.at[1 - slot], sem_ref.at[1 - slot])

    _wait_row_gather(h_hbm, tg, buf_ref.at[slot], sem_ref.at[slot])
    o_ref[...] = buf_ref[slot].astype(o_ref.dtype)


def _group_rows(h, src):
    d = h.shape[1]
    r = src.shape[0]
    tg = min(TG_GATHER, r)
    kern = functools.partial(_group_rows_kernel, tg=tg)
    return pl.pallas_call(
        kern,
        grid_spec=pltpu.PrefetchScalarGridSpec(
            num_scalar_prefetch=1,
            grid=(r // tg,),
            in_specs=[pl.BlockSpec(memory_space=pl.ANY)],
            out_specs=pl.BlockSpec((tg, d), lambda i, s: (i, 0)),
            scratch_shapes=[pltpu.VMEM((2, tg, d), F32),
                            pltpu.SemaphoreType.DMA((2,))]),
        out_shape=jax.ShapeDtypeStruct((r, d), BF16),
        compiler_params=_cparams(("arbitrary",), 32),
        name="group_rows_by_expert",
    )(src, h)


def _combine_kernel(pos_ref, x_ref, g0_ref, g1_ref, fn_ref, y_hbm, o_ref, buf_ref, sem_ref,
                    *, tc, ntok):
    i = pl.program_id(0)
    n = pl.num_programs(0)
    slot = i % 2

    def start(step, s):
        _start_row_gather(y_hbm, pos_ref, step * tc, tc, buf_ref.at[s, 0], sem_ref.at[s, 0])
        _start_row_gather(y_hbm, pos_ref, ntok + step * tc, tc, buf_ref.at[s, 1], sem_ref.at[s, 1])

    @pl.when(i == 0)
    def _():
        start(0, 0)

    @pl.when(i + 1 < n)
    def _():
        start(i + 1, 1 - slot)

    _wait_row_gather(y_hbm, tc, buf_ref.at[slot, 0], sem_ref.at[slot, 0])
    _wait_row_gather(y_hbm, tc, buf_ref.at[slot, 1], sem_ref.at[slot, 1])
    x = x_ref[...] + g0_ref[...] * buf_ref[slot, 0] + g1_ref[...] * buf_ref[slot, 1]
    ms = jnp.mean(x * x, axis=-1, keepdims=True)
    o_ref[...] = x * lax.rsqrt(ms + NORM_EPS) * fn_ref[...]


def _combine_final_norm(x, y, pos, g0, g1, final_norm):
    m, d = x.shape
    tc = min(TC_COMBINE, m)
    kern = functools.partial(_combine_kernel, tc=tc, ntok=m)
    return pl.pallas_call(
        kern,
        grid_spec=pltpu.PrefetchScalarGridSpec(
            num_scalar_prefetch=1,
            grid=(m // tc,),
            in_specs=[pl.BlockSpec((tc, d), lambda i, s: (i, 0)),
                      pl.BlockSpec((tc, 1), lambda i, s: (i, 0)),
                      pl.BlockSpec((tc, 1), lambda i, s: (i, 0)),
                      pl.BlockSpec((1, d), lambda i, s: (0, 0)),
                      pl.BlockSpec(memory_space=pl.ANY)],
            out_specs=pl.BlockSpec((tc, d), lambda i, s: (i, 0)),
            scratch_shapes=[pltpu.VMEM((2, 2, tc, d), F32),
                            pltpu.SemaphoreType.DMA((2, 2))]),
        out_shape=jax.ShapeDtypeStruct((m, d), F32),
        compiler_params=_cparams(("arbitrary",), 32),
        name="moe_combine_final_norm",
    )(pos, x, g0, g1, final_norm.reshape(1, d), y)


def _tile_pipeline(off_ref, e, n_experts, n_tiles, in_copy, out_copy, compute, zero_out):
    t0 = off_ref[e]
    n = off_ref[e + 1] - t0

    @pl.when(n > 0)
    def _():
        in_copy(t0, 0).start()

        def body(i, carry):
            slot = i % 2
            t = t0 + i

            @pl.when(i + 1 < n)
            def _():
                in_copy(t + 1, 1 - slot).start()

            in_copy(t, slot).wait()

            @pl.when(i >= 2)
            def _():
                out_copy(t - 2, slot).wait()

            compute(slot)
            out_copy(t, slot).start()
            return carry

        lax.fori_loop(0, n, body, 0)

        @pl.when(n >= 2)
        def _():
            out_copy(t0 + n - 2, n % 2).wait()

        out_copy(t0 + n - 1, (n - 1) % 2).wait()

    @pl.when(e == n_experts - 1)
    def _():
        zero_out()

        def tail(t, carry):
            out_copy(t, 0).start()
            out_copy(t, 0).wait()
            return carry

        lax.fori_loop(off_ref[n_experts], n_tiles, tail, 0)


def _moe_up_kernel(off_ref, xs_hbm, wg_ref, wu_ref, act_hbm, wgb_ref, wub_ref, xbuf, obuf,
                   xsem, osem, *, tm, tf, n_tiles):
    j = pl.program_id(0)
    e = pl.program_id(1)
    wgb_ref[...] = wg_ref[0].astype(BF16)
    wub_ref[...] = wu_ref[0].astype(BF16)

    def in_copy(t, slot):
        return pltpu.make_async_copy(xs_hbm.at[pl.ds(t * tm, tm), :], xbuf.at[slot], xsem.at[slot])

    def out_copy(t, slot):
        return pltpu.make_async_copy(
            obuf.at[slot], act_hbm.at[pl.ds(t * tm, tm), pl.ds(j * tf, tf)], osem.at[slot])

    def compute(slot):
        x = xbuf[slot]
        gate = _bdot(x, wgb_ref[...])
        up = _bdot(x, wub_ref[...])
        obuf[slot] = (gate * jax.nn.sigmoid(gate) * up).astype(obuf.dtype)

    def zero_out():
        obuf[0] = jnp.zeros(obuf.shape[1:], obuf.dtype)

    _tile_pipeline(off_ref, e, N_EXPERTS, n_tiles, in_copy, out_copy, compute, zero_out)


def _moe_up(xs, wg, wu, tile_off, tm):
    r, d = xs.shape
    f = wg.shape[2]
    tf = min(TF_MOE, f)
    kern = functools.partial(_moe_up_kernel, tm=tm, tf=tf, n_tiles=r // tm)
    return pl.pallas_call(
        kern,
        grid_spec=pltpu.PrefetchScalarGridSpec(
            num_scalar_prefetch=1,
            grid=(f // tf, N_EXPERTS),
            in_specs=[pl.BlockSpec(memory_space=pl.ANY),
                      pl.BlockSpec((1, d, tf), lambda j, e, off: (e, 0, j)),
                      pl.BlockSpec((1, d, tf), lambda j, e, off: (e, 0, j))],
            out_specs=pl.BlockSpec(memory_space=pl.ANY),
            scratch_shapes=[pltpu.VMEM((d, tf), BF16), pltpu.VMEM((d, tf), BF16),
                            pltpu.VMEM((2, tm, d), BF16), pltpu.VMEM((2, tm, tf), BF16),
                            pltpu.SemaphoreType.DMA((2,)), pltpu.SemaphoreType.DMA((2,))]),
        out_shape=jax.ShapeDtypeStruct((r, f), BF16),
        compiler_params=_cparams(("arbitrary", "arbitrary"), 56),
        name="moe_gate_up",
    )(tile_off, xs, wg, wu)


def _moe_down_kernel(off_ref, act_hbm, wd_ref, y_hbm, wdb_ref, abuf, obuf, asem, osem,
                     *, tm, tn, n_tiles):
    j = pl.program_id(0)
    e = pl.program_id(1)
    wdb_ref[...] = wd_ref[0].astype(BF16)

    def in_copy(t, slot):
        return pltpu.make_async_copy(act_hbm.at[pl.ds(t * tm, tm), :], abuf.at[slot], asem.at[slot])

    def out_copy(t, slot):
        return pltpu.make_async_copy(
            obuf.at[slot], y_hbm.at[pl.ds(t * tm, tm), pl.ds(j * tn, tn)], osem.at[slot])

    def compute(slot):
        obuf[slot] = _bdot(abuf[slot], wdb_ref[...])

    def zero_out():
        obuf[0] = jnp.zeros(obuf.shape[1:], obuf.dtype)

    _tile_pipeline(off_ref, e, N_EXPERTS, n_tiles, in_copy, out_copy, compute, zero_out)


def _moe_down(act, wd, tile_off, tm):
    r, f = act.shape
    d = wd.shape[2]
    tn = min(TN_MOE, d)
    kern = functools.partial(_moe_down_kernel, tm=tm, tn=tn, n_tiles=r // tm)
    return pl.pallas_call(
        kern,
        grid_spec=pltpu.PrefetchScalarGridSpec(
            num_scalar_prefetch=1,
            grid=(d // tn, N_EXPERTS),
            in_specs=[pl.BlockSpec(memory_space=pl.ANY),
                      pl.BlockSpec((1, f, tn), lambda j, e, off: (e, 0, j))],
            out_specs=pl.BlockSpec(memory_space=pl.ANY),
            scratch_shapes=[pltpu.VMEM((f, tn), BF16),
                            pltpu.VMEM((2, tm, f), BF16), pltpu.VMEM((2, tm, tn), F32),
                            pltpu.SemaphoreType.DMA((2,)), pltpu.SemaphoreType.DMA((2,))]),
        out_shape=jax.ShapeDtypeStruct((r, d), F32),
        compiler_params=_cparams(("arbitrary", "arbitrary"), 56),
        name="moe_down",
    )(tile_off, act, wd)


def _routing_tables(gates, sel, rank, counts, tm):
    ntok = gates.shape[0]
    n_tiles = (2 * ntok) // tm + N_EXPERTS
    cnt = counts[0, :N_EXPERTS].astype(jnp.int32)
    padded = ((cnt + tm - 1) // tm) * tm
    ends = jnp.cumsum(padded)
    offs = ends - padded
    selb = sel[:, :N_EXPERTS] > 0.5
    pos = offs[None, :] + rank[:, :N_EXPERTS].astype(jnp.int32)
    e_lo = jnp.argmax(selb, axis=1)
    e_hi = N_EXPERTS - 1 - jnp.argmax(selb[:, ::-1], axis=1)
    take = lambda a, e: jnp.take_along_axis(a, e[:, None], axis=1)[:, 0]
    pos0, pos1 = take(pos, e_lo), take(pos, e_hi)
    g8 = gates[:, :N_EXPERTS]
    g0, g1 = take(g8, e_lo), take(g8, e_hi)
    tok = jnp.arange(ntok, dtype=jnp.int32)
    pos_all = jnp.concatenate([pos0, pos1]).astype(jnp.int32)
    src = jnp.zeros((n_tiles * tm,), jnp.int32).at[pos_all].set(
        jnp.concatenate([tok, tok]), unique_indices=True)
    tile_off = jnp.concatenate([jnp.zeros((1,), jnp.int32), (ends // tm).astype(jnp.int32)])
    return src, pos_all, g0[:, None], g1[:, None], tile_off


def _moe_final(x, g, router_w, wg, wu, wd, final_norm):
    h, gates, sel, rank, counts = _router(x, g, router_w)
    tm = min(TM_MOE, x.shape[0])
    src, pos_all, g0, g1, tile_off = _routing_tables(gates, sel, rank, counts, tm)
    xs = _group_rows(h, src)
    act = _moe_up(xs, wg, wu, tile_off, tm)
    y = _moe_down(act, wd, tile_off, tm)
    return _combine_final_norm(x, y, pos_all, g0, g1, final_norm)


def _forward(x, norm_mix_even, w_in_even, conv3_w, w_out_even, norm_ffn_even,
             ffn_w_gate, ffn_w_up, ffn_w_down, norm_mix_odd, cf_pw1_w, cf_pw1_b,
             cf_dw_w, cf_dw_b, cf_ln_g, cf_ln_b, cf_pw2_w, cf_pw2_b, norm_moe,
             router_w, moe_w_gate, moe_w_up, moe_w_down, final_norm):
    batch, seq, d = x.shape
    xt = x.reshape(batch * seq, d)
    ncol = SC_WIDTH // LANES
    q_lo, q_hi = 3 * SC_WIDTH, 3 * SC_WIDTH + SB_HEADS * SB_HEAD_DIM
    col = jnp.arange(w_in_even.shape[2])
    col_scale = jnp.where((col >= q_lo) & (col < q_hi), -(SB_HEAD_DIM ** -0.5) * LOG2E, 1.0)
    p = _norm_matmul(xt, norm_mix_even[0], w_in_even[0], col_scale.astype(F32), BF16)
    ysb = _stick_breaking(p, batch, seq, 3 * ncol, 3 * ncol + SB_HEADS, 3 * ncol + 2 * SB_HEADS)
    xt = _mix_out_proj(p, ysb, conv3_w[0], w_out_even[0], xt, seq)
    xt = _dense_swiglu(xt, norm_ffn_even[0], ffn_w_gate[0], ffn_w_up[0], ffn_w_down[0])
    u = _norm_glu(xt, norm_mix_odd[0], cf_pw1_w[0], cf_pw1_b[0])
    v = _dwconv_ln_silu(u, cf_dw_w[0], cf_dw_b[0], cf_ln_g[0], cf_ln_b[0], seq)
    xt = _matmul_bias_res(v, cf_pw2_w[0], cf_pw2_b[0], xt)
    out = _moe_final(xt, norm_moe[0], router_w[0], moe_w_gate[0], moe_w_up[0], moe_w_down[0],
                     final_norm)
    return out.reshape(batch, seq, d)


def kernel(x, norm_mix_even, w_in_even, conv3_w, w_out_even, norm_ffn_even, ffn_w_gate, ffn_w_up, ffn_w_down, norm_mix_odd, cf_pw1_w, cf_pw1_b, cf_dw_w, cf_dw_b, cf_ln_g, cf_ln_b, cf_pw2_w, cf_pw2_b, norm_moe, router_w, moe_w_gate, moe_w_up, moe_w_down, final_norm):
    return _forward(x, norm_mix_even, w_in_even, conv3_w, w_out_even, norm_ffn_even,
                    ffn_w_gate, ffn_w_up, ffn_w_down, norm_mix_odd, cf_pw1_w, cf_pw1_b,
                    cf_dw_w, cf_dw_b, cf_ln_g, cf_ln_b, cf_pw2_w, cf_pw2_b, norm_moe,
                    router_w, moe_w_gate, moe_w_up, moe_w_down, final_norm)
```

```python
import functools

import jax
import jax.numpy as jnp
from jax import lax
from jax.experimental import pallas as pl
from jax.experimental.pallas import tpu as pltpu

F32 = jnp.float32
BF16 = jnp.bfloat16

NORM_EPS = 1e-6
SC_WIDTH = 1024
SC_KERNEL = 3
SB_HEADS = 8
SB_HEAD_DIM = 128
CF_KERNEL = 31
N_EXPERTS = 8
LANES = 128
SUBLANES = 8
MIB = 1024 * 1024

TM_PROJ = 1024
TN_PROJ = 1024
TN_GLU = 512
TM_FFN = 1024
TF_FFN = 256
TQ_ATTN = 512
LOG2E = 1.4426950408889634
TM_CONV = 256
CONV_HALO = 32
SC_HALO = 16
TM_ROUTE = 512
TM_MOE = 256
TF_MOE = 1024
TN_MOE = 512
IN_RING = 3
OUT_RING = 2
TG_GATHER = 256
TC_COMBINE = 128


def _cparams(semantics, vmem_mib):
    return pltpu.CompilerParams(dimension_semantics=semantics,
                                vmem_limit_bytes=vmem_mib * MIB)


def _rmsnorm_to(x_ref, g_ref, h_ref, chunk=128):
    g = g_ref[...]

    def body(c, carry):
        r = pl.multiple_of(c * chunk, chunk)
        x = x_ref[pl.ds(r, chunk), :]
        ms = jnp.mean(x * x, axis=-1, keepdims=True)
        h_ref[pl.ds(r, chunk), :] = (x * lax.rsqrt(ms + NORM_EPS) * g).astype(h_ref.dtype)
        return carry

    lax.fori_loop(0, x_ref.shape[0] // chunk, body, 0)


def _bdot(a, b):
    return jnp.dot(a, b, preferred_element_type=F32)


def _norm_mm_kernel(x_ref, g_ref, w_ref, cs_ref, o_ref, h_ref):
    @pl.when(pl.program_id(1) == 0)
    def _():
        _rmsnorm_to(x_ref, g_ref, h_ref)

    o_ref[...] = (_bdot(h_ref[...], w_ref[...].astype(BF16)) * cs_ref[...]).astype(o_ref.dtype)


def _norm_matmul(x, g, w, col_scale, out_dtype):
    m, k = x.shape
    n = w.shape[1]
    tm, tn = min(TM_PROJ, m), min(TN_PROJ, n)
    return pl.pallas_call(
        _norm_mm_kernel,
        grid=(m // tm, n // tn),
        in_specs=[pl.BlockSpec((tm, k), lambda i, j: (i, 0)),
                  pl.BlockSpec((1, k), lambda i, j: (0, 0)),
                  pl.BlockSpec((k, tn), lambda i, j: (0, j)),
                  pl.BlockSpec((1, tn), lambda i, j: (0, j))],
        out_specs=pl.BlockSpec((tm, tn), lambda i, j: (i, j)),
        out_shape=jax.ShapeDtypeStruct((m, n), out_dtype),
        scratch_shapes=[pltpu.VMEM((tm, k), BF16)],
        compiler_params=_cparams(("parallel", "arbitrary"), 56),
        name="norm_in_proj",
    )(x, g.reshape(1, k), w, col_scale.reshape(1, n))


def _norm_glu_kernel(x_ref, g_ref, wa_ref, wg_ref, ba_ref, bg_ref, o_ref, h_ref):
    @pl.when(pl.program_id(1) == 0)
    def _():
        _rmsnorm_to(x_ref, g_ref, h_ref)

    h = h_ref[...]
    a = _bdot(h, wa_ref[...].astype(BF16)) + ba_ref[...]
    gate = _bdot(h, wg_ref[...].astype(BF16)) + bg_ref[...]
    o_ref[...] = (a * jax.nn.sigmoid(gate)).astype(o_ref.dtype)


def _norm_glu(x, g, w, b):
    m, k = x.shape
    n = w.shape[1] // 2
    tm, tn = min(TM_PROJ, m), min(TN_GLU, n)
    nb = n // tn
    b2 = b.reshape(1, 2 * n)
    return pl.pallas_call(
        _norm_glu_kernel,
        grid=(m // tm, nb),
        in_specs=[pl.BlockSpec((tm, k), lambda i, j: (i, 0)),
                  pl.BlockSpec((1, k), lambda i, j: (0, 0)),
                  pl.BlockSpec((k, tn), lambda i, j: (0, j)),
                  pl.BlockSpec((k, tn), lambda i, j: (0, j + nb)),
                  pl.BlockSpec((1, tn), lambda i, j: (0, j)),
                  pl.BlockSpec((1, tn), lambda i, j: (0, j + nb))],
        out_specs=pl.BlockSpec((tm, tn), lambda i, j: (i, j)),
        out_shape=jax.ShapeDtypeStruct((m, n), F32),
        scratch_shapes=[pltpu.VMEM((tm, k), BF16)],
        compiler_params=_cparams(("parallel", "arbitrary"), 56),
        name="norm_pw1_glu",
    )(x, g.reshape(1, k), w, w, b2, b2)


def _mm_bias_res_kernel(a_ref, w_ref, b_ref, r_ref, o_ref):
    o_ref[...] = r_ref[...] + _bdot(a_ref[...], w_ref[...].astype(BF16)) + b_ref[...]


def _matmul_bias_res(a, w, b, res):
    m, k = a.shape
    n = w.shape[1]
    tm, tn = min(TM_PROJ, m), min(TN_PROJ, n)
    return pl.pallas_call(
        _mm_bias_res_kernel,
        grid=(m // tm, n // tn),
        in_specs=[pl.BlockSpec((tm, k), lambda i, j: (i, 0)),
                  pl.BlockSpec((k, tn), lambda i, j: (0, j)),
                  pl.BlockSpec((1, tn), lambda i, j: (0, j)),
                  pl.BlockSpec((tm, tn), lambda i, j: (i, j))],
        out_specs=pl.BlockSpec((tm, tn), lambda i, j: (i, j)),
        out_shape=jax.ShapeDtypeStruct((m, n), F32),
        compiler_params=_cparams(("parallel", "arbitrary"), 56),
        name="pw2_residual",
    )(a, w, b.reshape(1, n), res)


def _attn_kernel(q_ref, k_ref, v_ref, o_ref, u_ref, c_ref, acc_ref, nz_ref, cm_ref, *, tq):
    hk = LANES
    tk = 2 * hk
    q0 = pl.program_id(2) * tq
    q = q_ref[...]
    c_ref[...] = jnp.zeros_like(c_ref)
    acc_ref[...] = jnp.zeros_like(acc_ref)

    jj = lax.broadcasted_iota(jnp.int32, (2 * hk, 2 * hk), 0) & (hk - 1)
    ss = lax.broadcasted_iota(jnp.int32, (2 * hk, 2 * hk), 1)
    u_ref[...] = jnp.where((ss >= hk) | (jj >= ss), 1.0, 0.0).astype(BF16)

    def half_sums(lk):
        hi = lk.astype(BF16)
        lo = (lk - hi.astype(F32)).astype(BF16)
        return _bdot(jnp.concatenate([hi, lo], axis=1), u_ref[...])

    def scores(kstart, slot, mask):
        kb = k_ref[pl.ds(kstart, tk), :]
        nz = lax.dot_general(q, kb, (((1,), (1,)), ((), ())), preferred_element_type=F32)
        log_keep = jnp.minimum(nz, 0.0) - jnp.log2(1.0 + jnp.exp2(-jnp.abs(nz)))
        if mask is not None:
            log_keep = jnp.where(mask, log_keep, 0.0)
        nz_ref[slot] = nz
        cm_ref[slot, 1] = half_sums(log_keep[:, hk:])
        cm_ref[slot, 0] = half_sums(log_keep[:, :hk])

    def weights(kstart, slot, mask):
        c = c_ref[...]
        nz = nz_ref[slot]
        cm1 = cm_ref[slot, 1]
        cm0 = cm_ref[slot, 0]
        logit1 = (c + cm1[:, :hk]) - nz[:, hk:]
        c = c + cm1[:, hk:]
        logit0 = (c + cm0[:, :hk]) - nz[:, :hk]
        c_ref[...] = c + cm0[:, hk:]
        a = jnp.exp2(jnp.concatenate([logit0, logit1], axis=1))
        if mask is not None:
            a = jnp.where(mask, a, 0.0)
        acc_ref[...] += _bdot(a.astype(BF16), v_ref[pl.ds(kstart, tk), :])

    rows = lax.broadcasted_iota(jnp.int32, (tq, tk), 0)
    cols = lax.broadcasted_iota(jnp.int32, (tq, tk), 1)
    nd = tq // tk
    diag = [(pl.multiple_of(q0 + d * tk, tk), (cols + d * tk) < rows)
            for d in reversed(range(nd))]
    nb = q0 // tk

    def below(j):
        return pl.multiple_of(q0 - (j + 1) * tk, tk)

    scores(diag[0][0], 0, diag[0][1])
    for i in range(1, nd):
        scores(diag[i][0], i % 2, diag[i][1])
        weights(diag[i - 1][0], (i - 1) % 2, diag[i - 1][1])
    last_k, last_mask = diag[nd - 1]

    @pl.when(nb == 0)
    def _():
        weights(last_k, (nd - 1) % 2, last_mask)

    @pl.when(nb > 0)
    def _():
        scores(below(0), nd % 2, None)
        weights(last_k, (nd - 1) % 2, last_mask)

        def body(j, carry):
            weights(below(j), (nd + j) % 2, None)
            scores(below(j + 1), (nd + j + 1) % 2, None)
            return carry

        lax.fori_loop(0, nb - 1, body, 0)
        weights(below(nb - 1), (nd + nb - 1) % 2, None)

    o_ref[...] = acc_ref[...].astype(o_ref.dtype)


def _stick_breaking(p, batch, seq, col_q, col_k, col_v):
    tq = min(TQ_ATTN, seq)
    nq = seq // tq
    dh = SB_HEAD_DIM
    kern = functools.partial(_attn_kernel, tq=tq)
    return pl.pallas_call(
        kern,
        grid=(batch, SB_HEADS, nq),
        in_specs=[pl.BlockSpec((tq, dh), lambda b, h, i: (b * nq + i, col_q + h)),
                  pl.BlockSpec((seq, dh), lambda b, h, i: (b, col_k + h)),
                  pl.BlockSpec((seq, dh), lambda b, h, i: (b, col_v + h))],
        out_specs=pl.BlockSpec((tq, dh), lambda b, h, i: (b * nq + i, h)),
        out_shape=jax.ShapeDtypeStruct((batch * seq, SB_HEADS * dh), BF16),
        scratch_shapes=[pltpu.VMEM((2 * LANES, 2 * LANES), BF16),
                        pltpu.VMEM((tq, LANES), F32),
                        pltpu.VMEM((tq, dh), F32),
                        pltpu.VMEM((2, tq, 2 * LANES), F32),
                        pltpu.VMEM((2, 2, tq, 2 * LANES), F32)],
        compiler_params=_cparams(("parallel", "parallel", "arbitrary"), 32),
        name="stick_breaking_attention",
    )(p, p, p)


def _mix_out_kernel(pb_ref, pc_ref, ph_ref, hc_ref, hh_ref, cw_ref, ysb_ref, w_ref, x_ref,
                    o_ref, mix_ref, u_ref, *, tm, seq, chunk):
    i = pl.program_id(0)

    @pl.when(pl.program_id(1) == 0)
    def _():
        halo = hc_ref[...].astype(F32) * hh_ref[...].astype(F32)
        u_ref[0:SC_HALO, :] = jnp.where((i * tm) % seq == 0, 0.0, halo)
        for r in range(0, tm, chunk):
            u_ref[SC_HALO + r:SC_HALO + r + chunk, :] = (
                pc_ref[r:r + chunk, :].astype(F32) * ph_ref[r:r + chunk, :].astype(F32))
        w0, w1, w2 = cw_ref[0:1, :], cw_ref[1:2, :], cw_ref[2:3, :]
        for r in range(0, tm, chunk):
            base = SC_HALO + r
            conv = (w2 * u_ref[base:base + chunk, :]
                    + w1 * u_ref[base - 1:base - 1 + chunk, :]
                    + w0 * u_ref[base - 2:base - 2 + chunk, :])
            ysc = pb_ref[r:r + chunk, :].astype(F32) * conv
            mix_ref[r:r + chunk, 0:SC_WIDTH] = ysc.astype(BF16)
        mix_ref[:, SC_WIDTH:] = ysb_ref[...]

    o_ref[...] = x_ref[...] + _bdot(mix_ref[...], w_ref[...].astype(BF16))


def _mix_out_proj(p, ysb, conv_w, w_out, x, seq):
    m = x.shape[0]
    d = w_out.shape[1]
    kdim = w_out.shape[0]
    tm, tn = min(TM_PROJ, seq), min(TN_GLU, d)
    hb = tm // SC_HALO
    kern = functools.partial(_mix_out_kernel, tm=tm, seq=seq, chunk=min(128, tm))
    return pl.pallas_call(
        kern,
        grid=(m // tm, d // tn),
        in_specs=[pl.BlockSpec((tm, SC_WIDTH), lambda i, j: (i, 0)),
                  pl.BlockSpec((tm, SC_WIDTH), lambda i, j: (i, 1)),
                  pl.BlockSpec((tm, SC_WIDTH), lambda i, j: (i, 2)),
                  pl.BlockSpec((SC_HALO, SC_WIDTH), lambda i, j: (jnp.maximum(i * hb - 1, 0), 1)),
                  pl.BlockSpec((SC_HALO, SC_WIDTH), lambda i, j: (jnp.maximum(i * hb - 1, 0), 2)),
                  pl.BlockSpec((SC_KERNEL, SC_WIDTH), lambda i, j: (0, 0)),
                  pl.BlockSpec((tm, SC_WIDTH), lambda i, j: (i, 0)),
                  pl.BlockSpec((kdim, tn), lambda i, j: (0, j)),
                  pl.BlockSpec((tm, tn), lambda i, j: (i, j))],
        out_specs=pl.BlockSpec((tm, tn), lambda i, j: (i, j)),
        out_shape=jax.ShapeDtypeStruct((m, d), F32),
        scratch_shapes=[pltpu.VMEM((tm, kdim), BF16),
                        pltpu.VMEM((tm + SC_HALO, SC_WIDTH), F32)],
        compiler_params=_cparams(("parallel", "arbitrary"), 56),
        name="shortconv_out_proj",
    )(p, p, p, p, p, conv_w, ysb, w_out, x)


def _ffn_kernel(x_ref, g_ref, wg_ref, wu_ref, wd_ref, o_ref, h_ref):
    @pl.when(pl.program_id(1) == 0)
    def _():
        _rmsnorm_to(x_ref, g_ref, h_ref)
        o_ref[...] = x_ref[...]

    h = h_ref[...]
    gate = _bdot(h, wg_ref[...].astype(BF16))
    up = _bdot(h, wu_ref[...].astype(BF16))
    act = (gate * jax.nn.sigmoid(gate) * up).astype(BF16)
    o_ref[...] += _bdot(act, wd_ref[...].astype(BF16))


def _dense_swiglu(x, g, wg, wu, wd):
    m, d = x.shape
    f = wg.shape[1]
    tm, tf = min(TM_FFN, m), min(TF_FFN, f)
    return pl.pallas_call(
        _ffn_kernel,
        grid=(m // tm, f // tf),
        in_specs=[pl.BlockSpec((tm, d), lambda i, j: (i, 0), pipeline_mode=pl.Buffered(1)),
                  pl.BlockSpec((1, d), lambda i, j: (0, 0)),
                  pl.BlockSpec((d, tf), lambda i, j: (0, j)),
                  pl.BlockSpec((d, tf), lambda i, j: (0, j)),
                  pl.BlockSpec((tf, d), lambda i, j: (j, 0))],
        out_specs=pl.BlockSpec((tm, d), lambda i, j: (i, 0)),
        out_shape=jax.ShapeDtypeStruct((m, d), F32),
        scratch_shapes=[pltpu.VMEM((tm, d), BF16)],
        compiler_params=_cparams(("parallel", "arbitrary"), 60),
        name="dense_swiglu",
    )(x, g.reshape(1, d), wg, wu, wd)


def _dwconv_ln_kernel(u_ref, halo_ref, w_ref, b_ref, lg_ref, lb_ref, o_ref, buf_ref, conv_ref,
                      *, tm, seq, rchunk):
    i = pl.program_id(0)
    buf_ref[0:CONV_HALO, :] = jnp.where((i * tm) % seq == 0, 0.0, halo_ref[...])
    buf_ref[CONV_HALO:, :] = u_ref[...]
    nlt = u_ref.shape[1] // LANES
    first = CONV_HALO - (CF_KERNEL - 1)

    def lane_tile(c, carry):
        lanes = pl.ds(pl.multiple_of(c * LANES, LANES), LANES)
        for r in range(0, tm, rchunk):
            acc = None
            for m in range(SUBLANES):
                rows = rchunk + (SUBLANES if m else 0)
                g = None
                for k in range(CF_KERNEL):
                    if (first + k) % SUBLANES != m:
                        continue
                    base = r + first + k - m
                    term = w_ref[k:k + 1, lanes] * buf_ref[base:base + rows, lanes]
                    g = term if g is None else g + term
                if m:
                    g = pltpu.roll(g, rows - m, axis=0)[:rchunk]
                acc = g if acc is None else acc + g
            conv_ref[r:r + rchunk, lanes] = acc
        return carry

    lax.fori_loop(0, nlt, lane_tile, 0)

    bias, lg, lb = b_ref[...], lg_ref[...], lb_ref[...]
    for r in range(0, tm, rchunk):
        y = conv_ref[r:r + rchunk, :] + bias
        mu = jnp.mean(y, axis=-1, keepdims=True)
        yc = y - mu
        var = jnp.mean(yc * yc, axis=-1, keepdims=True)
        t = yc * lax.rsqrt(var + NORM_EPS) * lg + lb
        o_ref[r:r + rchunk, :] = (t * jax.nn.sigmoid(t)).astype(o_ref.dtype)


def _dwconv_ln_silu(u, w, b, lg, lb, seq):
    m, c = u.shape
    tm = min(TM_CONV, seq)
    hb = tm // CONV_HALO
    kern = functools.partial(_dwconv_ln_kernel, tm=tm, seq=seq, rchunk=min(128, tm))
    return pl.pallas_call(
        kern,
        grid=(m // tm,),
        in_specs=[pl.BlockSpec((tm, c), lambda i: (i, 0)),
                  pl.BlockSpec((CONV_HALO, c), lambda i: (jnp.maximum(i * hb - 1, 0), 0)),
                  pl.BlockSpec((CF_KERNEL, c), lambda i: (0, 0)),
                  pl.BlockSpec((1, c), lambda i: (0, 0)),
                  pl.BlockSpec((1, c), lambda i: (0, 0)),
                  pl.BlockSpec((1, c), lambda i: (0, 0))],
        out_specs=pl.BlockSpec((tm, c), lambda i: (i, 0)),
        out_shape=jax.ShapeDtypeStruct((m, c), BF16),
        scratch_shapes=[pltpu.VMEM((tm + CONV_HALO, c), F32),
                        pltpu.VMEM((tm, c), F32)],
        compiler_params=_cparams(("parallel",), 32),
        name="dwconv_ln_silu",
    )(u, u, w, b.reshape(1, c), lg.reshape(1, c), lb.reshape(1, c))


def _router_kernel(x_ref, g_ref, rw_ref, h_ref, gates_ref, sel_ref, rank_ref, cnt_ref, carry_ref,
                   *, tm):
    @pl.when(pl.program_id(0) == 0)
    def _():
        carry_ref[...] = jnp.zeros_like(carry_ref)

    _rmsnorm_to(x_ref, g_ref, h_ref)
    logits = jnp.dot(h_ref[...], rw_ref[...], preferred_element_type=F32,
                     precision=lax.Precision.HIGHEST)
    lane = lax.broadcasted_iota(jnp.int32, logits.shape, 1).astype(F32)
    neg = jnp.float32(-jnp.inf)
    logits = jnp.where(lane < N_EXPERTS, logits, neg)
    m1 = jnp.max(logits, axis=-1, keepdims=True)
    i1 = jnp.min(jnp.where(logits == m1, lane, float(LANES)), axis=-1, keepdims=True)
    rest = jnp.where(lane == i1, neg, logits)
    m2 = jnp.max(rest, axis=-1, keepdims=True)
    i2 = jnp.min(jnp.where(rest == m2, lane, float(LANES)), axis=-1, keepdims=True)
    e2 = jnp.exp(m2 - m1)
    w1 = 1.0 / (1.0 + e2)
    w2 = e2 / (1.0 + e2)
    gates_ref[...] = jnp.where(lane == i1, w1, 0.0) + jnp.where(lane == i2, w2, 0.0)
    sel = jnp.where((lane == i1) | (lane == i2), 1.0, 0.0)
    sel_ref[...] = sel
    rr = lax.broadcasted_iota(jnp.int32, (tm, tm), 0)
    cc = lax.broadcasted_iota(jnp.int32, (tm, tm), 1)
    tri = jnp.where(cc < rr, 1.0, 0.0).astype(BF16)
    rank_ref[...] = _bdot(tri, sel.astype(BF16)) + carry_ref[...]
    carry_ref[...] += jnp.sum(sel, axis=0, keepdims=True)
    cnt_ref[...] = carry_ref[...]


def _router(x, g, router_w):
    m, d = x.shape
    tm = min(TM_ROUTE, m)
    rw = jnp.zeros((d, LANES), F32).at[:, :N_EXPERTS].set(router_w)
    kern = functools.partial(_router_kernel, tm=tm)
    wide = jax.ShapeDtypeStruct((m, LANES), F32)
    return pl.pallas_call(
        kern,
        grid=(m // tm,),
        in_specs=[pl.BlockSpec((tm, d), lambda i: (i, 0)),
                  pl.BlockSpec((1, d), lambda i: (0, 0)),
                  pl.BlockSpec((d, LANES), lambda i: (0, 0))],
        out_specs=[pl.BlockSpec((tm, d), lambda i: (i, 0)),
                   pl.BlockSpec((tm, LANES), lambda i: (i, 0)),
                   pl.BlockSpec((tm, LANES), lambda i: (i, 0)),
                   pl.BlockSpec((tm, LANES), lambda i: (i, 0)),
                   pl.BlockSpec((1, LANES), lambda i: (0, 0))],
        out_shape=[jax.ShapeDtypeStruct((m, d), F32), wide, wide, wide,
                   jax.ShapeDtypeStruct((1, LANES), F32)],
        scratch_shapes=[pltpu.VMEM((1, LANES), F32)],
        compiler_params=_cparams(("arbitrary",), 32),
        name="router_top2",
    )(x, g.reshape(1, d), rw)


def _start_row_gather(src_hbm, idx_ref, base, n, dst_ref, sem):
    def body(r, carry):
        t = idx_ref[base + r]
        pltpu.make_async_copy(src_hbm.at[pl.ds(t, 1), :], dst_ref.at[pl.ds(r, 1), :], sem).start()
        return carry

    lax.fori_loop(0, n, body, 0, unroll=8)


def _wait_row_gather(src_hbm, n, dst_ref, sem):
    def body(r, carry):
        pltpu.make_async_copy(src_hbm.at[pl.ds(0, 1), :], dst_ref.at[pl.ds(r, 1), :], sem).wait()
        return carry

    lax.fori_loop(0, n, body, 0, unroll=8)


def _group_rows_kernel(src_ref, h_hbm, o_ref, buf_ref, sem_ref, *, tg):
    i = pl.program_id(0)
    n = pl.num_programs(0)
    slot = i % 2

    @pl.when(i == 0)
    def _():
        _start_row_gather(h_hbm, src_ref, 0, tg, buf_ref.at[0], sem_ref.at[0])

    @pl.when(i + 1 < n)
    def _():
        _start_row_gather(h_hbm, src_ref, (i + 1) * tg, tg, buf_ref.at[1 - slot], sem_ref.at[1 - slot])

    _wait_row_gather(h_hbm, tg, buf_ref.at[slot], sem_ref.at[slot])
    o_ref[...] = buf_ref[slot].astype(o_ref.dtype)


def _group_rows(h, src):
    d = h.shape[1]
    r = src.shape[0]
    tg = min(TG_GATHER, r)
    kern = functools.partial(_group_rows_kernel, tg=tg)
    return pl.pallas_call(
        kern,
        grid_spec=pltpu.PrefetchScalarGridSpec(
            num_scalar_prefetch=1,
            grid=(r // tg,),
            in_specs=[pl.BlockSpec(memory_space=pl.ANY)],
            out_specs=pl.BlockSpec((tg, d), lambda i, s: (i, 0)),
            scratch_shapes=[pltpu.VMEM((2, tg, d), F32),
                            pltpu.SemaphoreType.DMA((2,))]),
        out_shape=jax.ShapeDtypeStruct((r, d), BF16),
        compiler_params=_cparams(("arbitrary",), 32),
        name="group_rows_by_expert",
    )(src, h)


def _combine_kernel(pos_ref, x_ref, g0_ref, g1_ref, fn_ref, y_hbm, o_ref, buf_ref, sem_ref,
                    *, tc, ntok):
    i = pl.program_id(0)
    n = pl.num_programs(0)
    slot = i % 2

    def start(step, s):
        _start_row_gather(y_hbm, pos_ref, step * tc, tc, buf_ref.at[s, 0], sem_ref.at[s, 0])
        _start_row_gather(y_hbm, pos_ref, ntok + step * tc, tc, buf_ref.at[s, 1], sem_ref.at[s, 1])

    @pl.when(i == 0)
    def _():
        start(0, 0)

    @pl.when(i + 1 < n)
    def _():
        start(i + 1, 1 - slot)

    _wait_row_gather(y_hbm, tc, buf_ref.at[slot, 0], sem_ref.at[slot, 0])
    _wait_row_gather(y_hbm, tc, buf_ref.at[slot, 1], sem_ref.at[slot, 1])
    x = x_ref[...] + g0_ref[...] * buf_ref[slot, 0] + g1_ref[...] * buf_ref[slot, 1]
    ms = jnp.mean(x * x, axis=-1, keepdims=True)
    o_ref[...] = x * lax.rsqrt(ms + NORM_EPS) * fn_ref[...]


def _combine_final_norm(x, y, pos, g0, g1, final_norm):
    m, d = x.shape
    tc = min(TC_COMBINE, m)
    kern = functools.partial(_combine_kernel, tc=tc, ntok=m)
    return pl.pallas_call(
        kern,
        grid_spec=pltpu.PrefetchScalarGridSpec(
            num_scalar_prefetch=1,
            grid=(m // tc,),
            in_specs=[pl.BlockSpec((tc, d), lambda i, s: (i, 0)),
                      pl.BlockSpec((tc, 1), lambda i, s: (i, 0)),
                      pl.BlockSpec((tc, 1), lambda i, s: (i, 0)),
                      pl.BlockSpec((1, d), lambda i, s: (0, 0)),
                      pl.BlockSpec(memory_space=pl.ANY)],
            out_specs=pl.BlockSpec((tc, d), lambda i, s: (i, 0)),
            scratch_shapes=[pltpu.VMEM((2, 2, tc, d), F32),
                            pltpu.SemaphoreType.DMA((2, 2))]),
        out_shape=jax.ShapeDtypeStruct((m, d), F32),
        compiler_params=_cparams(("arbitrary",), 32),
        name="moe_combine_final_norm",
    )(pos, x, g0, g1, final_norm.reshape(1, d), y)


def _tile_pipeline(off_ref, cnt_ref, n_col_tiles, n_tiles, in_copy, out_copy, zero_copy, compute):
    j = pl.program_id(0)
    e = pl.program_id(1)
    n_used = off_ref[N_EXPERTS]
    last_col = j == n_col_tiles - 1

    @pl.when((j == 0) & (e == 0))
    def _():
        cnt_ref[0] = 0
        for k in range(IN_RING):
            in_copy(k, k).start()

    def body(t, g):
        in_slot = g % IN_RING
        out_slot = g % OUT_RING
        in_copy(t, in_slot).wait()

        @pl.when(g >= OUT_RING)
        def _():
            out_copy(0, out_slot).wait()

        compute(in_slot, out_slot)
        out_copy(t, out_slot).start()
        ahead = t + IN_RING
        wraps = ahead >= n_used

        @pl.when(jnp.logical_not(wraps & last_col))
        def _():
            in_copy(jnp.where(wraps, ahead - n_used, ahead), in_slot).start()

        return g + 1

    g = lax.fori_loop(off_ref[e], off_ref[e + 1], body, cnt_ref[0])
    cnt_ref[0] = g

    @pl.when(e == N_EXPERTS - 1)
    def _():
        def tail(t, carry):
            zero_copy(t).start()
            zero_copy(t).wait()
            return carry

        lax.fori_loop(n_used, n_tiles, tail, 0)

        @pl.when(last_col)
        def _():
            for back in range(OUT_RING, 0, -1):
                @pl.when(g >= back)
                def _():
                    out_copy(0, (g - back) % OUT_RING).wait()


def _moe_up_kernel(off_ref, xs_hbm, wg_ref, wu_ref, act_hbm, wgb_ref, wub_ref, xbuf, obuf, zbuf,
                   cnt_ref, xsem, osem, zsem, *, tm, tf, n_tiles):
    j = pl.program_id(0)
    e = pl.program_id(1)

    @pl.when((j == 0) & (e == 0))
    def _():
        zbuf[...] = jnp.zeros_like(zbuf)

    @pl.when(off_ref[e + 1] > off_ref[e])
    def _():
        wgb_ref[...] = wg_ref[0].astype(BF16)
        wub_ref[...] = wu_ref[0].astype(BF16)

    def in_copy(t, slot):
        return pltpu.make_async_copy(xs_hbm.at[pl.ds(t * tm, tm), :], xbuf.at[slot], xsem.at[slot])

    def out_tile(t):
        return act_hbm.at[pl.ds(t * tm, tm), pl.ds(j * tf, tf)]

    def out_copy(t, slot):
        return pltpu.make_async_copy(obuf.at[slot], out_tile(t), osem.at[slot])

    def zero_copy(t):
        return pltpu.make_async_copy(zbuf, out_tile(t), zsem.at[0])

    def compute(in_slot, out_slot):
        x = xbuf[in_slot]
        gate = _bdot(x, wgb_ref[...])
        up = _bdot(x, wub_ref[...])
        obuf[out_slot] = (gate * jax.nn.sigmoid(gate) * up).astype(obuf.dtype)

    _tile_pipeline(off_ref, cnt_ref, pl.num_programs(0), n_tiles, in_copy, out_copy, zero_copy,
                   compute)


def _moe_up(xs, wg, wu, tile_off, tm):
    r, d = xs.shape
    f = wg.shape[2]
    tf = min(TF_MOE, f)
    kern = functools.partial(_moe_up_kernel, tm=tm, tf=tf, n_tiles=r // tm)
    return pl.pallas_call(
        kern,
        grid_spec=pltpu.PrefetchScalarGridSpec(
            num_scalar_prefetch=1,
            grid=(f // tf, N_EXPERTS),
            in_specs=[pl.BlockSpec(memory_space=pl.ANY),
                      pl.BlockSpec((1, d, tf), lambda j, e, off: (e, 0, j)),
                      pl.BlockSpec((1, d, tf), lambda j, e, off: (e, 0, j))],
            out_specs=pl.BlockSpec(memory_space=pl.ANY),
            scratch_shapes=[pltpu.VMEM((d, tf), BF16), pltpu.VMEM((d, tf), BF16),
                            pltpu.VMEM((IN_RING, tm, d), BF16),
                            pltpu.VMEM((OUT_RING, tm, tf), BF16),
                            pltpu.VMEM((tm, tf), BF16),
                            pltpu.SMEM((1,), jnp.int32),
                            pltpu.SemaphoreType.DMA((IN_RING,)),
                            pltpu.SemaphoreType.DMA((OUT_RING,)),
                            pltpu.SemaphoreType.DMA((1,))]),
        out_shape=jax.ShapeDtypeStruct((r, f), BF16),
        compiler_params=_cparams(("arbitrary", "arbitrary"), 56),
        name="moe_gate_up",
    )(tile_off, xs, wg, wu)


def _moe_down_kernel(off_ref, act_hbm, wd_ref, y_hbm, wdb_ref, abuf, obuf, zbuf, cnt_ref,
                     asem, osem, zsem, *, tm, tn, n_tiles):
    j = pl.program_id(0)
    e = pl.program_id(1)

    @pl.when((j == 0) & (e == 0))
    def _():
        zbuf[...] = jnp.zeros_like(zbuf)

    @pl.when(off_ref[e + 1] > off_ref[e])
    def _():
        wdb_ref[...] = wd_ref[0].astype(BF16)

    def in_copy(t, slot):
        return pltpu.make_async_copy(act_hbm.at[pl.ds(t * tm, tm), :], abuf.at[slot], asem.at[slot])

    def out_tile(t):
        return y_hbm.at[pl.ds(t * tm, tm), pl.ds(j * tn, tn)]

    def out_copy(t, slot):
        return pltpu.make_async_copy(obuf.at[slot], out_tile(t), osem.at[slot])

    def zero_copy(t):
        return pltpu.make_async_copy(zbuf, out_tile(t), zsem.at[0])

    def compute(in_slot, out_slot):
        obuf[out_slot] = _bdot(abuf[in_slot], wdb_ref[...])

    _tile_pipeline(off_ref, cnt_ref, pl.num_programs(0), n_tiles, in_copy, out_copy, zero_copy,
                   compute)


def _moe_down(act, wd, tile_off, tm):
    r, f = act.shape
    d = wd.shape[2]
    tn = min(TN_MOE, d)
    kern = functools.partial(_moe_down_kernel, tm=tm, tn=tn, n_tiles=r // tm)
    return pl.pallas_call(
        kern,
        grid_spec=pltpu.PrefetchScalarGridSpec(
            num_scalar_prefetch=1,
            grid=(d // tn, N_EXPERTS),
            in_specs=[pl.BlockSpec(memory_space=pl.ANY),
                      pl.BlockSpec((1, f, tn), lambda j, e, off: (e, 0, j))],
            out_specs=pl.BlockSpec(memory_space=pl.ANY),
            scratch_shapes=[pltpu.VMEM((f, tn), BF16),
                            pltpu.VMEM((IN_RING, tm, f), BF16),
                            pltpu.VMEM((OUT_RING, tm, tn), F32),
                            pltpu.VMEM((tm, tn), F32),
                            pltpu.SMEM((1,), jnp.int32),
                            pltpu.SemaphoreType.DMA((IN_RING,)),
                            pltpu.SemaphoreType.DMA((OUT_RING,)),
                            pltpu.SemaphoreType.DMA((1,))]),
        out_shape=jax.ShapeDtypeStruct((r, d), F32),
        compiler_params=_cparams(("arbitrary", "arbitrary"), 56),
        name="moe_down",
    )(tile_off, act, wd)


def _routing_tables(gates, sel, rank, counts, tm):
    ntok = gates.shape[0]
    n_tiles = (2 * ntok) // tm + N_EXPERTS
    cnt = counts[0, :N_EXPERTS].astype(jnp.int32)
    padded = ((cnt + tm - 1) // tm) * tm
    ends = jnp.cumsum(padded)
    offs = ends - padded
    selb = sel[:, :N_EXPERTS] > 0.5
    pos = offs[None, :] + rank[:, :N_EXPERTS].astype(jnp.int32)
    e_lo = jnp.argmax(selb, axis=1)
    e_hi = N_EXPERTS - 1 - jnp.argmax(selb[:, ::-1], axis=1)
    take = lambda a, e: jnp.take_along_axis(a, e[:, None], axis=1)[:, 0]
    pos0, pos1 = take(pos, e_lo), take(pos, e_hi)
    g8 = gates[:, :N_EXPERTS]
    g0, g1 = take(g8, e_lo), take(g8, e_hi)
    tok = jnp.arange(ntok, dtype=jnp.int32)
    pos_all = jnp.concatenate([pos0, pos1]).astype(jnp.int32)
    src = jnp.zeros((n_tiles * tm,), jnp.int32).at[pos_all].set(
        jnp.concatenate([tok, tok]), unique_indices=True)
    tile_off = jnp.concatenate([jnp.zeros((1,), jnp.int32), (ends // tm).astype(jnp.int32)])
    return src, pos_all, g0[:, None], g1[:, None], tile_off


def _moe_final(x, g, router_w, wg, wu, wd, final_norm):
    h, gates, sel, rank, counts = _router(x, g, router_w)
    tm = min(TM_MOE, x.shape[0])
    src, pos_all, g0, g1, tile_off = _routing_tables(gates, sel, rank, counts, tm)
    xs = _group_rows(h, src)
    act = _moe_up(xs, wg, wu, tile_off, tm)
    y = _moe_down(act, wd, tile_off, tm)
    return _combine_final_norm(x, y, pos_all, g0, g1, final_norm)


def _forward(x, norm_mix_even, w_in_even, conv3_w, w_out_even, norm_ffn_even,
             ffn_w_gate, ffn_w_up, ffn_w_down, norm_mix_odd, cf_pw1_w, cf_pw1_b,
             cf_dw_w, cf_dw_b, cf_ln_g, cf_ln_b, cf_pw2_w, cf_pw2_b, norm_moe,
             router_w, moe_w_gate, moe_w_up, moe_w_down, final_norm):
    batch, seq, d = x.shape
    xt = x.reshape(batch * seq, d)
    ncol = SC_WIDTH // LANES
    q_lo, q_hi = 3 * SC_WIDTH, 3 * SC_WIDTH + SB_HEADS * SB_HEAD_DIM
    col = jnp.arange(w_in_even.shape[2])
    col_scale = jnp.where((col >= q_lo) & (col < q_hi), -(SB_HEAD_DIM ** -0.5) * LOG2E, 1.0)
    p = _norm_matmul(xt, norm_mix_even[0], w_in_even[0], col_scale.astype(F32), BF16)
    ysb = _stick_breaking(p, batch, seq, 3 * ncol, 3 * ncol + SB_HEADS, 3 * ncol + 2 * SB_HEADS)
    xt = _mix_out_proj(p, ysb, conv3_w[0], w_out_even[0], xt, seq)
    xt = _dense_swiglu(xt, norm_ffn_even[0], ffn_w_gate[0], ffn_w_up[0], ffn_w_down[0])
    u = _norm_glu(xt, norm_mix_odd[0], cf_pw1_w[0], cf_pw1_b[0])
    v = _dwconv_ln_silu(u, cf_dw_w[0], cf_dw_b[0], cf_ln_g[0], cf_ln_b[0], seq)
    xt = _matmul_bias_res(v, cf_pw2_w[0], cf_pw2_b[0], xt)
    out = _moe_final(xt, norm_moe[0], router_w[0], moe_w_gate[0], moe_w_up[0], moe_w_down[0],
                     final_norm)
    return out.reshape(batch, seq, d)


def kernel(x, norm_mix_even, w_in_even, conv3_w, w_out_even, norm_ffn_even, ffn_w_gate, ffn_w_up, ffn_w_down, norm_mix_odd, cf_pw1_w, cf_pw1_b, cf_dw_w, cf_dw_b, cf_ln_g, cf_ln_b, cf_pw2_w, cf_pw2_b, norm_moe, router_w, moe_w_gate, moe_w_up, moe_w_down, final_norm):
    return _forward(x, norm_mix_even, w_in_even, conv3_w, w_out_even, norm_ffn_even,
                    ffn_w_gate, ffn_w_up, ffn_w_down, norm_mix_odd, cf_pw1_w, cf_pw1_b,
                    cf_dw_w, cf_dw_b, cf_ln_g, cf_ln_b, cf_pw2_w, cf_pw2_b, norm_moe,
                    router_w, moe_w_gate, moe_w_up, moe_w_down, final_norm)
```

```python
import functools

import jax
import jax.numpy as jnp
from jax import lax
from jax.experimental import pallas as pl
from jax.experimental.pallas import tpu as pltpu

F32 = jnp.float32
BF16 = jnp.bfloat16

NORM_EPS = 1e-6
SC_WIDTH = 1024
SC_KERNEL = 3
SB_HEADS = 8
SB_HEAD_DIM = 128
CF_KERNEL = 31
N_EXPERTS = 8
LANES = 128
SUBLANES = 8
MIB = 1024 * 1024

TM_PROJ = 1024
TN_PROJ = 1024
TN_GLU = 512
TM_FFN = 1024
TF_FFN = 256
TQ_ATTN = 512
ATTN_UNROLL = 4
LOG2E = 1.4426950408889634
TM_CONV = 256
CONV_HALO = 32
SC_HALO = 16
TM_ROUTE = 512
TM_MOE = 256
TF_MOE = 1024
TN_MOE = 512
IN_RING = 3
OUT_RING = 2
TG_GATHER = 256
TC_COMBINE = 128


def _cparams(semantics, vmem_mib):
    return pltpu.CompilerParams(dimension_semantics=semantics,
                                vmem_limit_bytes=vmem_mib * MIB)


def _rmsnorm_to(x_ref, g_ref, h_ref, chunk=128):
    g = g_ref[...]

    def body(c, carry):
        r = pl.multiple_of(c * chunk, chunk)
        x = x_ref[pl.ds(r, chunk), :]
        ms = jnp.mean(x * x, axis=-1, keepdims=True)
        h_ref[pl.ds(r, chunk), :] = (x * lax.rsqrt(ms + NORM_EPS) * g).astype(h_ref.dtype)
        return carry

    lax.fori_loop(0, x_ref.shape[0] // chunk, body, 0)


def _bdot(a, b):
    return jnp.dot(a, b, preferred_element_type=F32)


def _norm_mm_kernel(x_ref, g_ref, w_ref, cs_ref, o_ref, h_ref):
    @pl.when(pl.program_id(1) == 0)
    def _():
        _rmsnorm_to(x_ref, g_ref, h_ref)

    o_ref[...] = (_bdot(h_ref[...], w_ref[...].astype(BF16)) * cs_ref[...]).astype(o_ref.dtype)


def _norm_matmul(x, g, w, col_scale, out_dtype):
    m, k = x.shape
    n = w.shape[1]
    tm, tn = min(TM_PROJ, m), min(TN_PROJ, n)
    return pl.pallas_call(
        _norm_mm_kernel,
        grid=(m // tm, n // tn),
        in_specs=[pl.BlockSpec((tm, k), lambda i, j: (i, 0)),
                  pl.BlockSpec((1, k), lambda i, j: (0, 0)),
                  pl.BlockSpec((k, tn), lambda i, j: (0, j)),
                  pl.BlockSpec((1, tn), lambda i, j: (0, j))],
        out_specs=pl.BlockSpec((tm, tn), lambda i, j: (i, j)),
        out_shape=jax.ShapeDtypeStruct((m, n), out_dtype),
        scratch_shapes=[pltpu.VMEM((tm, k), BF16)],
        compiler_params=_cparams(("parallel", "arbitrary"), 56),
        name="norm_in_proj",
    )(x, g.reshape(1, k), w, col_scale.reshape(1, n))


def _norm_glu_kernel(x_ref, g_ref, wa_ref, wg_ref, ba_ref, bg_ref, o_ref, h_ref):
    @pl.when(pl.program_id(1) == 0)
    def _():
        _rmsnorm_to(x_ref, g_ref, h_ref)

    h = h_ref[...]
    a = _bdot(h, wa_ref[...].astype(BF16)) + ba_ref[...]
    gate = _bdot(h, wg_ref[...].astype(BF16)) + bg_ref[...]
    o_ref[...] = (a * jax.nn.sigmoid(gate)).astype(o_ref.dtype)


def _norm_glu(x, g, w, b):
    m, k = x.shape
    n = w.shape[1] // 2
    tm, tn = min(TM_PROJ, m), min(TN_GLU, n)
    nb = n // tn
    b2 = b.reshape(1, 2 * n)
    return pl.pallas_call(
        _norm_glu_kernel,
        grid=(m // tm, nb),
        in_specs=[pl.BlockSpec((tm, k), lambda i, j: (i, 0)),
                  pl.BlockSpec((1, k), lambda i, j: (0, 0)),
                  pl.BlockSpec((k, tn), lambda i, j: (0, j)),
                  pl.BlockSpec((k, tn), lambda i, j: (0, j + nb)),
                  pl.BlockSpec((1, tn), lambda i, j: (0, j)),
                  pl.BlockSpec((1, tn), lambda i, j: (0, j + nb))],
        out_specs=pl.BlockSpec((tm, tn), lambda i, j: (i, j)),
        out_shape=jax.ShapeDtypeStruct((m, n), F32),
        scratch_shapes=[pltpu.VMEM((tm, k), BF16)],
        compiler_params=_cparams(("parallel", "arbitrary"), 56),
        name="norm_pw1_glu",
    )(x, g.reshape(1, k), w, w, b2, b2)


def _mm_bias_res_kernel(a_ref, w_ref, b_ref, r_ref, o_ref):
    o_ref[...] = r_ref[...] + _bdot(a_ref[...], w_ref[...].astype(BF16)) + b_ref[...]


def _matmul_bias_res(a, w, b, res):
    m, k = a.shape
    n = w.shape[1]
    tm, tn = min(TM_PROJ, m), min(TN_PROJ, n)
    return pl.pallas_call(
        _mm_bias_res_kernel,
        grid=(m // tm, n // tn),
        in_specs=[pl.BlockSpec((tm, k), lambda i, j: (i, 0)),
                  pl.BlockSpec((k, tn), lambda i, j: (0, j)),
                  pl.BlockSpec((1, tn), lambda i, j: (0, j)),
                  pl.BlockSpec((tm, tn), lambda i, j: (i, j))],
        out_specs=pl.BlockSpec((tm, tn), lambda i, j: (i, j)),
        out_shape=jax.ShapeDtypeStruct((m, n), F32),
        compiler_params=_cparams(("parallel", "arbitrary"), 56),
        name="pw2_residual",
    )(a, w, b.reshape(1, n), res)


def _attn_kernel(q_ref, k_ref, v_ref, o_ref, u_ref, c_ref, acc_ref, nz_ref, cm_ref, *, tq):
    hk = LANES
    tk = 2 * hk
    q0 = pl.program_id(2) * tq
    q = q_ref[...]
    c_ref[...] = jnp.zeros_like(c_ref)
    acc_ref[...] = jnp.zeros_like(acc_ref)

    jj = lax.broadcasted_iota(jnp.int32, (2 * hk, 2 * hk), 0) & (hk - 1)
    ss = lax.broadcasted_iota(jnp.int32, (2 * hk, 2 * hk), 1)
    u_ref[...] = jnp.where((ss >= hk) | (jj >= ss), 1.0, 0.0).astype(BF16)

    def half_sums(lk):
        hi = lk.astype(BF16)
        lo = (lk - hi.astype(F32)).astype(BF16)
        return _bdot(jnp.concatenate([hi, lo], axis=1), u_ref[...])

    def scores(kstart, slot, mask):
        kb = k_ref[pl.ds(kstart, tk), :]
        nz = lax.dot_general(q, kb, (((1,), (1,)), ((), ())), preferred_element_type=F32)
        neg_abs = lax.bitcast_convert_type(
            lax.bitcast_convert_type(nz, jnp.uint32) | jnp.uint32(0x80000000), F32)
        log_keep = jnp.minimum(nz, 0.0) - jnp.log2(1.0 + jnp.exp2(neg_abs))
        if mask is not None:
            log_keep = jnp.where(mask, log_keep, 0.0)
        nz_ref[slot] = nz
        cm_ref[slot, 1] = half_sums(log_keep[:, hk:])
        cm_ref[slot, 0] = half_sums(log_keep[:, :hk])

    def weights(kstart, slot, mask):
        c = c_ref[...]
        nz = nz_ref[slot]
        cm1 = cm_ref[slot, 1]
        cm0 = cm_ref[slot, 0]
        logit1 = (c + cm1[:, :hk]) - nz[:, hk:]
        c = c + cm1[:, hk:]
        logit0 = (c + cm0[:, :hk]) - nz[:, :hk]
        c_ref[...] = c + cm0[:, hk:]
        a = jnp.exp2(jnp.concatenate([logit0, logit1], axis=1))
        if mask is not None:
            a = jnp.where(mask, a, 0.0)
        acc_ref[...] += _bdot(a.astype(BF16), v_ref[pl.ds(kstart, tk), :])

    rows = lax.broadcasted_iota(jnp.int32, (tq, tk), 0)
    cols = lax.broadcasted_iota(jnp.int32, (tq, tk), 1)
    nd = tq // tk
    assert nd == 2 and ATTN_UNROLL % 2 == 0
    diag = [(pl.multiple_of(q0 + d * tk, tk), (cols + d * tk) < rows)
            for d in reversed(range(nd))]
    nb = q0 // tk

    def below(j):
        return pl.multiple_of(q0 - (j + 1) * tk, tk)

    scores(diag[0][0], 0, diag[0][1])
    for i in range(1, nd):
        scores(diag[i][0], i % 2, diag[i][1])
        weights(diag[i - 1][0], (i - 1) % 2, diag[i - 1][1])
    last_k, last_mask = diag[nd - 1]

    @pl.when(nb == 0)
    def _():
        weights(last_k, (nd - 1) % 2, last_mask)

    @pl.when(nb > 0)
    def _():
        scores(below(0), nd % 2, None)
        weights(last_k, (nd - 1) % 2, last_mask)

        def steps(j0, count):
            for k in range(count):
                weights(below(j0 + k), (nd + k) % 2, None)
                scores(below(j0 + k + 1), (nd + k + 1) % 2, None)

        def body(i, carry):
            steps(ATTN_UNROLL * i, ATTN_UNROLL)
            return carry

        trips = (nb - 1) // ATTN_UNROLL
        lax.fori_loop(0, trips, body, 0)
        rest = nb - 1 - ATTN_UNROLL * trips
        for r in range(1, ATTN_UNROLL, 2):
            @pl.when(rest == r)
            def _():
                steps(nb - 1 - r, r)
                weights(below(nb - 1), (nd + r) % 2, None)

    o_ref[...] = acc_ref[...].astype(o_ref.dtype)


def _stick_breaking(p, batch, seq, col_q, col_k, col_v):
    tq = min(TQ_ATTN, seq)
    nq = seq // tq
    dh = SB_HEAD_DIM
    kern = functools.partial(_attn_kernel, tq=tq)
    return pl.pallas_call(
        kern,
        grid=(batch, SB_HEADS, nq),
        in_specs=[pl.BlockSpec((tq, dh), lambda b, h, i: (b * nq + i, col_q + h)),
                  pl.BlockSpec((seq, dh), lambda b, h, i: (b, col_k + h)),
                  pl.BlockSpec((seq, dh), lambda b, h, i: (b, col_v + h))],
        out_specs=pl.BlockSpec((tq, dh), lambda b, h, i: (b * nq + i, h)),
        out_shape=jax.ShapeDtypeStruct((batch * seq, SB_HEADS * dh), BF16),
        scratch_shapes=[pltpu.VMEM((2 * LANES, 2 * LANES), BF16),
                        pltpu.VMEM((tq, LANES), F32),
                        pltpu.VMEM((tq, dh), F32),
                        pltpu.VMEM((2, tq, 2 * LANES), F32),
                        pltpu.VMEM((2, 2, tq, 2 * LANES), F32)],
        compiler_params=_cparams(("parallel", "parallel", "arbitrary"), 32),
        name="stick_breaking_attention",
    )(p, p, p)


def _mix_out_kernel(pb_ref, pc_ref, ph_ref, hc_ref, hh_ref, cw_ref, ysb_ref, w_ref, x_ref,
                    o_ref, mix_ref, u_ref, *, tm, seq, chunk):
    i = pl.program_id(0)

    @pl.when(pl.program_id(1) == 0)
    def _():
        halo = hc_ref[...].astype(F32) * hh_ref[...].astype(F32)
        u_ref[0:SC_HALO, :] = jnp.where((i * tm) % seq == 0, 0.0, halo)
        for r in range(0, tm, chunk):
            u_ref[SC_HALO + r:SC_HALO + r + chunk, :] = (
                pc_ref[r:r + chunk, :].astype(F32) * ph_ref[r:r + chunk, :].astype(F32))
        w0, w1, w2 = cw_ref[0:1, :], cw_ref[1:2, :], cw_ref[2:3, :]
        for r in range(0, tm, chunk):
            base = SC_HALO + r
            conv = (w2 * u_ref[base:base + chunk, :]
                    + w1 * u_ref[base - 1:base - 1 + chunk, :]
                    + w0 * u_ref[base - 2:base - 2 + chunk, :])
            ysc = pb_ref[r:r + chunk, :].astype(F32) * conv
            mix_ref[r:r + chunk, 0:SC_WIDTH] = ysc.astype(BF16)
        mix_ref[:, SC_WIDTH:] = ysb_ref[...]

    o_ref[...] = x_ref[...] + _bdot(mix_ref[...], w_ref[...].astype(BF16))


def _mix_out_proj(p, ysb, conv_w, w_out, x, seq):
    m = x.shape[0]
    d = w_out.shape[1]
    kdim = w_out.shape[0]
    tm, tn = min(TM_PROJ, seq), min(TN_GLU, d)
    hb = tm // SC_HALO
    kern = functools.partial(_mix_out_kernel, tm=tm, seq=seq, chunk=min(128, tm))
    return pl.pallas_call(
        kern,
        grid=(m // tm, d // tn),
        in_specs=[pl.BlockSpec((tm, SC_WIDTH), lambda i, j: (i, 0)),
                  pl.BlockSpec((tm, SC_WIDTH), lambda i, j: (i, 1)),
                  pl.BlockSpec((tm, SC_WIDTH), lambda i, j: (i, 2)),
                  pl.BlockSpec((SC_HALO, SC_WIDTH), lambda i, j: (jnp.maximum(i * hb - 1, 0), 1)),
                  pl.BlockSpec((SC_HALO, SC_WIDTH), lambda i, j: (jnp.maximum(i * hb - 1, 0), 2)),
                  pl.BlockSpec((SC_KERNEL, SC_WIDTH), lambda i, j: (0, 0)),
                  pl.BlockSpec((tm, SC_WIDTH), lambda i, j: (i, 0)),
                  pl.BlockSpec((kdim, tn), lambda i, j: (0, j)),
                  pl.BlockSpec((tm, tn), lambda i, j: (i, j))],
        out_specs=pl.BlockSpec((tm, tn), lambda i, j: (i, j)),
        out_shape=jax.ShapeDtypeStruct((m, d), F32),
        scratch_shapes=[pltpu.VMEM((tm, kdim), BF16),
                        pltpu.VMEM((tm + SC_HALO, SC_WIDTH), F32)],
        compiler_params=_cparams(("parallel", "arbitrary"), 56),
        name="shortconv_out_proj",
    )(p, p, p, p, p, conv_w, ysb, w_out, x)


def _ffn_kernel(x_ref, g_ref, wg_ref, wu_ref, wd_ref, o_ref, h_ref):
    @pl.when(pl.program_id(1) == 0)
    def _():
        _rmsnorm_to(x_ref, g_ref, h_ref)
        o_ref[...] = x_ref[...]

    h = h_ref[...]
    gate = _bdot(h, wg_ref[...].astype(BF16))
    up = _bdot(h, wu_ref[...].astype(BF16))
    act = (gate * jax.nn.sigmoid(gate) * up).astype(BF16)
    o_ref[...] += _bdot(act, wd_ref[...].astype(BF16))


def _dense_swiglu(x, g, wg, wu, wd):
    m, d = x.shape
    f = wg.shape[1]
    tm, tf = min(TM_FFN, m), min(TF_FFN, f)
    return pl.pallas_call(
        _ffn_kernel,
        grid=(m // tm, f // tf),
        in_specs=[pl.BlockSpec((tm, d), lambda i, j: (i, 0), pipeline_mode=pl.Buffered(1)),
                  pl.BlockSpec((1, d), lambda i, j: (0, 0)),
                  pl.BlockSpec((d, tf), lambda i, j: (0, j)),
                  pl.BlockSpec((d, tf), lambda i, j: (0, j)),
                  pl.BlockSpec((tf, d), lambda i, j: (j, 0))],
        out_specs=pl.BlockSpec((tm, d), lambda i, j: (i, 0)),
        out_shape=jax.ShapeDtypeStruct((m, d), F32),
        scratch_shapes=[pltpu.VMEM((tm, d), BF16)],
        compiler_params=_cparams(("parallel", "arbitrary"), 60),
        name="dense_swiglu",
    )(x, g.reshape(1, d), wg, wu, wd)


def _dwconv_ln_kernel(u_ref, halo_ref, w_ref, b_ref, lg_ref, lb_ref, o_ref, buf_ref, conv_ref,
                      *, tm, seq, rchunk):
    i = pl.program_id(0)
    buf_ref[0:CONV_HALO, :] = jnp.where((i * tm) % seq == 0, 0.0, halo_ref[...])
    buf_ref[CONV_HALO:, :] = u_ref[...]
    nlt = u_ref.shape[1] // LANES
    first = CONV_HALO - (CF_KERNEL - 1)

    def lane_tile(c, carry):
        lanes = pl.ds(pl.multiple_of(c * LANES, LANES), LANES)
        for r in range(0, tm, rchunk):
            acc = None
            for m in range(SUBLANES):
                rows = rchunk + (SUBLANES if m else 0)
                g = None
                for k in range(CF_KERNEL):
                    if (first + k) % SUBLANES != m:
                        continue
                    base = r + first + k - m
                    term = w_ref[k:k + 1, lanes] * buf_ref[base:base + rows, lanes]
                    g = term if g is None else g + term
                if m:
                    g = pltpu.roll(g, rows - m, axis=0)[:rchunk]
                acc = g if acc is None else acc + g
            conv_ref[r:r + rchunk, lanes] = acc
        return carry

    lax.fori_loop(0, nlt, lane_tile, 0)

    bias, lg, lb = b_ref[...], lg_ref[...], lb_ref[...]
    for r in range(0, tm, rchunk):
        y = conv_ref[r:r + rchunk, :] + bias
        mu = jnp.mean(y, axis=-1, keepdims=True)
        yc = y - mu
        var = jnp.mean(yc * yc, axis=-1, keepdims=True)
        t = yc * lax.rsqrt(var + NORM_EPS) * lg + lb
        o_ref[r:r + rchunk, :] = (t * jax.nn.sigmoid(t)).astype(o_ref.dtype)


def _dwconv_ln_silu(u, w, b, lg, lb, seq):
    m, c = u.shape
    tm = min(TM_CONV, seq)
    hb = tm // CONV_HALO
    kern = functools.partial(_dwconv_ln_kernel, tm=tm, seq=seq, rchunk=min(128, tm))
    return pl.pallas_call(
        kern,
        grid=(m // tm,),
        in_specs=[pl.BlockSpec((tm, c), lambda i: (i, 0)),
                  pl.BlockSpec((CONV_HALO, c), lambda i: (jnp.maximum(i * hb - 1, 0), 0)),
                  pl.BlockSpec((CF_KERNEL, c), lambda i: (0, 0)),
                  pl.BlockSpec((1, c), lambda i: (0, 0)),
                  pl.BlockSpec((1, c), lambda i: (0, 0)),
                  pl.BlockSpec((1, c), lambda i: (0, 0))],
        out_specs=pl.BlockSpec((tm, c), lambda i: (i, 0)),
        out_shape=jax.ShapeDtypeStruct((m, c), BF16),
        scratch_shapes=[pltpu.VMEM((tm + CONV_HALO, c), F32),
                        pltpu.VMEM((tm, c), F32)],
        compiler_params=_cparams(("parallel",), 32),
        name="dwconv_ln_silu",
    )(u, u, w, b.reshape(1, c), lg.reshape(1, c), lb.reshape(1, c))


def _router_kernel(x_ref, g_ref, rw_ref, h_ref, gates_ref, sel_ref, rank_ref, cnt_ref, carry_ref,
                   *, tm):
    @pl.when(pl.program_id(0) == 0)
    def _():
        carry_ref[...] = jnp.zeros_like(carry_ref)

    _rmsnorm_to(x_ref, g_ref, h_ref)
    logits = jnp.dot(h_ref[...], rw_ref[...], preferred_element_type=F32,
                     precision=lax.Precision.HIGHEST)
    lane = lax.broadcasted_iota(jnp.int32, logits.shape, 1).astype(F32)
    neg = jnp.float32(-jnp.inf)
    logits = jnp.where(lane < N_EXPERTS, logits, neg)
    m1 = jnp.max(logits, axis=-1, keepdims=True)
    i1 = jnp.min(jnp.where(logits == m1, lane, float(LANES)), axis=-1, keepdims=True)
    rest = jnp.where(lane == i1, neg, logits)
    m2 = jnp.max(rest, axis=-1, keepdims=True)
    i2 = jnp.min(jnp.where(rest == m2, lane, float(LANES)), axis=-1, keepdims=True)
    e2 = jnp.exp(m2 - m1)
    w1 = 1.0 / (1.0 + e2)
    w2 = e2 / (1.0 + e2)
    gates_ref[...] = jnp.where(lane == i1, w1, 0.0) + jnp.where(lane == i2, w2, 0.0)
    sel = jnp.where((lane == i1) | (lane == i2), 1.0, 0.0)
    sel_ref[...] = sel
    rr = lax.broadcasted_iota(jnp.int32, (tm, tm), 0)
    cc = lax.broadcasted_iota(jnp.int32, (tm, tm), 1)
    tri = jnp.where(cc < rr, 1.0, 0.0).astype(BF16)
    rank_ref[...] = _bdot(tri, sel.astype(BF16)) + carry_ref[...]
    carry_ref[...] += jnp.sum(sel, axis=0, keepdims=True)
    cnt_ref[...] = carry_ref[...]


def _router(x, g, router_w):
    m, d = x.shape
    tm = min(TM_ROUTE, m)
    rw = jnp.zeros((d, LANES), F32).at[:, :N_EXPERTS].set(router_w)
    kern = functools.partial(_router_kernel, tm=tm)
    wide = jax.ShapeDtypeStruct((m, LANES), F32)
    return pl.pallas_call(
        kern,
        grid=(m // tm,),
        in_specs=[pl.BlockSpec((tm, d), lambda i: (i, 0)),
                  pl.BlockSpec((1, d), lambda i: (0, 0)),
                  pl.BlockSpec((d, LANES), lambda i: (0, 0))],
        out_specs=[pl.BlockSpec((tm, d), lambda i: (i, 0)),
                   pl.BlockSpec((tm, LANES), lambda i: (i, 0)),
                   pl.BlockSpec((tm, LANES), lambda i: (i, 0)),
                   pl.BlockSpec((tm, LANES), lambda i: (i, 0)),
                   pl.BlockSpec((1, LANES), lambda i: (0, 0))],
        out_shape=[jax.ShapeDtypeStruct((m, d), F32), wide, wide, wide,
                   jax.ShapeDtypeStruct((1, LANES), F32)],
        scratch_shapes=[pltpu.VMEM((1, LANES), F32)],
        compiler_params=_cparams(("arbitrary",), 32),
        name="router_top2",
    )(x, g.reshape(1, d), rw)


def _start_row_gather(src_hbm, idx_ref, base, n, dst_ref, sem):
    def body(r, carry):
        t = idx_ref[base + r]
        pltpu.make_async_copy(src_hbm.at[pl.ds(t, 1), :], dst_ref.at[pl.ds(r, 1), :], sem).start()
        return carry

    lax.fori_loop(0, n, body, 0, unroll=8)


def _wait_row_gather(src_hbm, n, dst_ref, sem):
    def body(r, carry):
        pltpu.make_async_copy(src_hbm.at[pl.ds(0, 1), :], dst_ref.at[pl.ds(r, 1), :], sem).wait()
        return carry

    lax.fori_loop(0, n, body, 0, unroll=8)


def _group_rows_kernel(src_ref, h_hbm, o_ref, buf_ref, sem_ref, *, tg):
    i = pl.program_id(0)
    n = pl.num_programs(0)
    slot = i % 2

    @pl.when(i == 0)
    def _():
        _start_row_gather(h_hbm, src_ref, 0, tg, buf_ref.at[0], sem_ref.at[0])

    @pl.when(i + 1 < n)
    def _():
        _start_row_gather(h_hbm, src_ref, (i + 1) * tg, tg, buf_ref.at[1 - slot], sem_ref.at[1 - slot])

    _wait_row_gather(h_hbm, tg, buf_ref.at[slot], sem_ref.at[slot])
    o_ref[...] = buf_ref[slot].astype(o_ref.dtype)


def _group_rows(h, src):
    d = h.shape[1]
    r = src.shape[0]
    tg = min(TG_GATHER, r)
    kern = functools.partial(_group_rows_kernel, tg=tg)
    return pl.pallas_call(
        kern,
        grid_spec=pltpu.PrefetchScalarGridSpec(
            num_scalar_prefetch=1,
            grid=(r // tg,),
            in_specs=[pl.BlockSpec(memory_space=pl.ANY)],
            out_specs=pl.BlockSpec((tg, d), lambda i, s: (i, 0)),
            scratch_shapes=[pltpu.VMEM((2, tg, d), F32),
                            pltpu.SemaphoreType.DMA((2,))]),
        out_shape=jax.ShapeDtypeStruct((r, d), BF16),
        compiler_params=_cparams(("arbitrary",), 32),
        name="group_rows_by_expert",
    )(src, h)


def _combine_kernel(pos_ref, x_ref, g0_ref, g1_ref, fn_ref, y_hbm, o_ref, buf_ref, sem_ref,
                    *, tc, ntok):
    i = pl.program_id(0)
    n = pl.num_programs(0)
    slot = i % 2

    def start(step, s):
        _start_row_gather(y_hbm, pos_ref, step * tc, tc, buf_ref.at[s, 0], sem_ref.at[s, 0])
        _start_row_gather(y_hbm, pos_ref, ntok + step * tc, tc, buf_ref.at[s, 1], sem_ref.at[s, 1])

    @pl.when(i == 0)
    def _():
        start(0, 0)

    @pl.when(i + 1 < n)
    def _():
        start(i + 1, 1 - slot)

    _wait_row_gather(y_hbm, tc, buf_ref.at[slot, 0], sem_ref.at[slot, 0])
    _wait_row_gather(y_hbm, tc, buf_ref.at[slot, 1], sem_ref.at[slot, 1])
    x = x_ref[...] + g0_ref[...] * buf_ref[slot, 0] + g1_ref[...] * buf_ref[slot, 1]
    ms = jnp.mean(x * x, axis=-1, keepdims=True)
    o_ref[...] = x * lax.rsqrt(ms + NORM_EPS) * fn_ref[...]


def _combine_final_norm(x, y, pos, g0, g1, final_norm):
    m, d = x.shape
    tc = min(TC_COMBINE, m)
    kern = functools.partial(_combine_kernel, tc=tc, ntok=m)
    return pl.pallas_call(
        kern,
        grid_spec=pltpu.PrefetchScalarGridSpec(
            num_scalar_prefetch=1,
            grid=(m // tc,),
            in_specs=[pl.BlockSpec((tc, d), lambda i, s: (i, 0)),
                      pl.BlockSpec((tc, 1), lambda i, s: (i, 0)),
                      pl.BlockSpec((tc, 1), lambda i, s: (i, 0)),
                      pl.BlockSpec((1, d), lambda i, s: (0, 0)),
                      pl.BlockSpec(memory_space=pl.ANY)],
            out_specs=pl.BlockSpec((tc, d), lambda i, s: (i, 0)),
            scratch_shapes=[pltpu.VMEM((2, 2, tc, d), F32),
                            pltpu.SemaphoreType.DMA((2, 2))]),
        out_shape=jax.ShapeDtypeStruct((m, d), F32),
        compiler_params=_cparams(("arbitrary",), 32),
        name="moe_combine_final_norm",
    )(pos, x, g0, g1, final_norm.reshape(1, d), y)


def _tile_pipeline(off_ref, cnt_ref, n_col_tiles, n_tiles, in_copy, out_copy, zero_copy, compute):
    j = pl.program_id(0)
    e = pl.program_id(1)
    n_used = off_ref[N_EXPERTS]
    last_col = j == n_col_tiles - 1

    @pl.when((j == 0) & (e == 0))
    def _():
        cnt_ref[0] = 0
        for k in range(IN_RING):
            in_copy(k, k).start()

    def body(t, g):
        in_slot = g % IN_RING
        out_slot = g % OUT_RING
        in_copy(t, in_slot).wait()

        @pl.when(g >= OUT_RING)
        def _():
            out_copy(0, out_slot).wait()

        compute(in_slot, out_slot)
        out_copy(t, out_slot).start()
        ahead = t + IN_RING
        wraps = ahead >= n_used

        @pl.when(jnp.logical_not(wraps & last_col))
        def _():
            in_copy(jnp.where(wraps, ahead - n_used, ahead), in_slot).start()

        return g + 1

    g = lax.fori_loop(off_ref[e], off_ref[e + 1], body, cnt_ref[0])
    cnt_ref[0] = g

    @pl.when(e == N_EXPERTS - 1)
    def _():
        def tail(t, carry):
            zero_copy(t).start()
            zero_copy(t).wait()
            return carry

        lax.fori_loop(n_used, n_tiles, tail, 0)

        @pl.when(last_col)
        def _():
            for back in range(OUT_RING, 0, -1):
                @pl.when(g >= back)
                def _():
                    out_copy(0, (g - back) % OUT_RING).wait()


def _moe_up_kernel(off_ref, xs_hbm, wg_ref, wu_ref, act_hbm, wgb_ref, wub_ref, xbuf, obuf, zbuf,
                   cnt_ref, xsem, osem, zsem, *, tm, tf, n_tiles):
    j = pl.program_id(0)
    e = pl.program_id(1)

    @pl.when((j == 0) & (e == 0))
    def _():
        zbuf[...] = jnp.zeros_like(zbuf)

    @pl.when(off_ref[e + 1] > off_ref[e])
    def _():
        wgb_ref[...] = wg_ref[0].astype(BF16)
        wub_ref[...] = wu_ref[0].astype(BF16)

    def in_copy(t, slot):
        return pltpu.make_async_copy(xs_hbm.at[pl.ds(t * tm, tm), :], xbuf.at[slot], xsem.at[slot])

    def out_tile(t):
        return act_hbm.at[pl.ds(t * tm, tm), pl.ds(j * tf, tf)]

    def out_copy(t, slot):
        return pltpu.make_async_copy(obuf.at[slot], out_tile(t), osem.at[slot])

    def zero_copy(t):
        return pltpu.make_async_copy(zbuf, out_tile(t), zsem.at[0])

    def compute(in_slot, out_slot):
        x = xbuf[in_slot]
        gate = _bdot(x, wgb_ref[...])
        up = _bdot(x, wub_ref[...])
        obuf[out_slot] = (gate * jax.nn.sigmoid(gate) * up).astype(obuf.dtype)

    _tile_pipeline(off_ref, cnt_ref, pl.num_programs(0), n_tiles, in_copy, out_copy, zero_copy,
                   compute)


def _moe_up(xs, wg, wu, tile_off, tm):
    r, d = xs.shape
    f = wg.shape[2]
    tf = min(TF_MOE, f)
    kern = functools.partial(_moe_up_kernel, tm=tm, tf=tf, n_tiles=r // tm)
    return pl.pallas_call(
        kern,
        grid_spec=pltpu.PrefetchScalarGridSpec(
            num_scalar_prefetch=1,
            grid=(f // tf, N_EXPERTS),
            in_specs=[pl.BlockSpec(memory_space=pl.ANY),
                      pl.BlockSpec((1, d, tf), lambda j, e, off: (e, 0, j)),
                      pl.BlockSpec((1, d, tf), lambda j, e, off: (e, 0, j))],
            out_specs=pl.BlockSpec(memory_space=pl.ANY),
            scratch_shapes=[pltpu.VMEM((d, tf), BF16), pltpu.VMEM((d, tf), BF16),
                            pltpu.VMEM((IN_RING, tm, d), BF16),
                            pltpu.VMEM((OUT_RING, tm, tf), BF16),
                            pltpu.VMEM((tm, tf), BF16),
                            pltpu.SMEM((1,), jnp.int32),
                            pltpu.SemaphoreType.DMA((IN_RING,)),
                            pltpu.SemaphoreType.DMA((OUT_RING,)),
                            pltpu.SemaphoreType.DMA((1,))]),
        out_shape=jax.ShapeDtypeStruct((r, f), BF16),
        compiler_params=_cparams(("arbitrary", "arbitrary"), 56),
        name="moe_gate_up",
    )(tile_off, xs, wg, wu)


def _moe_down_kernel(off_ref, act_hbm, wd_ref, y_hbm, wdb_ref, abuf, obuf, zbuf, cnt_ref,
                     asem, osem, zsem, *, tm, tn, n_tiles):
    j = pl.program_id(0)
    e = pl.program_id(1)

    @pl.when((j == 0) & (e == 0))
    def _():
        zbuf[...] = jnp.zeros_like(zbuf)

    @pl.when(off_ref[e + 1] > off_ref[e])
    def _():
        wdb_ref[...] = wd_ref[0].astype(BF16)

    def in_copy(t, slot):
        return pltpu.make_async_copy(act_hbm.at[pl.ds(t * tm, tm), :], abuf.at[slot], asem.at[slot])

    def out_tile(t):
        return y_hbm.at[pl.ds(t * tm, tm), pl.ds(j * tn, tn)]

    def out_copy(t, slot):
        return pltpu.make_async_copy(obuf.at[slot], out_tile(t), osem.at[slot])

    def zero_copy(t):
        return pltpu.make_async_copy(zbuf, out_tile(t), zsem.at[0])

    def compute(in_slot, out_slot):
        obuf[out_slot] = _bdot(abuf[in_slot], wdb_ref[...])

    _tile_pipeline(off_ref, cnt_ref, pl.num_programs(0), n_tiles, in_copy, out_copy, zero_copy,
                   compute)


def _moe_down(act, wd, tile_off, tm):
    r, f = act.shape
    d = wd.shape[2]
    tn = min(TN_MOE, d)
    kern = functools.partial(_moe_down_kernel, tm=tm, tn=tn, n_tiles=r // tm)
    return pl.pallas_call(
        kern,
        grid_spec=pltpu.PrefetchScalarGridSpec(
            num_scalar_prefetch=1,
            grid=(d // tn, N_EXPERTS),
            in_specs=[pl.BlockSpec(memory_space=pl.ANY),
                      pl.BlockSpec((1, f, tn), lambda j, e, off: (e, 0, j))],
            out_specs=pl.BlockSpec(memory_space=pl.ANY),
            scratch_shapes=[pltpu.VMEM((f, tn), BF16),
                            pltpu.VMEM((IN_RING, tm, f), BF16),
                            pltpu.VMEM((OUT_RING, tm, tn), F32),
                            pltpu.VMEM((tm, tn), F32),
                            pltpu.SMEM((1,), jnp.int32),
                            pltpu.SemaphoreType.DMA((IN_RING,)),
                            pltpu.SemaphoreType.DMA((OUT_RING,)),
                            pltpu.SemaphoreType.DMA((1,))]),
        out_shape=jax.ShapeDtypeStruct((r, d), F32),
        compiler_params=_cparams(("arbitrary", "arbitrary"), 56),
        name="moe_down",
    )(tile_off, act, wd)


def _routing_tables(gates, sel, rank, counts, tm):
    ntok = gates.shape[0]
    n_tiles = (2 * ntok) // tm + N_EXPERTS
    cnt = counts[0, :N_EXPERTS].astype(jnp.int32)
    padded = ((cnt + tm - 1) // tm) * tm
    ends = jnp.cumsum(padded)
    offs = ends - padded
    selb = sel[:, :N_EXPERTS] > 0.5
    pos = offs[None, :] + rank[:, :N_EXPERTS].astype(jnp.int32)
    e_lo = jnp.argmax(selb, axis=1)
    e_hi = N_EXPERTS - 1 - jnp.argmax(selb[:, ::-1], axis=1)
    take = lambda a, e: jnp.take_along_axis(a, e[:, None], axis=1)[:, 0]
    pos0, pos1 = take(pos, e_lo), take(pos, e_hi)
    g8 = gates[:, :N_EXPERTS]
    g0, g1 = take(g8, e_lo), take(g8, e_hi)
    tok = jnp.arange(ntok, dtype=jnp.int32)
    pos_all = jnp.concatenate([pos0, pos1]).astype(jnp.int32)
    src = jnp.zeros((n_tiles * tm,), jnp.int32).at[pos_all].set(
        jnp.concatenate([tok, tok]), unique_indices=True)
    tile_off = jnp.concatenate([jnp.zeros((1,), jnp.int32), (ends // tm).astype(jnp.int32)])
    return src, pos_all, g0[:, None], g1[:, None], tile_off


def _moe_final(x, g, router_w, wg, wu, wd, final_norm):
    h, gates, sel, rank, counts = _router(x, g, router_w)
    tm = min(TM_MOE, x.shape[0])
    assert 2 * x.shape[0] // tm >= IN_RING, "the tile ring needs at least IN_RING used row tiles"
    src, pos_all, g0, g1, tile_off = _routing_tables(gates, sel, rank, counts, tm)
    xs = _group_rows(h, src)
    act = _moe_up(xs, wg, wu, tile_off, tm)
    y = _moe_down(act, wd, tile_off, tm)
    return _combine_final_norm(x, y, pos_all, g0, g1, final_norm)


def _forward(x, norm_mix_even, w_in_even, conv3_w, w_out_even, norm_ffn_even,
             ffn_w_gate, ffn_w_up, ffn_w_down, norm_mix_odd, cf_pw1_w, cf_pw1_b,
             cf_dw_w, cf_dw_b, cf_ln_g, cf_ln_b, cf_pw2_w, cf_pw2_b, norm_moe,
             router_w, moe_w_gate, moe_w_up, moe_w_down, final_norm):
    batch, seq, d = x.shape
    xt = x.reshape(batch * seq, d)
    ncol = SC_WIDTH // LANES
    q_lo, q_hi = 3 * SC_WIDTH, 3 * SC_WIDTH + SB_HEADS * SB_HEAD_DIM
    col = jnp.arange(w_in_even.shape[2])
    col_scale = jnp.where((col >= q_lo) & (col < q_hi), -(SB_HEAD_DIM ** -0.5) * LOG2E, 1.0)
    p = _norm_matmul(xt, norm_mix_even[0], w_in_even[0], col_scale.astype(F32), BF16)
    ysb = _stick_breaking(p, batch, seq, 3 * ncol, 3 * ncol + SB_HEADS, 3 * ncol + 2 * SB_HEADS)
    xt = _mix_out_proj(p, ysb, conv3_w[0], w_out_even[0], xt, seq)
    xt = _dense_swiglu(xt, norm_ffn_even[0], ffn_w_gate[0], ffn_w_up[0], ffn_w_down[0])
    u = _norm_glu(xt, norm_mix_odd[0], cf_pw1_w[0], cf_pw1_b[0])
    v = _dwconv_ln_silu(u, cf_dw_w[0], cf_dw_b[0], cf_ln_g[0], cf_ln_b[0], seq)
    xt = _matmul_bias_res(v, cf_pw2_w[0], cf_pw2_b[0], xt)
    out = _moe_final(xt, norm_moe[0], router_w[0], moe_w_gate[0], moe_w_up[0], moe_w_down[0],
                     final_norm)
    return out.reshape(batch, seq, d)


def kernel(x, norm_mix_even, w_in_even, conv3_w, w_out_even, norm_ffn_even, ffn_w_gate, ffn_w_up, ffn_w_down, norm_mix_odd, cf_pw1_w, cf_pw1_b, cf_dw_w, cf_dw_b, cf_ln_g, cf_ln_b, cf_pw2_w, cf_pw2_b, norm_moe, router_w, moe_w_gate, moe_w_up, moe_w_down, final_norm):
    return _forward(x, norm_mix_even, w_in_even, conv3_w, w_out_even, norm_ffn_even,
                    ffn_w_gate, ffn_w_up, ffn_w_down, norm_mix_odd, cf_pw1_w, cf_pw1_b,
                    cf_dw_w, cf_dw_b, cf_ln_g, cf_ln_b, cf_pw2_w, cf_pw2_b, norm_moe,
                    router_w, moe_w_gate, moe_w_up, moe_w_down, final_norm)
```

```python
import functools

import jax
import jax.numpy as jnp
from jax import lax
from jax.experimental import pallas as pl
from jax.experimental.pallas import tpu as pltpu

F32 = jnp.float32
BF16 = jnp.bfloat16

NORM_EPS = 1e-6
SC_WIDTH = 1024
SC_KERNEL = 3
SB_HEADS = 8
SB_HEAD_DIM = 128
CF_KERNEL = 31
N_EXPERTS = 8
LANES = 128
SUBLANES = 8
MIB = 1024 * 1024

TM_PROJ = 1024
TN_PROJ = 1024
TM_RESIDENT = 512
TN_GLU = 512
TM_FFN = 1024
TF_FFN = 256
TQ_ATTN = 512
ATTN_UNROLL = 4
LOG2E = 1.4426950408889634
TM_CONV = 256
CONV_HALO = 32
SC_HALO = 16
TM_ROUTE = 512
TM_MOE = 256
TF_MOE = 1024
TN_MOE = 512
IN_RING = 3
OUT_RING = 2
TG_GATHER = 256
TC_COMBINE = 128


def _cparams(semantics, vmem_mib):
    return pltpu.CompilerParams(dimension_semantics=semantics,
                                vmem_limit_bytes=vmem_mib * MIB)


def _rmsnorm_to(x_ref, g_ref, h_ref, chunk=128):
    g = g_ref[...]

    def body(c, carry):
        r = pl.multiple_of(c * chunk, chunk)
        x = x_ref[pl.ds(r, chunk), :]
        ms = jnp.mean(x * x, axis=-1, keepdims=True)
        h_ref[pl.ds(r, chunk), :] = (x * lax.rsqrt(ms + NORM_EPS) * g).astype(h_ref.dtype)
        return carry

    lax.fori_loop(0, x_ref.shape[0] // chunk, body, 0)


def _bdot(a, b):
    return jnp.dot(a, b, preferred_element_type=F32)


def _norm_mm_kernel(x_ref, g_ref, w_ref, cs_ref, o_ref, h_ref):
    @pl.when(pl.program_id(1) == 0)
    def _():
        _rmsnorm_to(x_ref, g_ref, h_ref)

    o_ref[...] = (_bdot(h_ref[...], w_ref[...].astype(BF16)) * cs_ref[...]).astype(o_ref.dtype)


def _norm_matmul(x, g, w, col_scale, out_dtype):
    m, k = x.shape
    n = w.shape[1]
    tm, tn = min(TM_PROJ, m), min(TN_PROJ, n)
    return pl.pallas_call(
        _norm_mm_kernel,
        grid=(m // tm, n // tn),
        in_specs=[pl.BlockSpec((tm, k), lambda i, j: (i, 0)),
                  pl.BlockSpec((1, k), lambda i, j: (0, 0)),
                  pl.BlockSpec((k, tn), lambda i, j: (0, j)),
                  pl.BlockSpec((1, tn), lambda i, j: (0, j))],
        out_specs=pl.BlockSpec((tm, tn), lambda i, j: (i, j)),
        out_shape=jax.ShapeDtypeStruct((m, n), out_dtype),
        scratch_shapes=[pltpu.VMEM((tm, k), BF16)],
        compiler_params=_cparams(("parallel", "arbitrary"), 56),
        name="norm_in_proj",
    )(x, g.reshape(1, k), w, col_scale.reshape(1, n))


def _norm_glu_kernel(x_ref, g_ref, wa_ref, wg_ref, ba_ref, bg_ref, o_ref, h_ref):
    @pl.when(pl.program_id(1) == 0)
    def _():
        _rmsnorm_to(x_ref, g_ref, h_ref)

    h = h_ref[...]
    a = _bdot(h, wa_ref[...].astype(BF16)) + ba_ref[...]
    gate = _bdot(h, wg_ref[...].astype(BF16)) + bg_ref[...]
    o_ref[...] = (a * jax.nn.sigmoid(gate)).astype(o_ref.dtype)


def _norm_glu(x, g, w, b):
    m, k = x.shape
    n = w.shape[1] // 2
    tm, tn = min(TM_PROJ, m), min(TN_GLU, n)
    nb = n // tn
    b2 = b.reshape(1, 2 * n)
    return pl.pallas_call(
        _norm_glu_kernel,
        grid=(m // tm, nb),
        in_specs=[pl.BlockSpec((tm, k), lambda i, j: (i, 0)),
                  pl.BlockSpec((1, k), lambda i, j: (0, 0)),
                  pl.BlockSpec((k, tn), lambda i, j: (0, j)),
                  pl.BlockSpec((k, tn), lambda i, j: (0, j + nb)),
                  pl.BlockSpec((1, tn), lambda i, j: (0, j)),
                  pl.BlockSpec((1, tn), lambda i, j: (0, j + nb))],
        out_specs=pl.BlockSpec((tm, tn), lambda i, j: (i, j)),
        out_shape=jax.ShapeDtypeStruct((m, n), F32),
        scratch_shapes=[pltpu.VMEM((tm, k), BF16)],
        compiler_params=_cparams(("parallel", "arbitrary"), 56),
        name="norm_pw1_glu",
    )(x, g.reshape(1, k), w, w, b2, b2)


def _mm_bias_res_kernel(a_ref, w_ref, b_ref, r_ref, o_ref, wb_ref):
    @pl.when(pl.program_id(0) == 0)
    def _():
        wb_ref[...] = w_ref[...].astype(BF16)

    o_ref[...] = r_ref[...] + _bdot(a_ref[...], wb_ref[...]) + b_ref[...]


def _matmul_bias_res(a, w, b, res):
    m, k = a.shape
    n = w.shape[1]
    tm = min(TM_RESIDENT, m)
    return pl.pallas_call(
        _mm_bias_res_kernel,
        grid=(m // tm,),
        in_specs=[pl.BlockSpec((tm, k), lambda i: (i, 0)),
                  pl.BlockSpec((k, n), lambda i: (0, 0), pipeline_mode=pl.Buffered(1)),
                  pl.BlockSpec((1, n), lambda i: (0, 0)),
                  pl.BlockSpec((tm, n), lambda i: (i, 0))],
        out_specs=pl.BlockSpec((tm, n), lambda i: (i, 0)),
        out_shape=jax.ShapeDtypeStruct((m, n), F32),
        scratch_shapes=[pltpu.VMEM((k, n), BF16)],
        compiler_params=_cparams(("arbitrary",), 56),
        name="pw2_residual",
    )(a, w, b.reshape(1, n), res)


def _attn_kernel(q_ref, k_ref, v_ref, o_ref, u_ref, c_ref, acc_ref, nz_ref, cm_ref, *, tq):
    hk = LANES
    tk = 2 * hk
    q0 = pl.program_id(2) * tq
    q = q_ref[...]
    c_ref[...] = jnp.zeros_like(c_ref)
    acc_ref[...] = jnp.zeros_like(acc_ref)

    jj = lax.broadcasted_iota(jnp.int32, (2 * hk, 2 * hk), 0) & (hk - 1)
    ss = lax.broadcasted_iota(jnp.int32, (2 * hk, 2 * hk), 1)
    u_ref[...] = jnp.where((ss >= hk) | (jj >= ss), 1.0, 0.0).astype(BF16)

    def half_sums(lk):
        hi = lk.astype(BF16)
        lo = (lk - hi.astype(F32)).astype(BF16)
        return _bdot(jnp.concatenate([hi, lo], axis=1), u_ref[...])

    def scores(kstart, slot, mask):
        kb = k_ref[pl.ds(kstart, tk), :]
        nz = lax.dot_general(q, kb, (((1,), (1,)), ((), ())), preferred_element_type=F32)
        neg_abs = lax.bitcast_convert_type(
            lax.bitcast_convert_type(nz, jnp.uint32) | jnp.uint32(0x80000000), F32)
        log_keep = jnp.minimum(nz, 0.0) - jnp.log2(1.0 + jnp.exp2(neg_abs))
        if mask is not None:
            log_keep = jnp.where(mask, log_keep, 0.0)
        nz_ref[slot] = nz
        cm_ref[slot, 1] = half_sums(log_keep[:, hk:])
        cm_ref[slot, 0] = half_sums(log_keep[:, :hk])

    def weights(kstart, slot, mask):
        c = c_ref[...]
        nz = nz_ref[slot]
        cm1 = cm_ref[slot, 1]
        cm0 = cm_ref[slot, 0]
        logit1 = (c + cm1[:, :hk]) - nz[:, hk:]
        c = c + cm1[:, hk:]
        logit0 = (c + cm0[:, :hk]) - nz[:, :hk]
        c_ref[...] = c + cm0[:, hk:]
        a = jnp.exp2(jnp.concatenate([logit0, logit1], axis=1))
        if mask is not None:
            a = jnp.where(mask, a, 0.0)
        acc_ref[...] += _bdot(a.astype(BF16), v_ref[pl.ds(kstart, tk), :])

    rows = lax.broadcasted_iota(jnp.int32, (tq, tk), 0)
    cols = lax.broadcasted_iota(jnp.int32, (tq, tk), 1)
    nd = tq // tk
    assert nd == 2 and ATTN_UNROLL % 2 == 0
    diag = [(pl.multiple_of(q0 + d * tk, tk), (cols + d * tk) < rows)
            for d in reversed(range(nd))]
    nb = q0 // tk

    def below(j):
        return pl.multiple_of(q0 - (j + 1) * tk, tk)

    scores(diag[0][0], 0, diag[0][1])
    for i in range(1, nd):
        scores(diag[i][0], i % 2, diag[i][1])
        weights(diag[i - 1][0], (i - 1) % 2, diag[i - 1][1])
    last_k, last_mask = diag[nd - 1]

    @pl.when(nb == 0)
    def _():
        weights(last_k, (nd - 1) % 2, last_mask)

    @pl.when(nb > 0)
    def _():
        scores(below(0), nd % 2, None)
        weights(last_k, (nd - 1) % 2, last_mask)

        def steps(j0, count):
            for k in range(count):
                weights(below(j0 + k), (nd + k) % 2, None)
                scores(below(j0 + k + 1), (nd + k + 1) % 2, None)

        def body(i, carry):
            steps(ATTN_UNROLL * i, ATTN_UNROLL)
            return carry

        trips = (nb - 1) // ATTN_UNROLL
        lax.fori_loop(0, trips, body, 0)
        rest = nb - 1 - ATTN_UNROLL * trips
        for r in range(1, ATTN_UNROLL, 2):
            @pl.when(rest == r)
            def _():
                steps(nb - 1 - r, r)
                weights(below(nb - 1), (nd + r) % 2, None)

    o_ref[...] = acc_ref[...].astype(o_ref.dtype)


def _stick_breaking(p, batch, seq, col_q, col_k, col_v):
    tq = min(TQ_ATTN, seq)
    nq = seq // tq
    dh = SB_HEAD_DIM
    kern = functools.partial(_attn_kernel, tq=tq)
    return pl.pallas_call(
        kern,
        grid=(batch, SB_HEADS, nq),
        in_specs=[pl.BlockSpec((tq, dh), lambda b, h, i: (b * nq + i, col_q + h)),
                  pl.BlockSpec((seq, dh), lambda b, h, i: (b, col_k + h)),
                  pl.BlockSpec((seq, dh), lambda b, h, i: (b, col_v + h))],
        out_specs=pl.BlockSpec((tq, dh), lambda b, h, i: (b * nq + i, h)),
        out_shape=jax.ShapeDtypeStruct((batch * seq, SB_HEADS * dh), BF16),
        scratch_shapes=[pltpu.VMEM((2 * LANES, 2 * LANES), BF16),
                        pltpu.VMEM((tq, LANES), F32),
                        pltpu.VMEM((tq, dh), F32),
                        pltpu.VMEM((2, tq, 2 * LANES), F32),
                        pltpu.VMEM((2, 2, tq, 2 * LANES), F32)],
        compiler_params=_cparams(("parallel", "parallel", "arbitrary"), 32),
        name="stick_breaking_attention",
    )(p, p, p)


def _mix_out_kernel(pb_ref, pc_ref, ph_ref, hc_ref, hh_ref, cw_ref, ysb_ref, w_ref, x_ref,
                    o_ref, mix_ref, u_ref, wb_ref, *, tm, seq, chunk):
    i = pl.program_id(0)

    @pl.when(i == 0)
    def _():
        wb_ref[...] = w_ref[...].astype(BF16)

    halo = hc_ref[...].astype(F32) * hh_ref[...].astype(F32)
    u_ref[0:SC_HALO, :] = jnp.where((i * tm) % seq == 0, 0.0, halo)
    for r in range(0, tm, chunk):
        u_ref[SC_HALO + r:SC_HALO + r + chunk, :] = (
            pc_ref[r:r + chunk, :].astype(F32) * ph_ref[r:r + chunk, :].astype(F32))
    w0, w1, w2 = cw_ref[0:1, :], cw_ref[1:2, :], cw_ref[2:3, :]
    for r in range(0, tm, chunk):
        base = SC_HALO + r
        conv = (w2 * u_ref[base:base + chunk, :]
                + w1 * u_ref[base - 1:base - 1 + chunk, :]
                + w0 * u_ref[base - 2:base - 2 + chunk, :])
        ysc = pb_ref[r:r + chunk, :].astype(F32) * conv
        mix_ref[r:r + chunk, 0:SC_WIDTH] = ysc.astype(BF16)
    mix_ref[:, SC_WIDTH:] = ysb_ref[...]

    o_ref[...] = x_ref[...] + _bdot(mix_ref[...], wb_ref[...])


def _mix_out_proj(p, ysb, conv_w, w_out, x, seq):
    m = x.shape[0]
    d = w_out.shape[1]
    kdim = w_out.shape[0]
    tm = min(TM_RESIDENT, seq)
    hb = tm // SC_HALO
    kern = functools.partial(_mix_out_kernel, tm=tm, seq=seq, chunk=min(128, tm))
    return pl.pallas_call(
        kern,
        grid=(m // tm,),
        in_specs=[pl.BlockSpec((tm, SC_WIDTH), lambda i: (i, 0)),
                  pl.BlockSpec((tm, SC_WIDTH), lambda i: (i, 1)),
                  pl.BlockSpec((tm, SC_WIDTH), lambda i: (i, 2)),
                  pl.BlockSpec((SC_HALO, SC_WIDTH), lambda i: (jnp.maximum(i * hb - 1, 0), 1)),
                  pl.BlockSpec((SC_HALO, SC_WIDTH), lambda i: (jnp.maximum(i * hb - 1, 0), 2)),
                  pl.BlockSpec((SC_KERNEL, SC_WIDTH), lambda i: (0, 0)),
                  pl.BlockSpec((tm, SC_WIDTH), lambda i: (i, 0)),
                  pl.BlockSpec((kdim, d), lambda i: (0, 0), pipeline_mode=pl.Buffered(1)),
                  pl.BlockSpec((tm, d), lambda i: (i, 0))],
        out_specs=pl.BlockSpec((tm, d), lambda i: (i, 0)),
        out_shape=jax.ShapeDtypeStruct((m, d), F32),
        scratch_shapes=[pltpu.VMEM((tm, kdim), BF16),
                        pltpu.VMEM((tm + SC_HALO, SC_WIDTH), F32),
                        pltpu.VMEM((kdim, d), BF16)],
        compiler_params=_cparams(("arbitrary",), 56),
        name="shortconv_out_proj",
    )(p, p, p, p, p, conv_w, ysb, w_out, x)


def _ffn_kernel(x_ref, g_ref, wg_ref, wu_ref, wd_ref, o_ref, h_ref):
    @pl.when(pl.program_id(1) == 0)
    def _():
        _rmsnorm_to(x_ref, g_ref, h_ref)
        o_ref[...] = x_ref[...]

    h = h_ref[...]
    gate = _bdot(h, wg_ref[...].astype(BF16))
    up = _bdot(h, wu_ref[...].astype(BF16))
    act = (gate * jax.nn.sigmoid(gate) * up).astype(BF16)
    o_ref[...] += _bdot(act, wd_ref[...].astype(BF16))


def _dense_swiglu(x, g, wg, wu, wd):
    m, d = x.shape
    f = wg.shape[1]
    tm, tf = min(TM_FFN, m), min(TF_FFN, f)
    return pl.pallas_call(
        _ffn_kernel,
        grid=(m // tm, f // tf),
        in_specs=[pl.BlockSpec((tm, d), lambda i, j: (i, 0), pipeline_mode=pl.Buffered(1)),
                  pl.BlockSpec((1, d), lambda i, j: (0, 0)),
                  pl.BlockSpec((d, tf), lambda i, j: (0, j)),
                  pl.BlockSpec((d, tf), lambda i, j: (0, j)),
                  pl.BlockSpec((tf, d), lambda i, j: (j, 0))],
        out_specs=pl.BlockSpec((tm, d), lambda i, j: (i, 0)),
        out_shape=jax.ShapeDtypeStruct((m, d), F32),
        scratch_shapes=[pltpu.VMEM((tm, d), BF16)],
        compiler_params=_cparams(("parallel", "arbitrary"), 60),
        name="dense_swiglu",
    )(x, g.reshape(1, d), wg, wu, wd)


def _dwconv_ln_kernel(u_ref, halo_ref, w_ref, b_ref, lg_ref, lb_ref, o_ref, buf_ref, conv_ref,
                      *, tm, seq, rchunk):
    i = pl.program_id(0)
    buf_ref[0:CONV_HALO, :] = jnp.where((i * tm) % seq == 0, 0.0, halo_ref[...])
    buf_ref[CONV_HALO:, :] = u_ref[...]
    nlt = u_ref.shape[1] // LANES
    first = CONV_HALO - (CF_KERNEL - 1)

    def lane_tile(c, carry):
        lanes = pl.ds(pl.multiple_of(c * LANES, LANES), LANES)
        for r in range(0, tm, rchunk):
            acc = None
            for m in range(SUBLANES):
                rows = rchunk + (SUBLANES if m else 0)
                g = None
                for k in range(CF_KERNEL):
                    if (first + k) % SUBLANES != m:
                        continue
                    base = r + first + k - m
                    term = w_ref[k:k + 1, lanes] * buf_ref[base:base + rows, lanes]
                    g = term if g is None else g + term
                if m:
                    g = pltpu.roll(g, rows - m, axis=0)[:rchunk]
                acc = g if acc is None else acc + g
            conv_ref[r:r + rchunk, lanes] = acc
        return carry

    lax.fori_loop(0, nlt, lane_tile, 0)

    bias, lg, lb = b_ref[...], lg_ref[...], lb_ref[...]
    for r in range(0, tm, rchunk):
        y = conv_ref[r:r + rchunk, :] + bias
        mu = jnp.mean(y, axis=-1, keepdims=True)
        yc = y - mu
        var = jnp.mean(yc * yc, axis=-1, keepdims=True)
        t = yc * lax.rsqrt(var + NORM_EPS) * lg + lb
        o_ref[r:r + rchunk, :] = (t * jax.nn.sigmoid(t)).astype(o_ref.dtype)


def _dwconv_ln_silu(u, w, b, lg, lb, seq):
    m, c = u.shape
    tm = min(TM_CONV, seq)
    hb = tm // CONV_HALO
    kern = functools.partial(_dwconv_ln_kernel, tm=tm, seq=seq, rchunk=min(128, tm))
    return pl.pallas_call(
        kern,
        grid=(m // tm,),
        in_specs=[pl.BlockSpec((tm, c), lambda i: (i, 0)),
                  pl.BlockSpec((CONV_HALO, c), lambda i: (jnp.maximum(i * hb - 1, 0), 0)),
                  pl.BlockSpec((CF_KERNEL, c), lambda i: (0, 0)),
                  pl.BlockSpec((1, c), lambda i: (0, 0)),
                  pl.BlockSpec((1, c), lambda i: (0, 0)),
                  pl.BlockSpec((1, c), lambda i: (0, 0))],
        out_specs=pl.BlockSpec((tm, c), lambda i: (i, 0)),
        out_shape=jax.ShapeDtypeStruct((m, c), BF16),
        scratch_shapes=[pltpu.VMEM((tm + CONV_HALO, c), F32),
                        pltpu.VMEM((tm, c), F32)],
        compiler_params=_cparams(("parallel",), 32),
        name="dwconv_ln_silu",
    )(u, u, w, b.reshape(1, c), lg.reshape(1, c), lb.reshape(1, c))


def _router_kernel(x_ref, g_ref, rw_ref, h_ref, gates_ref, sel_ref, rank_ref, cnt_ref, carry_ref,
                   *, tm):
    @pl.when(pl.program_id(0) == 0)
    def _():
        carry_ref[...] = jnp.zeros_like(carry_ref)

    _rmsnorm_to(x_ref, g_ref, h_ref)
    logits = jnp.dot(h_ref[...], rw_ref[...], preferred_element_type=F32,
                     precision=lax.Precision.HIGHEST)
    lane = lax.broadcasted_iota(jnp.int32, logits.shape, 1).astype(F32)
    neg = jnp.float32(-jnp.inf)
    logits = jnp.where(lane < N_EXPERTS, logits, neg)
    m1 = jnp.max(logits, axis=-1, keepdims=True)
    i1 = jnp.min(jnp.where(logits == m1, lane, float(LANES)), axis=-1, keepdims=True)
    rest = jnp.where(lane == i1, neg, logits)
    m2 = jnp.max(rest, axis=-1, keepdims=True)
    i2 = jnp.min(jnp.where(rest == m2, lane, float(LANES)), axis=-1, keepdims=True)
    e2 = jnp.exp(m2 - m1)
    w1 = 1.0 / (1.0 + e2)
    w2 = e2 / (1.0 + e2)
    gates_ref[...] = jnp.where(lane == i1, w1, 0.0) + jnp.where(lane == i2, w2, 0.0)
    sel = jnp.where((lane == i1) | (lane == i2), 1.0, 0.0)
    sel_ref[...] = sel
    rr = lax.broadcasted_iota(jnp.int32, (tm, tm), 0)
    cc = lax.broadcasted_iota(jnp.int32, (tm, tm), 1)
    tri = jnp.where(cc < rr, 1.0, 0.0).astype(BF16)
    rank_ref[...] = _bdot(tri, sel.astype(BF16)) + carry_ref[...]
    carry_ref[...] += jnp.sum(sel, axis=0, keepdims=True)
    cnt_ref[...] = carry_ref[...]


def _router(x, g, router_w):
    m, d = x.shape
    tm = min(TM_ROUTE, m)
    rw = jnp.zeros((d, LANES), F32).at[:, :N_EXPERTS].set(router_w)
    kern = functools.partial(_router_kernel, tm=tm)
    wide = jax.ShapeDtypeStruct((m, LANES), F32)
    return pl.pallas_call(
        kern,
        grid=(m // tm,),
        in_specs=[pl.BlockSpec((tm, d), lambda i: (i, 0)),
                  pl.BlockSpec((1, d), lambda i: (0, 0)),
                  pl.BlockSpec((d, LANES), lambda i: (0, 0))],
        out_specs=[pl.BlockSpec((tm, d), lambda i: (i, 0)),
                   pl.BlockSpec((tm, LANES), lambda i: (i, 0)),
                   pl.BlockSpec((tm, LANES), lambda i: (i, 0)),
                   pl.BlockSpec((tm, LANES), lambda i: (i, 0)),
                   pl.BlockSpec((1, LANES), lambda i: (0, 0))],
        out_shape=[jax.ShapeDtypeStruct((m, d), F32), wide, wide, wide,
                   jax.ShapeDtypeStruct((1, LANES), F32)],
        scratch_shapes=[pltpu.VMEM((1, LANES), F32)],
        compiler_params=_cparams(("arbitrary",), 32),
        name="router_top2",
    )(x, g.reshape(1, d), rw)


def _start_row_gather(src_hbm, idx_ref, base, n, dst_ref, sem):
    def body(r, carry):
        t = idx_ref[base + r]
        pltpu.make_async_copy(src_hbm.at[pl.ds(t, 1), :], dst_ref.at[pl.ds(r, 1), :], sem).start()
        return carry

    lax.fori_loop(0, n, body, 0, unroll=8)


def _wait_row_gather(src_hbm, n, dst_ref, sem):
    def body(r, carry):
        pltpu.make_async_copy(src_hbm.at[pl.ds(0, 1), :], dst_ref.at[pl.ds(r, 1), :], sem).wait()
        return carry

    lax.fori_loop(0, n, body, 0, unroll=8)


def _group_rows_kernel(src_ref, h_hbm, o_ref, buf_ref, sem_ref, *, tg):
    i = pl.program_id(0)
    n = pl.num_programs(0)
    slot = i % 2

    @pl.when(i == 0)
    def _():
        _start_row_gather(h_hbm, src_ref, 0, tg, buf_ref.at[0], sem_ref.at[0])

    @pl.when(i + 1 < n)
    def _():
        _start_row_gather(h_hbm, src_ref, (i + 1) * tg, tg, buf_ref.at[1 - slot], sem_ref.at[1 - slot])

    _wait_row_gather(h_hbm, tg, buf_ref.at[slot], sem_ref.at[slot])
    o_ref[...] = buf_ref[slot].astype(o_ref.dtype)


def _group_rows(h, src):
    d = h.shape[1]
    r = src.shape[0]
    tg = min(TG_GATHER, r)
    kern = functools.partial(_group_rows_kernel, tg=tg)
    return pl.pallas_call(
        kern,
        grid_spec=pltpu.PrefetchScalarGridSpec(
            num_scalar_prefetch=1,
            grid=(r // tg,),
            in_specs=[pl.BlockSpec(memory_space=pl.ANY)],
            out_specs=pl.BlockSpec((tg, d), lambda i, s: (i, 0)),
            scratch_shapes=[pltpu.VMEM((2, tg, d), F32),
                            pltpu.SemaphoreType.DMA((2,))]),
        out_shape=jax.ShapeDtypeStruct((r, d), BF16),
        compiler_params=_cparams(("arbitrary",), 32),
        name="group_rows_by_expert",
    )(src, h)


def _combine_kernel(pos_ref, x_ref, g0_ref, g1_ref, fn_ref, y_hbm, o_ref, buf_ref, sem_ref,
                    *, tc, ntok):
    i = pl.program_id(0)
    n = pl.num_programs(0)
    slot = i % 2

    def start(step, s):
        _start_row_gather(y_hbm, pos_ref, step * tc, tc, buf_ref.at[s, 0], sem_ref.at[s, 0])
        _start_row_gather(y_hbm, pos_ref, ntok + step * tc, tc, buf_ref.at[s, 1], sem_ref.at[s, 1])

    @pl.when(i == 0)
    def _():
        start(0, 0)

    @pl.when(i + 1 < n)
    def _():
        start(i + 1, 1 - slot)

    _wait_row_gather(y_hbm, tc, buf_ref.at[slot, 0], sem_ref.at[slot, 0])
    _wait_row_gather(y_hbm, tc, buf_ref.at[slot, 1], sem_ref.at[slot, 1])
    x = x_ref[...] + g0_ref[...] * buf_ref[slot, 0] + g1_ref[...] * buf_ref[slot, 1]
    ms = jnp.mean(x * x, axis=-1, keepdims=True)
    o_ref[...] = x * lax.rsqrt(ms + NORM_EPS) * fn_ref[...]


def _combine_final_norm(x, y, pos, g0, g1, final_norm):
    m, d = x.shape
    tc = min(TC_COMBINE, m)
    kern = functools.partial(_combine_kernel, tc=tc, ntok=m)
    return pl.pallas_call(
        kern,
        grid_spec=pltpu.PrefetchScalarGridSpec(
            num_scalar_prefetch=1,
            grid=(m // tc,),
            in_specs=[pl.BlockSpec((tc, d), lambda i, s: (i, 0)),
                      pl.BlockSpec((tc, 1), lambda i, s: (i, 0)),
                      pl.BlockSpec((tc, 1), lambda i, s: (i, 0)),
                      pl.BlockSpec((1, d), lambda i, s: (0, 0)),
                      pl.BlockSpec(memory_space=pl.ANY)],
            out_specs=pl.BlockSpec((tc, d), lambda i, s: (i, 0)),
            scratch_shapes=[pltpu.VMEM((2, 2, tc, d), F32),
                            pltpu.SemaphoreType.DMA((2, 2))]),
        out_shape=jax.ShapeDtypeStruct((m, d), F32),
        compiler_params=_cparams(("arbitrary",), 32),
        name="moe_combine_final_norm",
    )(pos, x, g0, g1, final_norm.reshape(1, d), y)


def _tile_pipeline(off_ref, cnt_ref, n_col_tiles, n_tiles, in_copy, out_copy, zero_copy, compute):
    j = pl.program_id(0)
    e = pl.program_id(1)
    n_used = off_ref[N_EXPERTS]
    last_col = j == n_col_tiles - 1

    @pl.when((j == 0) & (e == 0))
    def _():
        cnt_ref[0] = 0
        for k in range(IN_RING):
            in_copy(k, k).start()

    def body(t, g):
        in_slot = g % IN_RING
        out_slot = g % OUT_RING
        in_copy(t, in_slot).wait()

        @pl.when(g >= OUT_RING)
        def _():
            out_copy(0, out_slot).wait()

        compute(in_slot, out_slot)
        out_copy(t, out_slot).start()
        ahead = t + IN_RING
        wraps = ahead >= n_used

        @pl.when(jnp.logical_not(wraps & last_col))
        def _():
            in_copy(jnp.where(wraps, ahead - n_used, ahead), in_slot).start()

        return g + 1

    g = lax.fori_loop(off_ref[e], off_ref[e + 1], body, cnt_ref[0])
    cnt_ref[0] = g

    @pl.when(e == N_EXPERTS - 1)
    def _():
        def tail(t, carry):
            zero_copy(t).start()
            zero_copy(t).wait()
            return carry

        lax.fori_loop(n_used, n_tiles, tail, 0)

        @pl.when(last_col)
        def _():
            for back in range(OUT_RING, 0, -1):
                @pl.when(g >= back)
                def _():
                    out_copy(0, (g - back) % OUT_RING).wait()


def _moe_up_kernel(off_ref, xs_hbm, wg_ref, wu_ref, act_hbm, wgb_ref, wub_ref, xbuf, obuf, zbuf,
                   cnt_ref, xsem, osem, zsem, *, tm, tf, n_tiles):
    j = pl.program_id(0)
    e = pl.program_id(1)

    @pl.when((j == 0) & (e == 0))
    def _():
        zbuf[...] = jnp.zeros_like(zbuf)

    @pl.when(off_ref[e + 1] > off_ref[e])
    def _():
        wgb_ref[...] = wg_ref[0].astype(BF16)
        wub_ref[...] = wu_ref[0].astype(BF16)

    def in_copy(t, slot):
        return pltpu.make_async_copy(xs_hbm.at[pl.ds(t * tm, tm), :], xbuf.at[slot], xsem.at[slot])

    def out_tile(t):
        return act_hbm.at[pl.ds(t * tm, tm), pl.ds(j * tf, tf)]

    def out_copy(t, slot):
        return pltpu.make_async_copy(obuf.at[slot], out_tile(t), osem.at[slot])

    def zero_copy(t):
        return pltpu.make_async_copy(zbuf, out_tile(t), zsem.at[0])

    def compute(in_slot, out_slot):
        x = xbuf[in_slot]
        gate = _bdot(x, wgb_ref[...])
        up = _bdot(x, wub_ref[...])
        obuf[out_slot] = (gate * jax.nn.sigmoid(gate) * up).astype(obuf.dtype)

    _tile_pipeline(off_ref, cnt_ref, pl.num_programs(0), n_tiles, in_copy, out_copy, zero_copy,
                   compute)


def _moe_up(xs, wg, wu, tile_off, tm):
    r, d = xs.shape
    f = wg.shape[2]
    tf = min(TF_MOE, f)
    kern = functools.partial(_moe_up_kernel, tm=tm, tf=tf, n_tiles=r // tm)
    return pl.pallas_call(
        kern,
        grid_spec=pltpu.PrefetchScalarGridSpec(
            num_scalar_prefetch=1,
            grid=(f // tf, N_EXPERTS),
            in_specs=[pl.BlockSpec(memory_space=pl.ANY),
                      pl.BlockSpec((1, d, tf), lambda j, e, off: (e, 0, j)),
                      pl.BlockSpec((1, d, tf), lambda j, e, off: (e, 0, j))],
            out_specs=pl.BlockSpec(memory_space=pl.ANY),
            scratch_shapes=[pltpu.VMEM((d, tf), BF16), pltpu.VMEM((d, tf), BF16),
                            pltpu.VMEM((IN_RING, tm, d), BF16),
                            pltpu.VMEM((OUT_RING, tm, tf), BF16),
                            pltpu.VMEM((tm, tf), BF16),
                            pltpu.SMEM((1,), jnp.int32),
                            pltpu.SemaphoreType.DMA((IN_RING,)),
                            pltpu.SemaphoreType.DMA((OUT_RING,)),
                            pltpu.SemaphoreType.DMA((1,))]),
        out_shape=jax.ShapeDtypeStruct((r, f), BF16),
        compiler_params=_cparams(("arbitrary", "arbitrary"), 56),
        name="moe_gate_up",
    )(tile_off, xs, wg, wu)


def _moe_down_kernel(off_ref, act_hbm, wd_ref, y_hbm, wdb_ref, abuf, obuf, zbuf, cnt_ref,
                     asem, osem, zsem, *, tm, tn, n_tiles):
    j = pl.program_id(0)
    e = pl.program_id(1)

    @pl.when((j == 0) & (e == 0))
    def _():
        zbuf[...] = jnp.zeros_like(zbuf)

    @pl.when(off_ref[e + 1] > off_ref[e])
    def _():
        wdb_ref[...] = wd_ref[0].astype(BF16)

    def in_copy(t, slot):
        return pltpu.make_async_copy(act_hbm.at[pl.ds(t * tm, tm), :], abuf.at[slot], asem.at[slot])

    def out_tile(t):
        return y_hbm.at[pl.ds(t * tm, tm), pl.ds(j * tn, tn)]

    def out_copy(t, slot):
        return pltpu.make_async_copy(obuf.at[slot], out_tile(t), osem.at[slot])

    def zero_copy(t):
        return pltpu.make_async_copy(zbuf, out_tile(t), zsem.at[0])

    def compute(in_slot, out_slot):
        obuf[out_slot] = _bdot(abuf[in_slot], wdb_ref[...])

    _tile_pipeline(off_ref, cnt_ref, pl.num_programs(0), n_tiles, in_copy, out_copy, zero_copy,
                   compute)


def _moe_down(act, wd, tile_off, tm):
    r, f = act.shape
    d = wd.shape[2]
    tn = min(TN_MOE, d)
    kern = functools.partial(_moe_down_kernel, tm=tm, tn=tn, n_tiles=r // tm)
    return pl.pallas_call(
        kern,
        grid_spec=pltpu.PrefetchScalarGridSpec(
            num_scalar_prefetch=1,
            grid=(d // tn, N_EXPERTS),
            in_specs=[pl.BlockSpec(memory_space=pl.ANY),
                      pl.BlockSpec((1, f, tn), lambda j, e, off: (e, 0, j))],
            out_specs=pl.BlockSpec(memory_space=pl.ANY),
            scratch_shapes=[pltpu.VMEM((f, tn), BF16),
                            pltpu.VMEM((IN_RING, tm, f), BF16),
                            pltpu.VMEM((OUT_RING, tm, tn), F32),
                            pltpu.VMEM((tm, tn), F32),
                            pltpu.SMEM((1,), jnp.int32),
                            pltpu.SemaphoreType.DMA((IN_RING,)),
                            pltpu.SemaphoreType.DMA((OUT_RING,)),
                            pltpu.SemaphoreType.DMA((1,))]),
        out_shape=jax.ShapeDtypeStruct((r, d), F32),
        compiler_params=_cparams(("arbitrary", "arbitrary"), 56),
        name="moe_down",
    )(tile_off, act, wd)


def _routing_tables(gates, sel, rank, counts, tm):
    ntok = gates.shape[0]
    n_tiles = (2 * ntok) // tm + N_EXPERTS
    cnt = counts[0, :N_EXPERTS].astype(jnp.int32)
    padded = ((cnt + tm - 1) // tm) * tm
    ends = jnp.cumsum(padded)
    offs = ends - padded
    selb = sel[:, :N_EXPERTS] > 0.5
    pos = offs[None, :] + rank[:, :N_EXPERTS].astype(jnp.int32)
    e_lo = jnp.argmax(selb, axis=1)
    e_hi = N_EXPERTS - 1 - jnp.argmax(selb[:, ::-1], axis=1)
    take = lambda a, e: jnp.take_along_axis(a, e[:, None], axis=1)[:, 0]
    pos0, pos1 = take(pos, e_lo), take(pos, e_hi)
    g8 = gates[:, :N_EXPERTS]
    g0, g1 = take(g8, e_lo), take(g8, e_hi)
    tok = jnp.arange(ntok, dtype=jnp.int32)
    pos_all = jnp.concatenate([pos0, pos1]).astype(jnp.int32)
    src = jnp.zeros((n_tiles * tm,), jnp.int32).at[pos_all].set(
        jnp.concatenate([tok, tok]), unique_indices=True)
    tile_off = jnp.concatenate([jnp.zeros((1,), jnp.int32), (ends // tm).astype(jnp.int32)])
    return src, pos_all, g0[:, None], g1[:, None], tile_off


def _moe_final(x, g, router_w, wg, wu, wd, final_norm):
    h, gates, sel, rank, counts = _router(x, g, router_w)
    tm = min(TM_MOE, x.shape[0])
    assert 2 * x.shape[0] // tm >= IN_RING, "the tile ring needs at least IN_RING used row tiles"
    src, pos_all, g0, g1, tile_off = _routing_tables(gates, sel, rank, counts, tm)
    xs = _group_rows(h, src)
    act = _moe_up(xs, wg, wu, tile_off, tm)
    y = _moe_down(act, wd, tile_off, tm)
    return _combine_final_norm(x, y, pos_all, g0, g1, final_norm)


def _forward(x, norm_mix_even, w_in_even, conv3_w, w_out_even, norm_ffn_even,
             ffn_w_gate, ffn_w_up, ffn_w_down, norm_mix_odd, cf_pw1_w, cf_pw1_b,
             cf_dw_w, cf_dw_b, cf_ln_g, cf_ln_b, cf_pw2_w, cf_pw2_b, norm_moe,
             router_w, moe_w_gate, moe_w_up, moe_w_down, final_norm):
    batch, seq, d = x.shape
    xt = x.reshape(batch * seq, d)
    ncol = SC_WIDTH // LANES
    q_lo, q_hi = 3 * SC_WIDTH, 3 * SC_WIDTH + SB_HEADS * SB_HEAD_DIM
    col = jnp.arange(w_in_even.shape[2])
    col_scale = jnp.where((col >= q_lo) & (col < q_hi), -(SB_HEAD_DIM ** -0.5) * LOG2E, 1.0)
    p = _norm_matmul(xt, norm_mix_even[0], w_in_even[0], col_scale.astype(F32), BF16)
    ysb = _stick_breaking(p, batch, seq, 3 * ncol, 3 * ncol + SB_HEADS, 3 * ncol + 2 * SB_HEADS)
    xt = _mix_out_proj(p, ysb, conv3_w[0], w_out_even[0], xt, seq)
    xt = _dense_swiglu(xt, norm_ffn_even[0], ffn_w_gate[0], ffn_w_up[0], ffn_w_down[0])
    u = _norm_glu(xt, norm_mix_odd[0], cf_pw1_w[0], cf_pw1_b[0])
    v = _dwconv_ln_silu(u, cf_dw_w[0], cf_dw_b[0], cf_ln_g[0], cf_ln_b[0], seq)
    xt = _matmul_bias_res(v, cf_pw2_w[0], cf_pw2_b[0], xt)
    out = _moe_final(xt, norm_moe[0], router_w[0], moe_w_gate[0], moe_w_up[0], moe_w_down[0],
                     final_norm)
    return out.reshape(batch, seq, d)


def kernel(x, norm_mix_even, w_in_even, conv3_w, w_out_even, norm_ffn_even, ffn_w_gate, ffn_w_up, ffn_w_down, norm_mix_odd, cf_pw1_w, cf_pw1_b, cf_dw_w, cf_dw_b, cf_ln_g, cf_ln_b, cf_pw2_w, cf_pw2_b, norm_moe, router_w, moe_w_gate, moe_w_up, moe_w_down, final_norm):
    return _forward(x, norm_mix_even, w_in_even, conv3_w, w_out_even, norm_ffn_even,
                    ffn_w_gate, ffn_w_up, ffn_w_down, norm_mix_odd, cf_pw1_w, cf_pw1_b,
                    cf_dw_w, cf_dw_b, cf_ln_g, cf_ln_b, cf_pw2_w, cf_pw2_b, norm_moe,
                    router_w, moe_w_gate, moe_w_up, moe_w_down, final_norm)
```

```python
import functools

import jax
import jax.numpy as jnp
from jax import lax
from jax.experimental import pallas as pl
from jax.experimental.pallas import tpu as pltpu

F32 = jnp.float32
BF16 = jnp.bfloat16

NORM_EPS = 1e-6
SC_WIDTH = 1024
SC_KERNEL = 3
SB_HEADS = 8
SB_HEAD_DIM = 128
CF_KERNEL = 31
N_EXPERTS = 8
LANES = 128
SUBLANES = 8
MIB = 1024 * 1024

TM_PROJ = 1024
TN_PROJ = 1024
TM_RESIDENT = 512
TN_GLU = 512
TM_FFN = 1024
TF_FFN = 256
TQ_ATTN = 512
ATTN_UNROLL = 4
LOG2E = 1.4426950408889634
TM_CONV = 256
CONV_HALO = 32
SC_HALO = 16
TM_ROUTE = 512
TM_MOE = 256
TF_MOE = 1024
TN_MOE = 512
IN_RING = 3
OUT_RING = 2
TG_GATHER = 256
TC_COMBINE = 128


def _cparams(semantics, vmem_mib):
    return pltpu.CompilerParams(dimension_semantics=semantics,
                                vmem_limit_bytes=vmem_mib * MIB)


def _rmsnorm_to(x_ref, g_ref, h_ref, chunk=128):
    g = g_ref[...]

    def body(c, carry):
        r = pl.multiple_of(c * chunk, chunk)
        x = x_ref[pl.ds(r, chunk), :]
        ms = jnp.mean(x * x, axis=-1, keepdims=True)
        h_ref[pl.ds(r, chunk), :] = (x * lax.rsqrt(ms + NORM_EPS) * g).astype(h_ref.dtype)
        return carry

    lax.fori_loop(0, x_ref.shape[0] // chunk, body, 0)


def _bdot(a, b):
    return jnp.dot(a, b, preferred_element_type=F32)


def _norm_mm_kernel(x_ref, g_ref, w_ref, cs_ref, o_ref, h_ref):
    @pl.when(pl.program_id(1) == 0)
    def _():
        _rmsnorm_to(x_ref, g_ref, h_ref)

    o_ref[...] = (_bdot(h_ref[...], w_ref[...].astype(BF16)) * cs_ref[...]).astype(o_ref.dtype)


def _norm_matmul(x, g, w, col_scale, out_dtype):
    m, k = x.shape
    n = w.shape[1]
    tm, tn = min(TM_PROJ, m), min(TN_PROJ, n)
    return pl.pallas_call(
        _norm_mm_kernel,
        grid=(m // tm, n // tn),
        in_specs=[pl.BlockSpec((tm, k), lambda i, j: (i, 0)),
                  pl.BlockSpec((1, k), lambda i, j: (0, 0)),
                  pl.BlockSpec((k, tn), lambda i, j: (0, j)),
                  pl.BlockSpec((1, tn), lambda i, j: (0, j))],
        out_specs=pl.BlockSpec((tm, tn), lambda i, j: (i, j)),
        out_shape=jax.ShapeDtypeStruct((m, n), out_dtype),
        scratch_shapes=[pltpu.VMEM((tm, k), BF16)],
        compiler_params=_cparams(("parallel", "arbitrary"), 56),
        name="norm_in_proj",
    )(x, g.reshape(1, k), w, col_scale.reshape(1, n))


def _norm_glu_kernel(x_ref, g_ref, wa_ref, wg_ref, ba_ref, bg_ref, o_ref, h_ref):
    @pl.when(pl.program_id(1) == 0)
    def _():
        _rmsnorm_to(x_ref, g_ref, h_ref)

    h = h_ref[...]
    a = _bdot(h, wa_ref[...].astype(BF16)) + ba_ref[...]
    gate = _bdot(h, wg_ref[...].astype(BF16)) + bg_ref[...]
    o_ref[...] = (a * jax.nn.sigmoid(gate)).astype(o_ref.dtype)


def _norm_glu(x, g, w, b):
    m, k = x.shape
    n = w.shape[1] // 2
    tm, tn = min(TM_PROJ, m), min(TN_GLU, n)
    nb = n // tn
    b2 = b.reshape(1, 2 * n)
    return pl.pallas_call(
        _norm_glu_kernel,
        grid=(m // tm, nb),
        in_specs=[pl.BlockSpec((tm, k), lambda i, j: (i, 0)),
                  pl.BlockSpec((1, k), lambda i, j: (0, 0)),
                  pl.BlockSpec((k, tn), lambda i, j: (0, j)),
                  pl.BlockSpec((k, tn), lambda i, j: (0, j + nb)),
                  pl.BlockSpec((1, tn), lambda i, j: (0, j)),
                  pl.BlockSpec((1, tn), lambda i, j: (0, j + nb))],
        out_specs=pl.BlockSpec((tm, tn), lambda i, j: (i, j)),
        out_shape=jax.ShapeDtypeStruct((m, n), F32),
        scratch_shapes=[pltpu.VMEM((tm, k), BF16)],
        compiler_params=_cparams(("parallel", "arbitrary"), 56),
        name="norm_pw1_glu",
    )(x, g.reshape(1, k), w, w, b2, b2)


def _mm_bias_res_kernel(a_ref, w_ref, b_ref, r_ref, o_ref, wb_ref):
    @pl.when(pl.program_id(0) == 0)
    def _():
        wb_ref[...] = w_ref[...].astype(BF16)

    o_ref[...] = r_ref[...] + _bdot(a_ref[...], wb_ref[...]) + b_ref[...]


def _matmul_bias_res(a, w, b, res):
    m, k = a.shape
    n = w.shape[1]
    tm = min(TM_RESIDENT, m)
    return pl.pallas_call(
        _mm_bias_res_kernel,
        grid=(m // tm,),
        in_specs=[pl.BlockSpec((tm, k), lambda i: (i, 0)),
                  pl.BlockSpec((k, n), lambda i: (0, 0), pipeline_mode=pl.Buffered(1)),
                  pl.BlockSpec((1, n), lambda i: (0, 0)),
                  pl.BlockSpec((tm, n), lambda i: (i, 0))],
        out_specs=pl.BlockSpec((tm, n), lambda i: (i, 0)),
        out_shape=jax.ShapeDtypeStruct((m, n), F32),
        scratch_shapes=[pltpu.VMEM((k, n), BF16)],
        compiler_params=_cparams(("arbitrary",), 56),
        name="pw2_residual",
    )(a, w, b.reshape(1, n), res)


def _attn_kernel(q_ref, k_ref, v_ref, o_ref, u_ref, c_ref, acc_ref, nz_ref, cm_ref, *, tq):
    hk = LANES
    tk = 2 * hk
    q0 = pl.program_id(2) * tq
    q = q_ref[...]
    c_ref[...] = jnp.zeros_like(c_ref)
    acc_ref[...] = jnp.zeros_like(acc_ref)

    jj = lax.broadcasted_iota(jnp.int32, (2 * hk, 2 * hk), 0) & (hk - 1)
    ss = lax.broadcasted_iota(jnp.int32, (2 * hk, 2 * hk), 1)
    u_ref[...] = jnp.where((ss >= hk) | (jj >= ss), 1.0, 0.0).astype(BF16)

    def half_sums(lk):
        hi = lk.astype(BF16)
        lo = (lk - hi.astype(F32)).astype(BF16)
        return _bdot(jnp.concatenate([hi, lo], axis=1), u_ref[...])

    def scores(kstart, slot, mask, r0=0):
        kb = k_ref[pl.ds(kstart, tk), :]
        nz = lax.dot_general(q[r0:], kb, (((1,), (1,)), ((), ())), preferred_element_type=F32)
        neg_abs = lax.bitcast_convert_type(
            lax.bitcast_convert_type(nz, jnp.uint32) | jnp.uint32(0x80000000), F32)
        log_keep = jnp.minimum(nz, 0.0) - jnp.log2(1.0 + jnp.exp2(neg_abs))
        if mask is not None:
            log_keep = jnp.where(mask[r0:], log_keep, 0.0)
        nz_ref[slot, r0:] = nz
        cm_ref[slot, 1, r0:] = half_sums(log_keep[:, hk:])
        cm_ref[slot, 0, r0:] = half_sums(log_keep[:, :hk])

    def weights(kstart, slot, mask, r0=0):
        c = c_ref[r0:]
        nz = nz_ref[slot, r0:]
        cm1 = cm_ref[slot, 1, r0:]
        cm0 = cm_ref[slot, 0, r0:]
        logit1 = (c + cm1[:, :hk]) - nz[:, hk:]
        c = c + cm1[:, hk:]
        logit0 = (c + cm0[:, :hk]) - nz[:, :hk]
        c_ref[r0:] = c + cm0[:, hk:]
        a = jnp.exp2(jnp.concatenate([logit0, logit1], axis=1))
        if mask is not None:
            a = jnp.where(mask[r0:], a, 0.0)
        acc_ref[r0:] += _bdot(a.astype(BF16), v_ref[pl.ds(kstart, tk), :])

    rows = lax.broadcasted_iota(jnp.int32, (tq, tk), 0)
    cols = lax.broadcasted_iota(jnp.int32, (tq, tk), 1)
    nd = tq // tk
    assert nd == 2 and ATTN_UNROLL % 2 == 0
    diag = [(pl.multiple_of(q0 + d * tk, tk), (cols + d * tk) < rows, d * tk)
            for d in reversed(range(nd))]
    nb = q0 // tk

    def below(j):
        return pl.multiple_of(q0 - (j + 1) * tk, tk)

    scores(diag[0][0], 0, diag[0][1], diag[0][2])
    for i in range(1, nd):
        scores(diag[i][0], i % 2, diag[i][1], diag[i][2])
        weights(diag[i - 1][0], (i - 1) % 2, diag[i - 1][1], diag[i - 1][2])
    last_k, last_mask, last_r0 = diag[nd - 1]

    @pl.when(nb == 0)
    def _():
        weights(last_k, (nd - 1) % 2, last_mask, last_r0)

    @pl.when(nb > 0)
    def _():
        scores(below(0), nd % 2, None)
        weights(last_k, (nd - 1) % 2, last_mask, last_r0)

        def steps(j0, count):
            for k in range(count):
                weights(below(j0 + k), (nd + k) % 2, None)
                scores(below(j0 + k + 1), (nd + k + 1) % 2, None)

        def body(i, carry):
            steps(ATTN_UNROLL * i, ATTN_UNROLL)
            return carry

        trips = (nb - 1) // ATTN_UNROLL
        lax.fori_loop(0, trips, body, 0)
        rest = nb - 1 - ATTN_UNROLL * trips
        for r in range(1, ATTN_UNROLL, 2):
            @pl.when(rest == r)
            def _():
                steps(nb - 1 - r, r)
                weights(below(nb - 1), (nd + r) % 2, None)

    o_ref[...] = acc_ref[...].astype(o_ref.dtype)


def _stick_breaking(p, batch, seq, col_q, col_k, col_v):
    tq = min(TQ_ATTN, seq)
    nq = seq // tq
    dh = SB_HEAD_DIM
    kern = functools.partial(_attn_kernel, tq=tq)
    return pl.pallas_call(
        kern,
        grid=(batch, SB_HEADS, nq),
        in_specs=[pl.BlockSpec((tq, dh), lambda b, h, i: (b * nq + i, col_q + h)),
                  pl.BlockSpec((seq, dh), lambda b, h, i: (b, col_k + h)),
                  pl.BlockSpec((seq, dh), lambda b, h, i: (b, col_v + h))],
        out_specs=pl.BlockSpec((tq, dh), lambda b, h, i: (b * nq + i, h)),
        out_shape=jax.ShapeDtypeStruct((batch * seq, SB_HEADS * dh), BF16),
        scratch_shapes=[pltpu.VMEM((2 * LANES, 2 * LANES), BF16),
                        pltpu.VMEM((tq, LANES), F32),
                        pltpu.VMEM((tq, dh), F32),
                        pltpu.VMEM((2, tq, 2 * LANES), F32),
                        pltpu.VMEM((2, 2, tq, 2 * LANES), F32)],
        compiler_params=_cparams(("parallel", "parallel", "arbitrary"), 32),
        name="stick_breaking_attention",
    )(p, p, p)


def _mix_out_kernel(pb_ref, pc_ref, ph_ref, hc_ref, hh_ref, cw_ref, ysb_ref, w_ref, x_ref,
                    o_ref, mix_ref, u_ref, wb_ref, *, tm, seq, chunk):
    i = pl.program_id(0)

    @pl.when(i == 0)
    def _():
        wb_ref[...] = w_ref[...].astype(BF16)

    halo = hc_ref[...].astype(F32) * hh_ref[...].astype(F32)
    u_ref[0:SC_HALO, :] = jnp.where((i * tm) % seq == 0, 0.0, halo)
    for r in range(0, tm, chunk):
        u_ref[SC_HALO + r:SC_HALO + r + chunk, :] = (
            pc_ref[r:r + chunk, :].astype(F32) * ph_ref[r:r + chunk, :].astype(F32))
    w0, w1, w2 = cw_ref[0:1, :], cw_ref[1:2, :], cw_ref[2:3, :]
    for r in range(0, tm, chunk):
        base = SC_HALO + r
        conv = (w2 * u_ref[base:base + chunk, :]
                + w1 * u_ref[base - 1:base - 1 + chunk, :]
                + w0 * u_ref[base - 2:base - 2 + chunk, :])
        ysc = pb_ref[r:r + chunk, :].astype(F32) * conv
        mix_ref[r:r + chunk, 0:SC_WIDTH] = ysc.astype(BF16)
    mix_ref[:, SC_WIDTH:] = ysb_ref[...]

    o_ref[...] = x_ref[...] + _bdot(mix_ref[...], wb_ref[...])


def _mix_out_proj(p, ysb, conv_w, w_out, x, seq):
    m = x.shape[0]
    d = w_out.shape[1]
    kdim = w_out.shape[0]
    tm = min(TM_RESIDENT, seq)
    hb = tm // SC_HALO
    kern = functools.partial(_mix_out_kernel, tm=tm, seq=seq, chunk=min(128, tm))
    return pl.pallas_call(
        kern,
        grid=(m // tm,),
        in_specs=[pl.BlockSpec((tm, SC_WIDTH), lambda i: (i, 0)),
                  pl.BlockSpec((tm, SC_WIDTH), lambda i: (i, 1)),
                  pl.BlockSpec((tm, SC_WIDTH), lambda i: (i, 2)),
                  pl.BlockSpec((SC_HALO, SC_WIDTH), lambda i: (jnp.maximum(i * hb - 1, 0), 1)),
                  pl.BlockSpec((SC_HALO, SC_WIDTH), lambda i: (jnp.maximum(i * hb - 1, 0), 2)),
                  pl.BlockSpec((SC_KERNEL, SC_WIDTH), lambda i: (0, 0)),
                  pl.BlockSpec((tm, SC_WIDTH), lambda i: (i, 0)),
                  pl.BlockSpec((kdim, d), lambda i: (0, 0), pipeline_mode=pl.Buffered(1)),
                  pl.BlockSpec((tm, d), lambda i: (i, 0))],
        out_specs=pl.BlockSpec((tm, d), lambda i: (i, 0)),
        out_shape=jax.ShapeDtypeStruct((m, d), F32),
        scratch_shapes=[pltpu.VMEM((tm, kdim), BF16),
                        pltpu.VMEM((tm + SC_HALO, SC_WIDTH), F32),
                        pltpu.VMEM((kdim, d), BF16)],
        compiler_params=_cparams(("arbitrary",), 56),
        name="shortconv_out_proj",
    )(p, p, p, p, p, conv_w, ysb, w_out, x)


def _ffn_kernel(x_ref, g_ref, wg_ref, wu_ref, wd_ref, o_ref, h_ref):
    @pl.when(pl.program_id(1) == 0)
    def _():
        _rmsnorm_to(x_ref, g_ref, h_ref)
        o_ref[...] = x_ref[...]

    h = h_ref[...]
    gate = _bdot(h, wg_ref[...].astype(BF16))
    up = _bdot(h, wu_ref[...].astype(BF16))
    act = (gate * jax.nn.sigmoid(gate) * up).astype(BF16)
    o_ref[...] += _bdot(act, wd_ref[...].astype(BF16))


def _dense_swiglu(x, g, wg, wu, wd):
    m, d = x.shape
    f = wg.shape[1]
    tm, tf = min(TM_FFN, m), min(TF_FFN, f)
    return pl.pallas_call(
        _ffn_kernel,
        grid=(m // tm, f // tf),
        in_specs=[pl.BlockSpec((tm, d), lambda i, j: (i, 0), pipeline_mode=pl.Buffered(1)),
                  pl.BlockSpec((1, d), lambda i, j: (0, 0)),
                  pl.BlockSpec((d, tf), lambda i, j: (0, j)),
                  pl.BlockSpec((d, tf), lambda i, j: (0, j)),
                  pl.BlockSpec((tf, d), lambda i, j: (j, 0))],
        out_specs=pl.BlockSpec((tm, d), lambda i, j: (i, 0)),
        out_shape=jax.ShapeDtypeStruct((m, d), F32),
        scratch_shapes=[pltpu.VMEM((tm, d), BF16)],
        compiler_params=_cparams(("parallel", "arbitrary"), 60),
        name="dense_swiglu",
    )(x, g.reshape(1, d), wg, wu, wd)


def _dwconv_ln_kernel(u_ref, halo_ref, w_ref, b_ref, lg_ref, lb_ref, o_ref, buf_ref, conv_ref,
                      *, tm, seq, rchunk):
    i = pl.program_id(0)
    buf_ref[0:CONV_HALO, :] = jnp.where((i * tm) % seq == 0, 0.0, halo_ref[...])
    buf_ref[CONV_HALO:, :] = u_ref[...]
    nlt = u_ref.shape[1] // LANES
    first = CONV_HALO - (CF_KERNEL - 1)

    def lane_tile(c, carry):
        lanes = pl.ds(pl.multiple_of(c * LANES, LANES), LANES)
        for r in range(0, tm, rchunk):
            acc = None
            for m in range(SUBLANES):
                rows = rchunk + (SUBLANES if m else 0)
                g = None
                for k in range(CF_KERNEL):
                    if (first + k) % SUBLANES != m:
                        continue
                    base = r + first + k - m
                    term = w_ref[k:k + 1, lanes] * buf_ref[base:base + rows, lanes]
                    g = term if g is None else g + term
                if m:
                    g = pltpu.roll(g, rows - m, axis=0)[:rchunk]
                acc = g if acc is None else acc + g
            conv_ref[r:r + rchunk, lanes] = acc
        return carry

    lax.fori_loop(0, nlt, lane_tile, 0)

    bias, lg, lb = b_ref[...], lg_ref[...], lb_ref[...]
    for r in range(0, tm, rchunk):
        y = conv_ref[r:r + rchunk, :] + bias
        mu = jnp.mean(y, axis=-1, keepdims=True)
        yc = y - mu
        var = jnp.mean(yc * yc, axis=-1, keepdims=True)
        t = yc * lax.rsqrt(var + NORM_EPS) * lg + lb
        o_ref[r:r + rchunk, :] = (t * jax.nn.sigmoid(t)).astype(o_ref.dtype)


def _dwconv_ln_silu(u, w, b, lg, lb, seq):
    m, c = u.shape
    tm = min(TM_CONV, seq)
    hb = tm // CONV_HALO
    kern = functools.partial(_dwconv_ln_kernel, tm=tm, seq=seq, rchunk=min(128, tm))
    return pl.pallas_call(
        kern,
        grid=(m // tm,),
        in_specs=[pl.BlockSpec((tm, c), lambda i: (i, 0)),
                  pl.BlockSpec((CONV_HALO, c), lambda i: (jnp.maximum(i * hb - 1, 0), 0)),
                  pl.BlockSpec((CF_KERNEL, c), lambda i: (0, 0)),
                  pl.BlockSpec((1, c), lambda i: (0, 0)),
                  pl.BlockSpec((1, c), lambda i: (0, 0)),
                  pl.BlockSpec((1, c), lambda i: (0, 0))],
        out_specs=pl.BlockSpec((tm, c), lambda i: (i, 0)),
        out_shape=jax.ShapeDtypeStruct((m, c), BF16),
        scratch_shapes=[pltpu.VMEM((tm + CONV_HALO, c), F32),
                        pltpu.VMEM((tm, c), F32)],
        compiler_params=_cparams(("parallel",), 32),
        name="dwconv_ln_silu",
    )(u, u, w, b.reshape(1, c), lg.reshape(1, c), lb.reshape(1, c))


def _router_kernel(x_ref, g_ref, rw_ref, h_ref, gates_ref, sel_ref, rank_ref, cnt_ref, carry_ref,
                   *, tm):
    @pl.when(pl.program_id(0) == 0)
    def _():
        carry_ref[...] = jnp.zeros_like(carry_ref)

    _rmsnorm_to(x_ref, g_ref, h_ref)
    logits = jnp.dot(h_ref[...], rw_ref[...], preferred_element_type=F32,
                     precision=lax.Precision.HIGHEST)
    lane = lax.broadcasted_iota(jnp.int32, logits.shape, 1).astype(F32)
    neg = jnp.float32(-jnp.inf)
    logits = jnp.where(lane < N_EXPERTS, logits, neg)
    m1 = jnp.max(logits, axis=-1, keepdims=True)
    i1 = jnp.min(jnp.where(logits == m1, lane, float(LANES)), axis=-1, keepdims=True)
    rest = jnp.where(lane == i1, neg, logits)
    m2 = jnp.max(rest, axis=-1, keepdims=True)
    i2 = jnp.min(jnp.where(rest == m2, lane, float(LANES)), axis=-1, keepdims=True)
    e2 = jnp.exp(m2 - m1)
    w1 = 1.0 / (1.0 + e2)
    w2 = e2 / (1.0 + e2)
    gates_ref[...] = jnp.where(lane == i1, w1, 0.0) + jnp.where(lane == i2, w2, 0.0)
    sel = jnp.where((lane == i1) | (lane == i2), 1.0, 0.0)
    sel_ref[...] = sel
    rr = lax.broadcasted_iota(jnp.int32, (tm, tm), 0)
    cc = lax.broadcasted_iota(jnp.int32, (tm, tm), 1)
    tri = jnp.where(cc < rr, 1.0, 0.0).astype(BF16)
    rank_ref[...] = _bdot(tri, sel.astype(BF16)) + carry_ref[...]
    carry_ref[...] += jnp.sum(sel, axis=0, keepdims=True)
    cnt_ref[...] = carry_ref[...]


def _router(x, g, router_w):
    m, d = x.shape
    tm = min(TM_ROUTE, m)
    rw = jnp.zeros((d, LANES), F32).at[:, :N_EXPERTS].set(router_w)
    kern = functools.partial(_router_kernel, tm=tm)
    wide = jax.ShapeDtypeStruct((m, LANES), F32)
    return pl.pallas_call(
        kern,
        grid=(m // tm,),
        in_specs=[pl.BlockSpec((tm, d), lambda i: (i, 0)),
                  pl.BlockSpec((1, d), lambda i: (0, 0)),
                  pl.BlockSpec((d, LANES), lambda i: (0, 0))],
        out_specs=[pl.BlockSpec((tm, d), lambda i: (i, 0)),
                   pl.BlockSpec((tm, LANES), lambda i: (i, 0)),
                   pl.BlockSpec((tm, LANES), lambda i: (i, 0)),
                   pl.BlockSpec((tm, LANES), lambda i: (i, 0)),
                   pl.BlockSpec((1, LANES), lambda i: (0, 0))],
        out_shape=[jax.ShapeDtypeStruct((m, d), F32), wide, wide, wide,
                   jax.ShapeDtypeStruct((1, LANES), F32)],
        scratch_shapes=[pltpu.VMEM((1, LANES), F32)],
        compiler_params=_cparams(("arbitrary",), 32),
        name="router_top2",
    )(x, g.reshape(1, d), rw)


def _start_row_gather(src_hbm, idx_ref, base, n, dst_ref, sem):
    def body(r, carry):
        t = idx_ref[base + r]
        pltpu.make_async_copy(src_hbm.at[pl.ds(t, 1), :], dst_ref.at[pl.ds(r, 1), :], sem).start()
        return carry

    lax.fori_loop(0, n, body, 0, unroll=8)


def _wait_row_gather(src_hbm, n, dst_ref, sem):
    def body(r, carry):
        pltpu.make_async_copy(src_hbm.at[pl.ds(0, 1), :], dst_ref.at[pl.ds(r, 1), :], sem).wait()
        return carry

    lax.fori_loop(0, n, body, 0, unroll=8)


def _group_rows_kernel(src_ref, h_hbm, o_ref, buf_ref, sem_ref, *, tg):
    i = pl.program_id(0)
    n = pl.num_programs(0)
    slot = i % 2

    @pl.when(i == 0)
    def _():
        _start_row_gather(h_hbm, src_ref, 0, tg, buf_ref.at[0], sem_ref.at[0])

    @pl.when(i + 1 < n)
    def _():
        _start_row_gather(h_hbm, src_ref, (i + 1) * tg, tg, buf_ref.at[1 - slot], sem_ref.at[1 - slot])

    _wait_row_gather(h_hbm, tg, buf_ref.at[slot], sem_ref.at[slot])
    o_ref[...] = buf_ref[slot].astype(o_ref.dtype)


def _group_rows(h, src):
    d = h.shape[1]
    r = src.shape[0]
    tg = min(TG_GATHER, r)
    kern = functools.partial(_group_rows_kernel, tg=tg)
    return pl.pallas_call(
        kern,
        grid_spec=pltpu.PrefetchScalarGridSpec(
            num_scalar_prefetch=1,
            grid=(r // tg,),
            in_specs=[pl.BlockSpec(memory_space=pl.ANY)],
            out_specs=pl.BlockSpec((tg, d), lambda i, s: (i, 0)),
            scratch_shapes=[pltpu.VMEM((2, tg, d), F32),
                            pltpu.SemaphoreType.DMA((2,))]),
        out_shape=jax.ShapeDtypeStruct((r, d), BF16),
        compiler_params=_cparams(("arbitrary",), 32),
        name="group_rows_by_expert",
    )(src, h)


def _combine_kernel(pos_ref, x_ref, g0_ref, g1_ref, fn_ref, y_hbm, o_ref, buf_ref, sem_ref,
                    *, tc, ntok):
    i = pl.program_id(0)
    n = pl.num_programs(0)
    slot = i % 2

    def start(step, s):
        _start_row_gather(y_hbm, pos_ref, step * tc, tc, buf_ref.at[s, 0], sem_ref.at[s, 0])
        _start_row_gather(y_hbm, pos_ref, ntok + step * tc, tc, buf_ref.at[s, 1], sem_ref.at[s, 1])

    @pl.when(i == 0)
    def _():
        start(0, 0)

    @pl.when(i + 1 < n)
    def _():
        start(i + 1, 1 - slot)

    _wait_row_gather(y_hbm, tc, buf_ref.at[slot, 0], sem_ref.at[slot, 0])
    _wait_row_gather(y_hbm, tc, buf_ref.at[slot, 1], sem_ref.at[slot, 1])
    x = x_ref[...] + g0_ref[...] * buf_ref[slot, 0] + g1_ref[...] * buf_ref[slot, 1]
    ms = jnp.mean(x * x, axis=-1, keepdims=True)
    o_ref[...] = x * lax.rsqrt(ms + NORM_EPS) * fn_ref[...]


def _combine_final_norm(x, y, pos, g0, g1, final_norm):
    m, d = x.shape
    tc = min(TC_COMBINE, m)
    kern = functools.partial(_combine_kernel, tc=tc, ntok=m)
    return pl.pallas_call(
        kern,
        grid_spec=pltpu.PrefetchScalarGridSpec(
            num_scalar_prefetch=1,
            grid=(m // tc,),
            in_specs=[pl.BlockSpec((tc, d), lambda i, s: (i, 0)),
                      pl.BlockSpec((tc, 1), lambda i, s: (i, 0)),
                      pl.BlockSpec((tc, 1), lambda i, s: (i, 0)),
                      pl.BlockSpec((1, d), lambda i, s: (0, 0)),
                      pl.BlockSpec(memory_space=pl.ANY)],
            out_specs=pl.BlockSpec((tc, d), lambda i, s: (i, 0)),
            scratch_shapes=[pltpu.VMEM((2, 2, tc, d), F32),
                            pltpu.SemaphoreType.DMA((2, 2))]),
        out_shape=jax.ShapeDtypeStruct((m, d), F32),
        compiler_params=_cparams(("arbitrary",), 32),
        name="moe_combine_final_norm",
    )(pos, x, g0, g1, final_norm.reshape(1, d), y)


def _tile_pipeline(off_ref, cnt_ref, n_col_tiles, n_tiles, in_copy, out_copy, zero_copy, compute):
    j = pl.program_id(0)
    e = pl.program_id(1)
    n_used = off_ref[N_EXPERTS]
    last_col = j == n_col_tiles - 1

    @pl.when((j == 0) & (e == 0))
    def _():
        cnt_ref[0] = 0
        for k in range(IN_RING):
            in_copy(k, k).start()

    def body(t, g):
        in_slot = g % IN_RING
        out_slot = g % OUT_RING
        in_copy(t, in_slot).wait()

        @pl.when(g >= OUT_RING)
        def _():
            out_copy(0, out_slot).wait()

        compute(in_slot, out_slot)
        out_copy(t, out_slot).start()
        ahead = t + IN_RING
        wraps = ahead >= n_used

        @pl.when(jnp.logical_not(wraps & last_col))
        def _():
            in_copy(jnp.where(wraps, ahead - n_used, ahead), in_slot).start()

        return g + 1

    g = lax.fori_loop(off_ref[e], off_ref[e + 1], body, cnt_ref[0])
    cnt_ref[0] = g

    @pl.when(e == N_EXPERTS - 1)
    def _():
        def tail(t, carry):
            zero_copy(t).start()
            zero_copy(t).wait()
            return carry

        lax.fori_loop(n_used, n_tiles, tail, 0)

        @pl.when(last_col)
        def _():
            for back in range(OUT_RING, 0, -1):
                @pl.when(g >= back)
                def _():
                    out_copy(0, (g - back) % OUT_RING).wait()


def _moe_up_kernel(off_ref, xs_hbm, wg_ref, wu_ref, act_hbm, wgb_ref, wub_ref, xbuf, obuf, zbuf,
                   cnt_ref, xsem, osem, zsem, *, tm, tf, n_tiles):
    j = pl.program_id(0)
    e = pl.program_id(1)

    @pl.when((j == 0) & (e == 0))
    def _():
        zbuf[...] = jnp.zeros_like(zbuf)

    @pl.when(off_ref[e + 1] > off_ref[e])
    def _():
        wgb_ref[...] = wg_ref[0].astype(BF16)
        wub_ref[...] = wu_ref[0].astype(BF16)

    def in_copy(t, slot):
        return pltpu.make_async_copy(xs_hbm.at[pl.ds(t * tm, tm), :], xbuf.at[slot], xsem.at[slot])

    def out_tile(t):
        return act_hbm.at[pl.ds(t * tm, tm), pl.ds(j * tf, tf)]

    def out_copy(t, slot):
        return pltpu.make_async_copy(obuf.at[slot], out_tile(t), osem.at[slot])

    def zero_copy(t):
        return pltpu.make_async_copy(zbuf, out_tile(t), zsem.at[0])

    def compute(in_slot, out_slot):
        x = xbuf[in_slot]
        gate = _bdot(x, wgb_ref[...])
        up = _bdot(x, wub_ref[...])
        obuf[out_slot] = (gate * jax.nn.sigmoid(gate) * up).astype(obuf.dtype)

    _tile_pipeline(off_ref, cnt_ref, pl.num_programs(0), n_tiles, in_copy, out_copy, zero_copy,
                   compute)


def _moe_up(xs, wg, wu, tile_off, tm):
    r, d = xs.shape
    f = wg.shape[2]
    tf = min(TF_MOE, f)
    kern = functools.partial(_moe_up_kernel, tm=tm, tf=tf, n_tiles=r // tm)
    return pl.pallas_call(
        kern,
        grid_spec=pltpu.PrefetchScalarGridSpec(
            num_scalar_prefetch=1,
            grid=(f // tf, N_EXPERTS),
            in_specs=[pl.BlockSpec(memory_space=pl.ANY),
                      pl.BlockSpec((1, d, tf), lambda j, e, off: (e, 0, j)),
                      pl.BlockSpec((1, d, tf), lambda j, e, off: (e, 0, j))],
            out_specs=pl.BlockSpec(memory_space=pl.ANY),
            scratch_shapes=[pltpu.VMEM((d, tf), BF16), pltpu.VMEM((d, tf), BF16),
                            pltpu.VMEM((IN_RING, tm, d), BF16),
                            pltpu.VMEM((OUT_RING, tm, tf), BF16),
                            pltpu.VMEM((tm, tf), BF16),
                            pltpu.SMEM((1,), jnp.int32),
                            pltpu.SemaphoreType.DMA((IN_RING,)),
                            pltpu.SemaphoreType.DMA((OUT_RING,)),
                            pltpu.SemaphoreType.DMA((1,))]),
        out_shape=jax.ShapeDtypeStruct((r, f), BF16),
        compiler_params=_cparams(("arbitrary", "arbitrary"), 56),
        name="moe_gate_up",
    )(tile_off, xs, wg, wu)


def _moe_down_kernel(off_ref, act_hbm, wd_ref, y_hbm, wdb_ref, abuf, obuf, zbuf, cnt_ref,
                     asem, osem, zsem, *, tm, tn, n_tiles):
    j = pl.program_id(0)
    e = pl.program_id(1)

    @pl.when((j == 0) & (e == 0))
    def _():
        zbuf[...] = jnp.zeros_like(zbuf)

    @pl.when(off_ref[e + 1] > off_ref[e])
    def _():
        wdb_ref[...] = wd_ref[0].astype(BF16)

    def in_copy(t, slot):
        return pltpu.make_async_copy(act_hbm.at[pl.ds(t * tm, tm), :], abuf.at[slot], asem.at[slot])

    def out_tile(t):
        return y_hbm.at[pl.ds(t * tm, tm), pl.ds(j * tn, tn)]

    def out_copy(t, slot):
        return pltpu.make_async_copy(obuf.at[slot], out_tile(t), osem.at[slot])

    def zero_copy(t):
        return pltpu.make_async_copy(zbuf, out_tile(t), zsem.at[0])

    def compute(in_slot, out_slot):
        obuf[out_slot] = _bdot(abuf[in_slot], wdb_ref[...])

    _tile_pipeline(off_ref, cnt_ref, pl.num_programs(0), n_tiles, in_copy, out_copy, zero_copy,
                   compute)


def _moe_down(act, wd, tile_off, tm):
    r, f = act.shape
    d = wd.shape[2]
    tn = min(TN_MOE, d)
    kern = functools.partial(_moe_down_kernel, tm=tm, tn=tn, n_tiles=r // tm)
    return pl.pallas_call(
        kern,
        grid_spec=pltpu.PrefetchScalarGridSpec(
            num_scalar_prefetch=1,
            grid=(d // tn, N_EXPERTS),
            in_specs=[pl.BlockSpec(memory_space=pl.ANY),
                      pl.BlockSpec((1, f, tn), lambda j, e, off: (e, 0, j))],
            out_specs=pl.BlockSpec(memory_space=pl.ANY),
            scratch_shapes=[pltpu.VMEM((f, tn), BF16),
                            pltpu.VMEM((IN_RING, tm, f), BF16),
                            pltpu.VMEM((OUT_RING, tm, tn), F32),
                            pltpu.VMEM((tm, tn), F32),
                            pltpu.SMEM((1,), jnp.int32),
                            pltpu.SemaphoreType.DMA((IN_RING,)),
                            pltpu.SemaphoreType.DMA((OUT_RING,)),
                            pltpu.SemaphoreType.DMA((1,))]),
        out_shape=jax.ShapeDtypeStruct((r, d), F32),
        compiler_params=_cparams(("arbitrary", "arbitrary"), 56),
        name="moe_down",
    )(tile_off, act, wd)


def _routing_tables(gates, sel, rank, counts, tm):
    ntok = gates.shape[0]
    n_tiles = (2 * ntok) // tm + N_EXPERTS
    cnt = counts[0, :N_EXPERTS].astype(jnp.int32)
    padded = ((cnt + tm - 1) // tm) * tm
    ends = jnp.cumsum(padded)
    offs = ends - padded
    selb = sel[:, :N_EXPERTS] > 0.5
    pos = offs[None, :] + rank[:, :N_EXPERTS].astype(jnp.int32)
    lane = jnp.arange(N_EXPERTS, dtype=jnp.int32)[None, :]
    lo = selb & (lane == jnp.min(jnp.where(selb, lane, N_EXPERTS), axis=1, keepdims=True))
    hi = selb & ~lo
    pick = lambda a, onehot: jnp.sum(jnp.where(onehot, a, 0), axis=1)
    pos0, pos1 = pick(pos, lo), pick(pos, hi)
    g8 = gates[:, :N_EXPERTS]
    g0, g1 = pick(g8, lo), pick(g8, hi)
    tok = jnp.arange(ntok, dtype=jnp.int32)
    pos_all = jnp.concatenate([pos0, pos1]).astype(jnp.int32)
    src = jnp.zeros((n_tiles * tm,), jnp.int32).at[pos_all].set(
        jnp.concatenate([tok, tok]), unique_indices=True)
    tile_off = jnp.concatenate([jnp.zeros((1,), jnp.int32), (ends // tm).astype(jnp.int32)])
    return src, pos_all, g0[:, None], g1[:, None], tile_off


def _moe_final(x, g, router_w, wg, wu, wd, final_norm):
    h, gates, sel, rank, counts = _router(x, g, router_w)
    tm = min(TM_MOE, x.shape[0])
    assert 2 * x.shape[0] // tm >= IN_RING, "the tile ring needs at least IN_RING used row tiles"
    src, pos_all, g0, g1, tile_off = _routing_tables(gates, sel, rank, counts, tm)
    xs = _group_rows(h, src)
    act = _moe_up(xs, wg, wu, tile_off, tm)
    y = _moe_down(act, wd, tile_off, tm)
    return _combine_final_norm(x, y, pos_all, g0, g1, final_norm)


def _forward(x, norm_mix_even, w_in_even, conv3_w, w_out_even, norm_ffn_even,
             ffn_w_gate, ffn_w_up, ffn_w_down, norm_mix_odd, cf_pw1_w, cf_pw1_b,
             cf_dw_w, cf_dw_b, cf_ln_g, cf_ln_b, cf_pw2_w, cf_pw2_b, norm_moe,
             router_w, moe_w_gate, moe_w_up, moe_w_down, final_norm):
    batch, seq, d = x.shape
    xt = x.reshape(batch * seq, d)
    ncol = SC_WIDTH // LANES
    q_lo, q_hi = 3 * SC_WIDTH, 3 * SC_WIDTH + SB_HEADS * SB_HEAD_DIM
    col = jnp.arange(w_in_even.shape[2])
    col_scale = jnp.where((col >= q_lo) & (col < q_hi), -(SB_HEAD_DIM ** -0.5) * LOG2E, 1.0)
    p = _norm_matmul(xt, norm_mix_even[0], w_in_even[0], col_scale.astype(F32), BF16)
    ysb = _stick_breaking(p, batch, seq, 3 * ncol, 3 * ncol + SB_HEADS, 3 * ncol + 2 * SB_HEADS)
    xt = _mix_out_proj(p, ysb, conv3_w[0], w_out_even[0], xt, seq)
    xt = _dense_swiglu(xt, norm_ffn_even[0], ffn_w_gate[0], ffn_w_up[0], ffn_w_down[0])
    u = _norm_glu(xt, norm_mix_odd[0], cf_pw1_w[0], cf_pw1_b[0])
    v = _dwconv_ln_silu(u, cf_dw_w[0], cf_dw_b[0], cf_ln_g[0], cf_ln_b[0], seq)
    xt = _matmul_bias_res(v, cf_pw2_w[0], cf_pw2_b[0], xt)
    out = _moe_final(xt, norm_moe[0], router_w[0], moe_w_gate[0], moe_w_up[0], moe_w_down[0],
                     final_norm)
    return out.reshape(batch, seq, d)


def kernel(x, norm_mix_even, w_in_even, conv3_w, w_out_even, norm_ffn_even, ffn_w_gate, ffn_w_up, ffn_w_down, norm_mix_odd, cf_pw1_w, cf_pw1_b, cf_dw_w, cf_dw_b, cf_ln_g, cf_ln_b, cf_pw2_w, cf_pw2_b, norm_moe, router_w, moe_w_gate, moe_w_up, moe_w_down, final_norm):
    return _forward(x, norm_mix_even, w_in_even, conv3_w, w_out_even, norm_ffn_even,
                    ffn_w_gate, ffn_w_up, ffn_w_down, norm_mix_odd, cf_pw1_w, cf_pw1_b,
                    cf_dw_w, cf_dw_b, cf_ln_g, cf_ln_b, cf_pw2_w, cf_pw2_b, norm_moe,
                    router_w, moe_w_gate, moe_w_up, moe_w_down, final_norm)
```

```python
import functools

import jax
import jax.numpy as jnp
from jax import lax
from jax.experimental import pallas as pl
from jax.experimental.pallas import tpu as pltpu

F32 = jnp.float32
BF16 = jnp.bfloat16

NORM_EPS = 1e-6
SC_WIDTH = 1024
SC_KERNEL = 3
SB_HEADS = 8
SB_HEAD_DIM = 128
CF_KERNEL = 31
N_EXPERTS = 8
LANES = 128
SUBLANES = 8
MIB = 1024 * 1024

TM_PROJ = 1024
TN_PROJ = 1024
TM_RESIDENT = 512
TN_GLU = 512
TM_FFN = 1024
TF_FFN = 256
TQ_ATTN = 512
ATTN_UNROLL = 6
LOG2E = 1.4426950408889634
TM_CONV = 256
CONV_HALO = 32
SC_HALO = 16
TM_ROUTE = 512
TM_MOE = 256
TF_MOE = 1024
TN_MOE = 512
IN_RING = 3
OUT_RING = 2
TG_GATHER = 256
TC_COMBINE = 128


def _cparams(semantics, vmem_mib):
    return pltpu.CompilerParams(dimension_semantics=semantics,
                                vmem_limit_bytes=vmem_mib * MIB)


def _rmsnorm_to(x_ref, g_ref, h_ref, chunk=128):
    g = g_ref[...]

    def body(c, carry):
        r = pl.multiple_of(c * chunk, chunk)
        x = x_ref[pl.ds(r, chunk), :]
        ms = jnp.mean(x * x, axis=-1, keepdims=True)
        h_ref[pl.ds(r, chunk), :] = (x * lax.rsqrt(ms + NORM_EPS) * g).astype(h_ref.dtype)
        return carry

    lax.fori_loop(0, x_ref.shape[0] // chunk, body, 0)


def _bdot(a, b):
    return jnp.dot(a, b, preferred_element_type=F32)


def _norm_mm_kernel(x_ref, g_ref, w_ref, cs_ref, o_ref, h_ref):
    @pl.when(pl.program_id(1) == 0)
    def _():
        _rmsnorm_to(x_ref, g_ref, h_ref)

    o_ref[...] = (_bdot(h_ref[...], w_ref[...].astype(BF16)) * cs_ref[...]).astype(o_ref.dtype)


def _norm_matmul(x, g, w, col_scale, out_dtype):
    m, k = x.shape
    n = w.shape[1]
    tm, tn = min(TM_PROJ, m), min(TN_PROJ, n)
    return pl.pallas_call(
        _norm_mm_kernel,
        grid=(m // tm, n // tn),
        in_specs=[pl.BlockSpec((tm, k), lambda i, j: (i, 0)),
                  pl.BlockSpec((1, k), lambda i, j: (0, 0)),
                  pl.BlockSpec((k, tn), lambda i, j: (0, j)),
                  pl.BlockSpec((1, tn), lambda i, j: (0, j))],
        out_specs=pl.BlockSpec((tm, tn), lambda i, j: (i, j)),
        out_shape=jax.ShapeDtypeStruct((m, n), out_dtype),
        scratch_shapes=[pltpu.VMEM((tm, k), BF16)],
        compiler_params=_cparams(("parallel", "arbitrary"), 56),
        name="norm_in_proj",
    )(x, g.reshape(1, k), w, col_scale.reshape(1, n))


def _norm_glu_kernel(x_ref, g_ref, wa_ref, wg_ref, ba_ref, bg_ref, o_ref, h_ref):
    @pl.when(pl.program_id(1) == 0)
    def _():
        _rmsnorm_to(x_ref, g_ref, h_ref)

    h = h_ref[...]
    a = _bdot(h, wa_ref[...].astype(BF16)) + ba_ref[...]
    gate = _bdot(h, wg_ref[...].astype(BF16)) + bg_ref[...]
    o_ref[...] = (a * jax.nn.sigmoid(gate)).astype(o_ref.dtype)


def _norm_glu(x, g, w, b):
    m, k = x.shape
    n = w.shape[1] // 2
    tm, tn = min(TM_PROJ, m), min(TN_GLU, n)
    nb = n // tn
    b2 = b.reshape(1, 2 * n)
    return pl.pallas_call(
        _norm_glu_kernel,
        grid=(m // tm, nb),
        in_specs=[pl.BlockSpec((tm, k), lambda i, j: (i, 0)),
                  pl.BlockSpec((1, k), lambda i, j: (0, 0)),
                  pl.BlockSpec((k, tn), lambda i, j: (0, j)),
                  pl.BlockSpec((k, tn), lambda i, j: (0, j + nb)),
                  pl.BlockSpec((1, tn), lambda i, j: (0, j)),
                  pl.BlockSpec((1, tn), lambda i, j: (0, j + nb))],
        out_specs=pl.BlockSpec((tm, tn), lambda i, j: (i, j)),
        out_shape=jax.ShapeDtypeStruct((m, n), F32),
        scratch_shapes=[pltpu.VMEM((tm, k), BF16)],
        compiler_params=_cparams(("parallel", "arbitrary"), 56),
        name="norm_pw1_glu",
    )(x, g.reshape(1, k), w, w, b2, b2)


def _mm_bias_res_kernel(a_ref, w_ref, b_ref, r_ref, o_ref, wb_ref):
    @pl.when(pl.program_id(0) == 0)
    def _():
        wb_ref[...] = w_ref[...].astype(BF16)

    o_ref[...] = r_ref[...] + _bdot(a_ref[...], wb_ref[...]) + b_ref[...]


def _matmul_bias_res(a, w, b, res):
    m, k = a.shape
    n = w.shape[1]
    tm = min(TM_RESIDENT, m)
    return pl.pallas_call(
        _mm_bias_res_kernel,
        grid=(m // tm,),
        in_specs=[pl.BlockSpec((tm, k), lambda i: (i, 0)),
                  pl.BlockSpec((k, n), lambda i: (0, 0), pipeline_mode=pl.Buffered(1)),
                  pl.BlockSpec((1, n), lambda i: (0, 0)),
                  pl.BlockSpec((tm, n), lambda i: (i, 0))],
        out_specs=pl.BlockSpec((tm, n), lambda i: (i, 0)),
        out_shape=jax.ShapeDtypeStruct((m, n), F32),
        scratch_shapes=[pltpu.VMEM((k, n), BF16)],
        compiler_params=_cparams(("arbitrary",), 56),
        name="pw2_residual",
    )(a, w, b.reshape(1, n), res)


def _attn_kernel(q_ref, k_ref, v_ref, o_ref, u_ref, c_ref, acc_ref, nz_ref, cm_ref, *, tq):
    hk = LANES
    tk = 2 * hk
    q0 = pl.program_id(2) * tq
    q = q_ref[...]
    c_ref[...] = jnp.zeros_like(c_ref)
    acc_ref[...] = jnp.zeros_like(acc_ref)

    jj = lax.broadcasted_iota(jnp.int32, (2 * hk, 2 * hk), 0) & (hk - 1)
    ss = lax.broadcasted_iota(jnp.int32, (2 * hk, 2 * hk), 1)
    u_ref[...] = jnp.where((ss >= hk) | (jj >= ss), 1.0, 0.0).astype(BF16)

    def half_sums(lk):
        hi = lk.astype(BF16)
        lo = (lk - hi.astype(F32)).astype(BF16)
        return _bdot(jnp.concatenate([hi, lo], axis=1), u_ref[...])

    def scores(kstart, slot, mask, r0=0):
        kb = k_ref[pl.ds(kstart, tk), :]
        nz = lax.dot_general(q[r0:], kb, (((1,), (1,)), ((), ())), preferred_element_type=F32)
        neg_abs = lax.bitcast_convert_type(
            lax.bitcast_convert_type(nz, jnp.uint32) | jnp.uint32(0x80000000), F32)
        log_keep = jnp.minimum(nz, 0.0) - jnp.log2(1.0 + jnp.exp2(neg_abs))
        if mask is not None:
            log_keep = jnp.where(mask[r0:], log_keep, 0.0)
        nz_ref[slot, r0:] = nz
        cm_ref[slot, 1, r0:] = half_sums(log_keep[:, hk:])
        cm_ref[slot, 0, r0:] = half_sums(log_keep[:, :hk])

    def weights(kstart, slot, mask, r0=0):
        c = c_ref[r0:]
        nz = nz_ref[slot, r0:]
        cm1 = cm_ref[slot, 1, r0:]
        cm0 = cm_ref[slot, 0, r0:]
        logit1 = (c + cm1[:, :hk]) - nz[:, hk:]
        c = c + cm1[:, hk:]
        logit0 = (c + cm0[:, :hk]) - nz[:, :hk]
        c_ref[r0:] = c + cm0[:, hk:]
        a = jnp.exp2(jnp.concatenate([logit0, logit1], axis=1))
        if mask is not None:
            a = jnp.where(mask[r0:], a, 0.0)
        acc_ref[r0:] += _bdot(a.astype(BF16), v_ref[pl.ds(kstart, tk), :])

    rows = lax.broadcasted_iota(jnp.int32, (tq, tk), 0)
    cols = lax.broadcasted_iota(jnp.int32, (tq, tk), 1)
    nd = tq // tk
    assert nd == 2 and ATTN_UNROLL % 2 == 0
    diag = [(pl.multiple_of(q0 + d * tk, tk), (cols + d * tk) < rows, d * tk)
            for d in reversed(range(nd))]
    nb = q0 // tk

    def below(j):
        return pl.multiple_of(q0 - (j + 1) * tk, tk)

    scores(diag[0][0], 0, diag[0][1], diag[0][2])
    for i in range(1, nd):
        scores(diag[i][0], i % 2, diag[i][1], diag[i][2])
        weights(diag[i - 1][0], (i - 1) % 2, diag[i - 1][1], diag[i - 1][2])
    last_k, last_mask, last_r0 = diag[nd - 1]

    @pl.when(nb == 0)
    def _():
        weights(last_k, (nd - 1) % 2, last_mask, last_r0)

    @pl.when(nb > 0)
    def _():
        scores(below(0), nd % 2, None)
        weights(last_k, (nd - 1) % 2, last_mask, last_r0)

        def steps(j0, count):
            for k in range(count):
                weights(below(j0 + k), (nd + k) % 2, None)
                scores(below(j0 + k + 1), (nd + k + 1) % 2, None)

        def body(i, carry):
            steps(ATTN_UNROLL * i, ATTN_UNROLL)
            return carry

        trips = (nb - 1) // ATTN_UNROLL
        lax.fori_loop(0, trips, body, 0)
        rest = nb - 1 - ATTN_UNROLL * trips
        for r in range(1, ATTN_UNROLL, 2):
            @pl.when(rest == r)
            def _():
                steps(nb - 1 - r, r)
                weights(below(nb - 1), (nd + r) % 2, None)

    o_ref[...] = acc_ref[...].astype(o_ref.dtype)


def _stick_breaking(p, batch, seq, col_q, col_k, col_v):
    tq = min(TQ_ATTN, seq)
    nq = seq // tq
    dh = SB_HEAD_DIM
    kern = functools.partial(_attn_kernel, tq=tq)
    return pl.pallas_call(
        kern,
        grid=(batch, SB_HEADS, nq),
        in_specs=[pl.BlockSpec((tq, dh), lambda b, h, i: (b * nq + i, col_q + h)),
                  pl.BlockSpec((seq, dh), lambda b, h, i: (b, col_k + h)),
                  pl.BlockSpec((seq, dh), lambda b, h, i: (b, col_v + h))],
        out_specs=pl.BlockSpec((tq, dh), lambda b, h, i: (b * nq + i, h)),
        out_shape=jax.ShapeDtypeStruct((batch * seq, SB_HEADS * dh), BF16),
        scratch_shapes=[pltpu.VMEM((2 * LANES, 2 * LANES), BF16),
                        pltpu.VMEM((tq, LANES), F32),
                        pltpu.VMEM((tq, dh), F32),
                        pltpu.VMEM((2, tq, 2 * LANES), F32),
                        pltpu.VMEM((2, 2, tq, 2 * LANES), F32)],
        compiler_params=_cparams(("parallel", "parallel", "arbitrary"), 32),
        name="stick_breaking_attention",
    )(p, p, p)


def _mix_out_kernel(pb_ref, pc_ref, ph_ref, hc_ref, hh_ref, cw_ref, ysb_ref, w_ref, x_ref,
                    o_ref, mix_ref, u_ref, wb_ref, *, tm, seq, chunk):
    i = pl.program_id(0)

    @pl.when(i == 0)
    def _():
        wb_ref[...] = w_ref[...].astype(BF16)

    halo = hc_ref[...].astype(F32) * hh_ref[...].astype(F32)
    u_ref[0:SC_HALO, :] = jnp.where((i * tm) % seq == 0, 0.0, halo)
    for r in range(0, tm, chunk):
        u_ref[SC_HALO + r:SC_HALO + r + chunk, :] = (
            pc_ref[r:r + chunk, :].astype(F32) * ph_ref[r:r + chunk, :].astype(F32))
    w0, w1, w2 = cw_ref[0:1, :], cw_ref[1:2, :], cw_ref[2:3, :]
    for r in range(0, tm, chunk):
        base = SC_HALO + r
        conv = (w2 * u_ref[base:base + chunk, :]
                + w1 * u_ref[base - 1:base - 1 + chunk, :]
                + w0 * u_ref[base - 2:base - 2 + chunk, :])
        ysc = pb_ref[r:r + chunk, :].astype(F32) * conv
        mix_ref[r:r + chunk, 0:SC_WIDTH] = ysc.astype(BF16)
    mix_ref[:, SC_WIDTH:] = ysb_ref[...]

    o_ref[...] = x_ref[...] + _bdot(mix_ref[...], wb_ref[...])


def _mix_out_proj(p, ysb, conv_w, w_out, x, seq):
    m = x.shape[0]
    d = w_out.shape[1]
    kdim = w_out.shape[0]
    tm = min(TM_RESIDENT, seq)
    hb = tm // SC_HALO
    kern = functools.partial(_mix_out_kernel, tm=tm, seq=seq, chunk=min(128, tm))
    return pl.pallas_call(
        kern,
        grid=(m // tm,),
        in_specs=[pl.BlockSpec((tm, SC_WIDTH), lambda i: (i, 0)),
                  pl.BlockSpec((tm, SC_WIDTH), lambda i: (i, 1)),
                  pl.BlockSpec((tm, SC_WIDTH), lambda i: (i, 2)),
                  pl.BlockSpec((SC_HALO, SC_WIDTH), lambda i: (jnp.maximum(i * hb - 1, 0), 1)),
                  pl.BlockSpec((SC_HALO, SC_WIDTH), lambda i: (jnp.maximum(i * hb - 1, 0), 2)),
                  pl.BlockSpec((SC_KERNEL, SC_WIDTH), lambda i: (0, 0)),
                  pl.BlockSpec((tm, SC_WIDTH), lambda i: (i, 0)),
                  pl.BlockSpec((kdim, d), lambda i: (0, 0), pipeline_mode=pl.Buffered(1)),
                  pl.BlockSpec((tm, d), lambda i: (i, 0))],
        out_specs=pl.BlockSpec((tm, d), lambda i: (i, 0)),
        out_shape=jax.ShapeDtypeStruct((m, d), F32),
        scratch_shapes=[pltpu.VMEM((tm, kdim), BF16),
                        pltpu.VMEM((tm + SC_HALO, SC_WIDTH), F32),
                        pltpu.VMEM((kdim, d), BF16)],
        compiler_params=_cparams(("arbitrary",), 56),
        name="shortconv_out_proj",
    )(p, p, p, p, p, conv_w, ysb, w_out, x)


def _ffn_kernel(x_ref, g_ref, wg_ref, wu_ref, wd_ref, o_ref, h_ref):
    @pl.when(pl.program_id(1) == 0)
    def _():
        _rmsnorm_to(x_ref, g_ref, h_ref)
        o_ref[...] = x_ref[...]

    h = h_ref[...]
    gate = _bdot(h, wg_ref[...].astype(BF16))
    up = _bdot(h, wu_ref[...].astype(BF16))
    act = (gate * jax.nn.sigmoid(gate) * up).astype(BF16)
    o_ref[...] += _bdot(act, wd_ref[...].astype(BF16))


def _dense_swiglu(x, g, wg, wu, wd):
    m, d = x.shape
    f = wg.shape[1]
    tm, tf = min(TM_FFN, m), min(TF_FFN, f)
    return pl.pallas_call(
        _ffn_kernel,
        grid=(m // tm, f // tf),
        in_specs=[pl.BlockSpec((tm, d), lambda i, j: (i, 0), pipeline_mode=pl.Buffered(1)),
                  pl.BlockSpec((1, d), lambda i, j: (0, 0)),
                  pl.BlockSpec((d, tf), lambda i, j: (0, j)),
                  pl.BlockSpec((d, tf), lambda i, j: (0, j)),
                  pl.BlockSpec((tf, d), lambda i, j: (j, 0))],
        out_specs=pl.BlockSpec((tm, d), lambda i, j: (i, 0)),
        out_shape=jax.ShapeDtypeStruct((m, d), F32),
        scratch_shapes=[pltpu.VMEM((tm, d), BF16)],
        compiler_params=_cparams(("parallel", "arbitrary"), 60),
        name="dense_swiglu",
    )(x, g.reshape(1, d), wg, wu, wd)


def _dwconv_ln_kernel(u_ref, halo_ref, w_ref, b_ref, lg_ref, lb_ref, o_ref, buf_ref, conv_ref,
                      *, tm, seq, rchunk):
    i = pl.program_id(0)
    buf_ref[0:CONV_HALO, :] = jnp.where((i * tm) % seq == 0, 0.0, halo_ref[...])
    buf_ref[CONV_HALO:, :] = u_ref[...]
    nlt = u_ref.shape[1] // LANES
    first = CONV_HALO - (CF_KERNEL - 1)

    def lane_tile(c, carry):
        lanes = pl.ds(pl.multiple_of(c * LANES, LANES), LANES)
        for r in range(0, tm, rchunk):
            acc = None
            for m in range(SUBLANES):
                rows = rchunk + (SUBLANES if m else 0)
                g = None
                for k in range(CF_KERNEL):
                    if (first + k) % SUBLANES != m:
                        continue
                    base = r + first + k - m
                    term = w_ref[k:k + 1, lanes] * buf_ref[base:base + rows, lanes]
                    g = term if g is None else g + term
                if m:
                    g = pltpu.roll(g, rows - m, axis=0)[:rchunk]
                acc = g if acc is None else acc + g
            conv_ref[r:r + rchunk, lanes] = acc
        return carry

    lax.fori_loop(0, nlt, lane_tile, 0)

    bias, lg, lb = b_ref[...], lg_ref[...], lb_ref[...]
    for r in range(0, tm, rchunk):
        y = conv_ref[r:r + rchunk, :] + bias
        mu = jnp.mean(y, axis=-1, keepdims=True)
        yc = y - mu
        var = jnp.mean(yc * yc, axis=-1, keepdims=True)
        t = yc * lax.rsqrt(var + NORM_EPS) * lg + lb
        o_ref[r:r + rchunk, :] = (t * jax.nn.sigmoid(t)).astype(o_ref.dtype)


def _dwconv_ln_silu(u, w, b, lg, lb, seq):
    m, c = u.shape
    tm = min(TM_CONV, seq)
    hb = tm // CONV_HALO
    kern = functools.partial(_dwconv_ln_kernel, tm=tm, seq=seq, rchunk=min(128, tm))
    return pl.pallas_call(
        kern,
        grid=(m // tm,),
        in_specs=[pl.BlockSpec((tm, c), lambda i: (i, 0)),
                  pl.BlockSpec((CONV_HALO, c), lambda i: (jnp.maximum(i * hb - 1, 0), 0)),
                  pl.BlockSpec((CF_KERNEL, c), lambda i: (0, 0)),
                  pl.BlockSpec((1, c), lambda i: (0, 0)),
                  pl.BlockSpec((1, c), lambda i: (0, 0)),
                  pl.BlockSpec((1, c), lambda i: (0, 0))],
        out_specs=pl.BlockSpec((tm, c), lambda i: (i, 0)),
        out_shape=jax.ShapeDtypeStruct((m, c), BF16),
        scratch_shapes=[pltpu.VMEM((tm + CONV_HALO, c), F32),
                        pltpu.VMEM((tm, c), F32)],
        compiler_params=_cparams(("parallel",), 32),
        name="dwconv_ln_silu",
    )(u, u, w, b.reshape(1, c), lg.reshape(1, c), lb.reshape(1, c))


def _router_kernel(x_ref, g_ref, rw_ref, h_ref, gates_ref, sel_ref, rank_ref, cnt_ref, carry_ref,
                   *, tm):
    @pl.when(pl.program_id(0) == 0)
    def _():
        carry_ref[...] = jnp.zeros_like(carry_ref)

    _rmsnorm_to(x_ref, g_ref, h_ref)
    logits = jnp.dot(h_ref[...], rw_ref[...], preferred_element_type=F32,
                     precision=lax.Precision.HIGHEST)
    lane = lax.broadcasted_iota(jnp.int32, logits.shape, 1).astype(F32)
    neg = jnp.float32(-jnp.inf)
    logits = jnp.where(lane < N_EXPERTS, logits, neg)
    m1 = jnp.max(logits, axis=-1, keepdims=True)
    i1 = jnp.min(jnp.where(logits == m1, lane, float(LANES)), axis=-1, keepdims=True)
    rest = jnp.where(lane == i1, neg, logits)
    m2 = jnp.max(rest, axis=-1, keepdims=True)
    i2 = jnp.min(jnp.where(rest == m2, lane, float(LANES)), axis=-1, keepdims=True)
    e2 = jnp.exp(m2 - m1)
    w1 = 1.0 / (1.0 + e2)
    w2 = e2 / (1.0 + e2)
    gates_ref[...] = jnp.where(lane == i1, w1, 0.0) + jnp.where(lane == i2, w2, 0.0)
    sel = jnp.where((lane == i1) | (lane == i2), 1.0, 0.0)
    sel_ref[...] = sel
    rr = lax.broadcasted_iota(jnp.int32, (tm, tm), 0)
    cc = lax.broadcasted_iota(jnp.int32, (tm, tm), 1)
    tri = jnp.where(cc < rr, 1.0, 0.0).astype(BF16)
    rank_ref[...] = _bdot(tri, sel.astype(BF16)) + carry_ref[...]
    carry_ref[...] += jnp.sum(sel, axis=0, keepdims=True)
    cnt_ref[...] = carry_ref[...]


def _router(x, g, router_w):
    m, d = x.shape
    tm = min(TM_ROUTE, m)
    rw = jnp.zeros((d, LANES), F32).at[:, :N_EXPERTS].set(router_w)
    kern = functools.partial(_router_kernel, tm=tm)
    wide = jax.ShapeDtypeStruct((m, LANES), F32)
    return pl.pallas_call(
        kern,
        grid=(m // tm,),
        in_specs=[pl.BlockSpec((tm, d), lambda i: (i, 0)),
                  pl.BlockSpec((1, d), lambda i: (0, 0)),
                  pl.BlockSpec((d, LANES), lambda i: (0, 0))],
        out_specs=[pl.BlockSpec((tm, d), lambda i: (i, 0)),
                   pl.BlockSpec((tm, LANES), lambda i: (i, 0)),
                   pl.BlockSpec((tm, LANES), lambda i: (i, 0)),
                   pl.BlockSpec((tm, LANES), lambda i: (i, 0)),
                   pl.BlockSpec((1, LANES), lambda i: (0, 0))],
        out_shape=[jax.ShapeDtypeStruct((m, d), F32), wide, wide, wide,
                   jax.ShapeDtypeStruct((1, LANES), F32)],
        scratch_shapes=[pltpu.VMEM((1, LANES), F32)],
        compiler_params=_cparams(("arbitrary",), 32),
        name="router_top2",
    )(x, g.reshape(1, d), rw)


def _start_row_gather(src_hbm, idx_ref, base, n, dst_ref, sem):
    def body(r, carry):
        t = idx_ref[base + r]
        pltpu.make_async_copy(src_hbm.at[pl.ds(t, 1), :], dst_ref.at[pl.ds(r, 1), :], sem).start()
        return carry

    lax.fori_loop(0, n, body, 0, unroll=8)


def _wait_row_gather(src_hbm, n, dst_ref, sem):
    def body(r, carry):
        pltpu.make_async_copy(src_hbm.at[pl.ds(0, 1), :], dst_ref.at[pl.ds(r, 1), :], sem).wait()
        return carry

    lax.fori_loop(0, n, body, 0, unroll=8)


def _group_rows_kernel(src_ref, h_hbm, o_ref, buf_ref, sem_ref, *, tg):
    i = pl.program_id(0)
    n = pl.num_programs(0)
    slot = i % 2

    @pl.when(i == 0)
    def _():
        _start_row_gather(h_hbm, src_ref, 0, tg, buf_ref.at[0], sem_ref.at[0])

    @pl.when(i + 1 < n)
    def _():
        _start_row_gather(h_hbm, src_ref, (i + 1) * tg, tg, buf_ref.at[1 - slot], sem_ref.at[1 - slot])

    _wait_row_gather(h_hbm, tg, buf_ref.at[slot], sem_ref.at[slot])
    o_ref[...] = buf_ref[slot].astype(o_ref.dtype)


def _group_rows(h, src):
    d = h.shape[1]
    r = src.shape[0]
    tg = min(TG_GATHER, r)
    kern = functools.partial(_group_rows_kernel, tg=tg)
    return pl.pallas_call(
        kern,
        grid_spec=pltpu.PrefetchScalarGridSpec(
            num_scalar_prefetch=1,
            grid=(r // tg,),
            in_specs=[pl.BlockSpec(memory_space=pl.ANY)],
            out_specs=pl.BlockSpec((tg, d), lambda i, s: (i, 0)),
            scratch_shapes=[pltpu.VMEM((2, tg, d), F32),
                            pltpu.SemaphoreType.DMA((2,))]),
        out_shape=jax.ShapeDtypeStruct((r, d), BF16),
        compiler_params=_cparams(("arbitrary",), 32),
        name="group_rows_by_expert",
    )(src, h)


def _combine_kernel(pos_ref, x_ref, g0_ref, g1_ref, fn_ref, y_hbm, o_ref, buf_ref, sem_ref,
                    *, tc, ntok):
    i = pl.program_id(0)
    n = pl.num_programs(0)
    slot = i % 2

    def start(step, s):
        _start_row_gather(y_hbm, pos_ref, step * tc, tc, buf_ref.at[s, 0], sem_ref.at[s, 0])
        _start_row_gather(y_hbm, pos_ref, ntok + step * tc, tc, buf_ref.at[s, 1], sem_ref.at[s, 1])

    @pl.when(i == 0)
    def _():
        start(0, 0)

    @pl.when(i + 1 < n)
    def _():
        start(i + 1, 1 - slot)

    _wait_row_gather(y_hbm, tc, buf_ref.at[slot, 0], sem_ref.at[slot, 0])
    _wait_row_gather(y_hbm, tc, buf_ref.at[slot, 1], sem_ref.at[slot, 1])
    x = x_ref[...] + g0_ref[...] * buf_ref[slot, 0] + g1_ref[...] * buf_ref[slot, 1]
    ms = jnp.mean(x * x, axis=-1, keepdims=True)
    o_ref[...] = x * lax.rsqrt(ms + NORM_EPS) * fn_ref[...]


def _combine_final_norm(x, y, pos, g0, g1, final_norm):
    m, d = x.shape
    tc = min(TC_COMBINE, m)
    kern = functools.partial(_combine_kernel, tc=tc, ntok=m)
    return pl.pallas_call(
        kern,
        grid_spec=pltpu.PrefetchScalarGridSpec(
            num_scalar_prefetch=1,
            grid=(m // tc,),
            in_specs=[pl.BlockSpec((tc, d), lambda i, s: (i, 0)),
                      pl.BlockSpec((tc, 1), lambda i, s: (i, 0)),
                      pl.BlockSpec((tc, 1), lambda i, s: (i, 0)),
                      pl.BlockSpec((1, d), lambda i, s: (0, 0)),
                      pl.BlockSpec(memory_space=pl.ANY)],
            out_specs=pl.BlockSpec((tc, d), lambda i, s: (i, 0)),
            scratch_shapes=[pltpu.VMEM((2, 2, tc, d), F32),
                            pltpu.SemaphoreType.DMA((2, 2))]),
        out_shape=jax.ShapeDtypeStruct((m, d), F32),
        compiler_params=_cparams(("arbitrary",), 32),
        name="moe_combine_final_norm",
    )(pos, x, g0, g1, final_norm.reshape(1, d), y)


def _tile_pipeline(off_ref, cnt_ref, n_col_tiles, n_tiles, in_copy, out_copy, zero_copy, compute):
    j = pl.program_id(0)
    e = pl.program_id(1)
    n_used = off_ref[N_EXPERTS]
    last_col = j == n_col_tiles - 1

    @pl.when((j == 0) & (e == 0))
    def _():
        cnt_ref[0] = 0
        for k in range(IN_RING):
            in_copy(k, k).start()

    def body(t, g):
        in_slot = g % IN_RING
        out_slot = g % OUT_RING
        in_copy(t, in_slot).wait()

        @pl.when(g >= OUT_RING)
        def _():
            out_copy(0, out_slot).wait()

        compute(in_slot, out_slot)
        out_copy(t, out_slot).start()
        ahead = t + IN_RING
        wraps = ahead >= n_used

        @pl.when(jnp.logical_not(wraps & last_col))
        def _():
            in_copy(jnp.where(wraps, ahead - n_used, ahead), in_slot).start()

        return g + 1

    g = lax.fori_loop(off_ref[e], off_ref[e + 1], body, cnt_ref[0])
    cnt_ref[0] = g

    @pl.when(e == N_EXPERTS - 1)
    def _():
        def tail(t, carry):
            zero_copy(t).start()
            zero_copy(t).wait()
            return carry

        lax.fori_loop(n_used, n_tiles, tail, 0)

        @pl.when(last_col)
        def _():
            for back in range(OUT_RING, 0, -1):
                @pl.when(g >= back)
                def _():
                    out_copy(0, (g - back) % OUT_RING).wait()


def _moe_up_kernel(off_ref, xs_hbm, wg_ref, wu_ref, act_hbm, wgb_ref, wub_ref, xbuf, obuf, zbuf,
                   cnt_ref, xsem, osem, zsem, *, tm, tf, n_tiles):
    j = pl.program_id(0)
    e = pl.program_id(1)

    @pl.when((j == 0) & (e == 0))
    def _():
        zbuf[...] = jnp.zeros_like(zbuf)

    @pl.when(off_ref[e + 1] > off_ref[e])
    def _():
        wgb_ref[...] = wg_ref[0].astype(BF16)
        wub_ref[...] = wu_ref[0].astype(BF16)

    def in_copy(t, slot):
        return pltpu.make_async_copy(xs_hbm.at[pl.ds(t * tm, tm), :], xbuf.at[slot], xsem.at[slot])

    def out_tile(t):
        return act_hbm.at[pl.ds(t * tm, tm), pl.ds(j * tf, tf)]

    def out_copy(t, slot):
        return pltpu.make_async_copy(obuf.at[slot], out_tile(t), osem.at[slot])

    def zero_copy(t):
        return pltpu.make_async_copy(zbuf, out_tile(t), zsem.at[0])

    def compute(in_slot, out_slot):
        x = xbuf[in_slot]
        gate = _bdot(x, wgb_ref[...])
        up = _bdot(x, wub_ref[...])
        obuf[out_slot] = (gate * jax.nn.sigmoid(gate) * up).astype(obuf.dtype)

    _tile_pipeline(off_ref, cnt_ref, pl.num_programs(0), n_tiles, in_copy, out_copy, zero_copy,
                   compute)


def _moe_up(xs, wg, wu, tile_off, tm):
    r, d = xs.shape
    f = wg.shape[2]
    tf = min(TF_MOE, f)
    kern = functools.partial(_moe_up_kernel, tm=tm, tf=tf, n_tiles=r // tm)
    return pl.pallas_call(
        kern,
        grid_spec=pltpu.PrefetchScalarGridSpec(
            num_scalar_prefetch=1,
            grid=(f // tf, N_EXPERTS),
            in_specs=[pl.BlockSpec(memory_space=pl.ANY),
                      pl.BlockSpec((1, d, tf), lambda j, e, off: (e, 0, j)),
                      pl.BlockSpec((1, d, tf), lambda j, e, off: (e, 0, j))],
            out_specs=pl.BlockSpec(memory_space=pl.ANY),
            scratch_shapes=[pltpu.VMEM((d, tf), BF16), pltpu.VMEM((d, tf), BF16),
                            pltpu.VMEM((IN_RING, tm, d), BF16),
                            pltpu.VMEM((OUT_RING, tm, tf), BF16),
                            pltpu.VMEM((tm, tf), BF16),
                            pltpu.SMEM((1,), jnp.int32),
                            pltpu.SemaphoreType.DMA((IN_RING,)),
                            pltpu.SemaphoreType.DMA((OUT_RING,)),
                            pltpu.SemaphoreType.DMA((1,))]),
        out_shape=jax.ShapeDtypeStruct((r, f), BF16),
        compiler_params=_cparams(("arbitrary", "arbitrary"), 56),
        name="moe_gate_up",
    )(tile_off, xs, wg, wu)


def _moe_down_kernel(off_ref, act_hbm, wd_ref, y_hbm, wdb_ref, abuf, obuf, zbuf, cnt_ref,
                     asem, osem, zsem, *, tm, tn, n_tiles):
    j = pl.program_id(0)
    e = pl.program_id(1)

    @pl.when((j == 0) & (e == 0))
    def _():
        zbuf[...] = jnp.zeros_like(zbuf)

    @pl.when(off_ref[e + 1] > off_ref[e])
    def _():
        wdb_ref[...] = wd_ref[0].astype(BF16)

    def in_copy(t, slot):
        return pltpu.make_async_copy(act_hbm.at[pl.ds(t * tm, tm), :], abuf.at[slot], asem.at[slot])

    def out_tile(t):
        return y_hbm.at[pl.ds(t * tm, tm), pl.ds(j * tn, tn)]

    def out_copy(t, slot):
        return pltpu.make_async_copy(obuf.at[slot], out_tile(t), osem.at[slot])

    def zero_copy(t):
        return pltpu.make_async_copy(zbuf, out_tile(t), zsem.at[0])

    def compute(in_slot, out_slot):
        obuf[out_slot] = _bdot(abuf[in_slot], wdb_ref[...])

    _tile_pipeline(off_ref, cnt_ref, pl.num_programs(0), n_tiles, in_copy, out_copy, zero_copy,
                   compute)


def _moe_down(act, wd, tile_off, tm):
    r, f = act.shape
    d = wd.shape[2]
    tn = min(TN_MOE, d)
    kern = functools.partial(_moe_down_kernel, tm=tm, tn=tn, n_tiles=r // tm)
    return pl.pallas_call(
        kern,
        grid_spec=pltpu.PrefetchScalarGridSpec(
            num_scalar_prefetch=1,
            grid=(d // tn, N_EXPERTS),
            in_specs=[pl.BlockSpec(memory_space=pl.ANY),
                      pl.BlockSpec((1, f, tn), lambda j, e, off: (e, 0, j))],
            out_specs=pl.BlockSpec(memory_space=pl.ANY),
            scratch_shapes=[pltpu.VMEM((f, tn), BF16),
                            pltpu.VMEM((IN_RING, tm, f), BF16),
                            pltpu.VMEM((OUT_RING, tm, tn), F32),
                            pltpu.VMEM((tm, tn), F32),
                            pltpu.SMEM((1,), jnp.int32),
                            pltpu.SemaphoreType.DMA((IN_RING,)),
                            pltpu.SemaphoreType.DMA((OUT_RING,)),
                            pltpu.SemaphoreType.DMA((1,))]),
        out_shape=jax.ShapeDtypeStruct((r, d), F32),
        compiler_params=_cparams(("arbitrary", "arbitrary"), 56),
        name="moe_down",
    )(tile_off, act, wd)


def _routing_tables(gates, sel, rank, counts, tm):
    ntok = gates.shape[0]
    n_tiles = (2 * ntok) // tm + N_EXPERTS
    cnt = counts[0, :N_EXPERTS].astype(jnp.int32)
    padded = ((cnt + tm - 1) // tm) * tm
    ends = jnp.cumsum(padded)
    offs = ends - padded
    selb = sel[:, :N_EXPERTS] > 0.5
    pos = offs[None, :] + rank[:, :N_EXPERTS].astype(jnp.int32)
    lane = jnp.arange(N_EXPERTS, dtype=jnp.int32)[None, :]
    lo = selb & (lane == jnp.min(jnp.where(selb, lane, N_EXPERTS), axis=1, keepdims=True))
    hi = selb & ~lo
    pick = lambda a, onehot: jnp.sum(jnp.where(onehot, a, 0), axis=1)
    pos0, pos1 = pick(pos, lo), pick(pos, hi)
    g8 = gates[:, :N_EXPERTS]
    g0, g1 = pick(g8, lo), pick(g8, hi)
    tok = jnp.arange(ntok, dtype=jnp.int32)
    pos_all = jnp.concatenate([pos0, pos1]).astype(jnp.int32)
    src = jnp.zeros((n_tiles * tm,), jnp.int32).at[pos_all].set(
        jnp.concatenate([tok, tok]), unique_indices=True)
    tile_off = jnp.concatenate([jnp.zeros((1,), jnp.int32), (ends // tm).astype(jnp.int32)])
    return src, pos_all, g0[:, None], g1[:, None], tile_off


def _moe_final(x, g, router_w, wg, wu, wd, final_norm):
    h, gates, sel, rank, counts = _router(x, g, router_w)
    tm = min(TM_MOE, x.shape[0])
    assert 2 * x.shape[0] // tm >= IN_RING, "the tile ring needs at least IN_RING used row tiles"
    src, pos_all, g0, g1, tile_off = _routing_tables(gates, sel, rank, counts, tm)
    xs = _group_rows(h, src)
    act = _moe_up(xs, wg, wu, tile_off, tm)
    y = _moe_down(act, wd, tile_off, tm)
    return _combine_final_norm(x, y, pos_all, g0, g1, final_norm)


def _forward(x, norm_mix_even, w_in_even, conv3_w, w_out_even, norm_ffn_even,
             ffn_w_gate, ffn_w_up, ffn_w_down, norm_mix_odd, cf_pw1_w, cf_pw1_b,
             cf_dw_w, cf_dw_b, cf_ln_g, cf_ln_b, cf_pw2_w, cf_pw2_b, norm_moe,
             router_w, moe_w_gate, moe_w_up, moe_w_down, final_norm):
    batch, seq, d = x.shape
    xt = x.reshape(batch * seq, d)
    ncol = SC_WIDTH // LANES
    q_lo, q_hi = 3 * SC_WIDTH, 3 * SC_WIDTH + SB_HEADS * SB_HEAD_DIM
    col = jnp.arange(w_in_even.shape[2])
    col_scale = jnp.where((col >= q_lo) & (col < q_hi), -(SB_HEAD_DIM ** -0.5) * LOG2E, 1.0)
    p = _norm_matmul(xt, norm_mix_even[0], w_in_even[0], col_scale.astype(F32), BF16)
    ysb = _stick_breaking(p, batch, seq, 3 * ncol, 3 * ncol + SB_HEADS, 3 * ncol + 2 * SB_HEADS)
    xt = _mix_out_proj(p, ysb, conv3_w[0], w_out_even[0], xt, seq)
    xt = _dense_swiglu(xt, norm_ffn_even[0], ffn_w_gate[0], ffn_w_up[0], ffn_w_down[0])
    u = _norm_glu(xt, norm_mix_odd[0], cf_pw1_w[0], cf_pw1_b[0])
    v = _dwconv_ln_silu(u, cf_dw_w[0], cf_dw_b[0], cf_ln_g[0], cf_ln_b[0], seq)
    xt = _matmul_bias_res(v, cf_pw2_w[0], cf_pw2_b[0], xt)
    out = _moe_final(xt, norm_moe[0], router_w[0], moe_w_gate[0], moe_w_up[0], moe_w_down[0],
                     final_norm)
    return out.reshape(batch, seq, d)


def kernel(x, norm_mix_even, w_in_even, conv3_w, w_out_even, norm_ffn_even, ffn_w_gate, ffn_w_up, ffn_w_down, norm_mix_odd, cf_pw1_w, cf_pw1_b, cf_dw_w, cf_dw_b, cf_ln_g, cf_ln_b, cf_pw2_w, cf_pw2_b, norm_moe, router_w, moe_w_gate, moe_w_up, moe_w_down, final_norm):
    return _forward(x, norm_mix_even, w_in_even, conv3_w, w_out_even, norm_ffn_even,
                    ffn_w_gate, ffn_w_up, ffn_w_down, norm_mix_odd, cf_pw1_w, cf_pw1_b,
                    cf_dw_w, cf_dw_b, cf_ln_g, cf_ln_b, cf_pw2_w, cf_pw2_b, norm_moe,
                    router_w, moe_w_gate, moe_w_up, moe_w_down, final_norm)
```

```python
import functools

import jax
import jax.numpy as jnp
from jax import lax
from jax.experimental import pallas as pl
from jax.experimental.pallas import tpu as pltpu

F32 = jnp.float32
BF16 = jnp.bfloat16

NORM_EPS = 1e-6
SC_WIDTH = 1024
SC_KERNEL = 3
SB_HEADS = 8
SB_HEAD_DIM = 128
CF_KERNEL = 31
N_EXPERTS = 8
LANES = 128
SUBLANES = 8
MIB = 1024 * 1024

TM_PROJ = 1024
TN_PROJ = 1024
TM_RESIDENT = 512
TN_GLU = 512
TM_FFN = 1024
TF_FFN = 256
TQ_ATTN = 512
ATTN_UNROLL = 6
LOG2E = 1.4426950408889634
TM_CONV = 512
CONV_HALO = 32
SC_HALO = 16
TM_ROUTE = 512
TM_MOE = 256
TF_MOE = 1024
TN_MOE = 512
IN_RING = 3
OUT_RING = 2
TG_GATHER = 256
TC_COMBINE = 256


def _cparams(semantics, vmem_mib):
    return pltpu.CompilerParams(dimension_semantics=semantics,
                                vmem_limit_bytes=vmem_mib * MIB)


def _rmsnorm_to(x_ref, g_ref, h_ref, chunk=128):
    g = g_ref[...]

    def body(c, carry):
        r = pl.multiple_of(c * chunk, chunk)
        x = x_ref[pl.ds(r, chunk), :]
        ms = jnp.mean(x * x, axis=-1, keepdims=True)
        h_ref[pl.ds(r, chunk), :] = (x * lax.rsqrt(ms + NORM_EPS) * g).astype(h_ref.dtype)
        return carry

    lax.fori_loop(0, x_ref.shape[0] // chunk, body, 0)


def _bdot(a, b):
    return jnp.dot(a, b, preferred_element_type=F32)


def _norm_mm_kernel(x_ref, g_ref, w_ref, cs_ref, o_ref, h_ref):
    @pl.when(pl.program_id(1) == 0)
    def _():
        _rmsnorm_to(x_ref, g_ref, h_ref)

    o_ref[...] = (_bdot(h_ref[...], w_ref[...].astype(BF16)) * cs_ref[...]).astype(o_ref.dtype)


def _norm_matmul(x, g, w, col_scale, out_dtype):
    m, k = x.shape
    n = w.shape[1]
    tm, tn = min(TM_PROJ, m), min(TN_PROJ, n)
    return pl.pallas_call(
        _norm_mm_kernel,
        grid=(m // tm, n // tn),
        in_specs=[pl.BlockSpec((tm, k), lambda i, j: (i, 0)),
                  pl.BlockSpec((1, k), lambda i, j: (0, 0)),
                  pl.BlockSpec((k, tn), lambda i, j: (0, j)),
                  pl.BlockSpec((1, tn), lambda i, j: (0, j))],
        out_specs=pl.BlockSpec((tm, tn), lambda i, j: (i, j)),
        out_shape=jax.ShapeDtypeStruct((m, n), out_dtype),
        scratch_shapes=[pltpu.VMEM((tm, k), BF16)],
        compiler_params=_cparams(("parallel", "arbitrary"), 56),
        name="norm_in_proj",
    )(x, g.reshape(1, k), w, col_scale.reshape(1, n))


def _norm_glu_kernel(x_ref, g_ref, wa_ref, wg_ref, ba_ref, bg_ref, o_ref, h_ref):
    @pl.when(pl.program_id(1) == 0)
    def _():
        _rmsnorm_to(x_ref, g_ref, h_ref)

    h = h_ref[...]
    a = _bdot(h, wa_ref[...].astype(BF16)) + ba_ref[...]
    gate = _bdot(h, wg_ref[...].astype(BF16)) + bg_ref[...]
    o_ref[...] = (a * jax.nn.sigmoid(gate)).astype(o_ref.dtype)


def _norm_glu(x, g, w, b):
    m, k = x.shape
    n = w.shape[1] // 2
    tm, tn = min(TM_PROJ, m), min(TN_GLU, n)
    nb = n // tn
    b2 = b.reshape(1, 2 * n)
    return pl.pallas_call(
        _norm_glu_kernel,
        grid=(m // tm, nb),
        in_specs=[pl.BlockSpec((tm, k), lambda i, j: (i, 0)),
                  pl.BlockSpec((1, k), lambda i, j: (0, 0)),
                  pl.BlockSpec((k, tn), lambda i, j: (0, j)),
                  pl.BlockSpec((k, tn), lambda i, j: (0, j + nb)),
                  pl.BlockSpec((1, tn), lambda i, j: (0, j)),
                  pl.BlockSpec((1, tn), lambda i, j: (0, j + nb))],
        out_specs=pl.BlockSpec((tm, tn), lambda i, j: (i, j)),
        out_shape=jax.ShapeDtypeStruct((m, n), F32),
        scratch_shapes=[pltpu.VMEM((tm, k), BF16)],
        compiler_params=_cparams(("parallel", "arbitrary"), 56),
        name="norm_pw1_glu",
    )(x, g.reshape(1, k), w, w, b2, b2)


def _mm_bias_res_kernel(a_ref, w_ref, b_ref, r_ref, o_ref, wb_ref):
    @pl.when(pl.program_id(0) == 0)
    def _():
        wb_ref[...] = w_ref[...].astype(BF16)

    o_ref[...] = r_ref[...] + _bdot(a_ref[...], wb_ref[...]) + b_ref[...]


def _matmul_bias_res(a, w, b, res):
    m, k = a.shape
    n = w.shape[1]
    tm = min(TM_RESIDENT, m)
    return pl.pallas_call(
        _mm_bias_res_kernel,
        grid=(m // tm,),
        in_specs=[pl.BlockSpec((tm, k), lambda i: (i, 0)),
                  pl.BlockSpec((k, n), lambda i: (0, 0), pipeline_mode=pl.Buffered(1)),
                  pl.BlockSpec((1, n), lambda i: (0, 0)),
                  pl.BlockSpec((tm, n), lambda i: (i, 0))],
        out_specs=pl.BlockSpec((tm, n), lambda i: (i, 0)),
        out_shape=jax.ShapeDtypeStruct((m, n), F32),
        scratch_shapes=[pltpu.VMEM((k, n), BF16)],
        compiler_params=_cparams(("arbitrary",), 56),
        name="pw2_residual",
    )(a, w, b.reshape(1, n), res)


def _attn_kernel(q_ref, k_ref, v_ref, o_ref, u_ref, c_ref, acc_ref, nz_ref, cm_ref, *, tq):
    hk = LANES
    tk = 2 * hk
    q0 = pl.program_id(2) * tq
    q = q_ref[...]
    c_ref[...] = jnp.zeros_like(c_ref)
    acc_ref[...] = jnp.zeros_like(acc_ref)

    jj = lax.broadcasted_iota(jnp.int32, (2 * hk, 2 * hk), 0) & (hk - 1)
    ss = lax.broadcasted_iota(jnp.int32, (2 * hk, 2 * hk), 1)
    u_ref[...] = jnp.where((ss >= hk) | (jj >= ss), 1.0, 0.0).astype(BF16)

    def half_sums(lk):
        hi = lk.astype(BF16)
        lo = (lk - hi.astype(F32)).astype(BF16)
        return _bdot(jnp.concatenate([hi, lo], axis=1), u_ref[...])

    def scores(kstart, slot, mask, r0=0):
        kb = k_ref[pl.ds(kstart, tk), :]
        nz = lax.dot_general(q[r0:], kb, (((1,), (1,)), ((), ())), preferred_element_type=F32)
        neg_abs = lax.bitcast_convert_type(
            lax.bitcast_convert_type(nz, jnp.uint32) | jnp.uint32(0x80000000), F32)
        log_keep = jnp.minimum(nz, 0.0) - jnp.log2(1.0 + jnp.exp2(neg_abs))
        if mask is not None:
            log_keep = jnp.where(mask[r0:], log_keep, 0.0)
        nz_ref[slot, r0:] = nz
        cm_ref[slot, 1, r0:] = half_sums(log_keep[:, hk:])
        cm_ref[slot, 0, r0:] = half_sums(log_keep[:, :hk])

    def weights(kstart, slot, mask, r0=0):
        c = c_ref[r0:]
        nz = nz_ref[slot, r0:]
        cm1 = cm_ref[slot, 1, r0:]
        cm0 = cm_ref[slot, 0, r0:]
        logit1 = (c + cm1[:, :hk]) - nz[:, hk:]
        c = c + cm1[:, hk:]
        logit0 = (c + cm0[:, :hk]) - nz[:, :hk]
        c_ref[r0:] = c + cm0[:, hk:]
        a = jnp.exp2(jnp.concatenate([logit0, logit1], axis=1))
        if mask is not None:
            a = jnp.where(mask[r0:], a, 0.0)
        acc_ref[r0:] += _bdot(a.astype(BF16), v_ref[pl.ds(kstart, tk), :])

    rows = lax.broadcasted_iota(jnp.int32, (tq, tk), 0)
    cols = lax.broadcasted_iota(jnp.int32, (tq, tk), 1)
    nd = tq // tk
    assert nd == 2 and ATTN_UNROLL % 2 == 0
    diag = [(pl.multiple_of(q0 + d * tk, tk), (cols + d * tk) < rows, d * tk)
            for d in reversed(range(nd))]
    nb = q0 // tk

    def below(j):
        return pl.multiple_of(q0 - (j + 1) * tk, tk)

    scores(diag[0][0], 0, diag[0][1], diag[0][2])
    for i in range(1, nd):
        scores(diag[i][0], i % 2, diag[i][1], diag[i][2])
        weights(diag[i - 1][0], (i - 1) % 2, diag[i - 1][1], diag[i - 1][2])
    last_k, last_mask, last_r0 = diag[nd - 1]

    @pl.when(nb == 0)
    def _():
        weights(last_k, (nd - 1) % 2, last_mask, last_r0)

    @pl.when(nb > 0)
    def _():
        scores(below(0), nd % 2, None)
        weights(last_k, (nd - 1) % 2, last_mask, last_r0)

        def steps(j0, count):
            for k in range(count):
                weights(below(j0 + k), (nd + k) % 2, None)
                scores(below(j0 + k + 1), (nd + k + 1) % 2, None)

        def body(i, carry):
            steps(ATTN_UNROLL * i, ATTN_UNROLL)
            return carry

        trips = (nb - 1) // ATTN_UNROLL
        lax.fori_loop(0, trips, body, 0)
        rest = nb - 1 - ATTN_UNROLL * trips
        for r in range(1, ATTN_UNROLL, 2):
            @pl.when(rest == r)
            def _():
                steps(nb - 1 - r, r)
                weights(below(nb - 1), (nd + r) % 2, None)

    o_ref[...] = acc_ref[...].astype(o_ref.dtype)


def _stick_breaking(p, batch, seq, col_q, col_k, col_v):
    tq = min(TQ_ATTN, seq)
    nq = seq // tq
    dh = SB_HEAD_DIM
    kern = functools.partial(_attn_kernel, tq=tq)
    return pl.pallas_call(
        kern,
        grid=(batch, SB_HEADS, nq),
        in_specs=[pl.BlockSpec((tq, dh), lambda b, h, i: (b * nq + i, col_q + h)),
                  pl.BlockSpec((seq, dh), lambda b, h, i: (b, col_k + h)),
                  pl.BlockSpec((seq, dh), lambda b, h, i: (b, col_v + h))],
        out_specs=pl.BlockSpec((tq, dh), lambda b, h, i: (b * nq + i, h)),
        out_shape=jax.ShapeDtypeStruct((batch * seq, SB_HEADS * dh), BF16),
        scratch_shapes=[pltpu.VMEM((2 * LANES, 2 * LANES), BF16),
                        pltpu.VMEM((tq, LANES), F32),
                        pltpu.VMEM((tq, dh), F32),
                        pltpu.VMEM((2, tq, 2 * LANES), F32),
                        pltpu.VMEM((2, 2, tq, 2 * LANES), F32)],
        compiler_params=_cparams(("parallel", "parallel", "arbitrary"), 32),
        name="stick_breaking_attention",
    )(p, p, p)


def _mix_out_kernel(pb_ref, pc_ref, ph_ref, hc_ref, hh_ref, cw_ref, ysb_ref, w_ref, x_ref,
                    o_ref, mix_ref, u_ref, wb_ref, *, tm, seq, chunk):
    i = pl.program_id(0)

    @pl.when(i == 0)
    def _():
        wb_ref[...] = w_ref[...].astype(BF16)

    halo = hc_ref[...].astype(F32) * hh_ref[...].astype(F32)
    u_ref[0:SC_HALO, :] = jnp.where((i * tm) % seq == 0, 0.0, halo)
    for r in range(0, tm, chunk):
        u_ref[SC_HALO + r:SC_HALO + r + chunk, :] = (
            pc_ref[r:r + chunk, :].astype(F32) * ph_ref[r:r + chunk, :].astype(F32))
    w0, w1, w2 = cw_ref[0:1, :], cw_ref[1:2, :], cw_ref[2:3, :]
    for r in range(0, tm, chunk):
        base = SC_HALO + r
        conv = (w2 * u_ref[base:base + chunk, :]
                + w1 * u_ref[base - 1:base - 1 + chunk, :]
                + w0 * u_ref[base - 2:base - 2 + chunk, :])
        ysc = pb_ref[r:r + chunk, :].astype(F32) * conv
        mix_ref[r:r + chunk, 0:SC_WIDTH] = ysc.astype(BF16)
    mix_ref[:, SC_WIDTH:] = ysb_ref[...]

    o_ref[...] = x_ref[...] + _bdot(mix_ref[...], wb_ref[...])


def _mix_out_proj(p, ysb, conv_w, w_out, x, seq):
    m = x.shape[0]
    d = w_out.shape[1]
    kdim = w_out.shape[0]
    tm = min(TM_RESIDENT, seq)
    hb = tm // SC_HALO
    kern = functools.partial(_mix_out_kernel, tm=tm, seq=seq, chunk=min(128, tm))
    return pl.pallas_call(
        kern,
        grid=(m // tm,),
        in_specs=[pl.BlockSpec((tm, SC_WIDTH), lambda i: (i, 0)),
                  pl.BlockSpec((tm, SC_WIDTH), lambda i: (i, 1)),
                  pl.BlockSpec((tm, SC_WIDTH), lambda i: (i, 2)),
                  pl.BlockSpec((SC_HALO, SC_WIDTH), lambda i: (jnp.maximum(i * hb - 1, 0), 1)),
                  pl.BlockSpec((SC_HALO, SC_WIDTH), lambda i: (jnp.maximum(i * hb - 1, 0), 2)),
                  pl.BlockSpec((SC_KERNEL, SC_WIDTH), lambda i: (0, 0)),
                  pl.BlockSpec((tm, SC_WIDTH), lambda i: (i, 0)),
                  pl.BlockSpec((kdim, d), lambda i: (0, 0), pipeline_mode=pl.Buffered(1)),
                  pl.BlockSpec((tm, d), lambda i: (i, 0))],
        out_specs=pl.BlockSpec((tm, d), lambda i: (i, 0)),
        out_shape=jax.ShapeDtypeStruct((m, d), F32),
        scratch_shapes=[pltpu.VMEM((tm, kdim), BF16),
                        pltpu.VMEM((tm + SC_HALO, SC_WIDTH), F32),
                        pltpu.VMEM((kdim, d), BF16)],
        compiler_params=_cparams(("arbitrary",), 56),
        name="shortconv_out_proj",
    )(p, p, p, p, p, conv_w, ysb, w_out, x)


def _ffn_kernel(x_ref, g_ref, wg_ref, wu_ref, wd_ref, o_ref, h_ref):
    @pl.when(pl.program_id(1) == 0)
    def _():
        _rmsnorm_to(x_ref, g_ref, h_ref)
        o_ref[...] = x_ref[...]

    h = h_ref[...]
    gate = _bdot(h, wg_ref[...].astype(BF16))
    up = _bdot(h, wu_ref[...].astype(BF16))
    act = (gate * jax.nn.sigmoid(gate) * up).astype(BF16)
    o_ref[...] += _bdot(act, wd_ref[...].astype(BF16))


def _dense_swiglu(x, g, wg, wu, wd):
    m, d = x.shape
    f = wg.shape[1]
    tm, tf = min(TM_FFN, m), min(TF_FFN, f)
    return pl.pallas_call(
        _ffn_kernel,
        grid=(m // tm, f // tf),
        in_specs=[pl.BlockSpec((tm, d), lambda i, j: (i, 0), pipeline_mode=pl.Buffered(1)),
                  pl.BlockSpec((1, d), lambda i, j: (0, 0)),
                  pl.BlockSpec((d, tf), lambda i, j: (0, j)),
                  pl.BlockSpec((d, tf), lambda i, j: (0, j)),
                  pl.BlockSpec((tf, d), lambda i, j: (j, 0))],
        out_specs=pl.BlockSpec((tm, d), lambda i, j: (i, 0)),
        out_shape=jax.ShapeDtypeStruct((m, d), F32),
        scratch_shapes=[pltpu.VMEM((tm, d), BF16)],
        compiler_params=_cparams(("parallel", "arbitrary"), 60),
        name="dense_swiglu",
    )(x, g.reshape(1, d), wg, wu, wd)


def _dwconv_ln_kernel(u_ref, halo_ref, w_ref, b_ref, lg_ref, lb_ref, o_ref, buf_ref, conv_ref,
                      *, tm, seq, rchunk):
    i = pl.program_id(0)
    buf_ref[0:CONV_HALO, :] = jnp.where((i * tm) % seq == 0, 0.0, halo_ref[...])
    buf_ref[CONV_HALO:, :] = u_ref[...]
    nlt = u_ref.shape[1] // LANES
    first = CONV_HALO - (CF_KERNEL - 1)

    def lane_tile(c, carry):
        lanes = pl.ds(pl.multiple_of(c * LANES, LANES), LANES)
        for r in range(0, tm, rchunk):
            acc = None
            for m in range(SUBLANES):
                rows = rchunk + (SUBLANES if m else 0)
                g = None
                for k in range(CF_KERNEL):
                    if (first + k) % SUBLANES != m:
                        continue
                    base = r + first + k - m
                    term = w_ref[k:k + 1, lanes] * buf_ref[base:base + rows, lanes]
                    g = term if g is None else g + term
                if m:
                    g = pltpu.roll(g, rows - m, axis=0)[:rchunk]
                acc = g if acc is None else acc + g
            conv_ref[r:r + rchunk, lanes] = acc
        return carry

    lax.fori_loop(0, nlt, lane_tile, 0)

    bias, lg, lb = b_ref[...], lg_ref[...], lb_ref[...]
    for r in range(0, tm, rchunk):
        y = conv_ref[r:r + rchunk, :] + bias
        mu = jnp.mean(y, axis=-1, keepdims=True)
        yc = y - mu
        var = jnp.mean(yc * yc, axis=-1, keepdims=True)
        t = yc * lax.rsqrt(var + NORM_EPS) * lg + lb
        o_ref[r:r + rchunk, :] = (t * jax.nn.sigmoid(t)).astype(o_ref.dtype)


def _dwconv_ln_silu(u, w, b, lg, lb, seq):
    m, c = u.shape
    tm = min(TM_CONV, seq)
    hb = tm // CONV_HALO
    kern = functools.partial(_dwconv_ln_kernel, tm=tm, seq=seq, rchunk=min(128, tm))
    return pl.pallas_call(
        kern,
        grid=(m // tm,),
        in_specs=[pl.BlockSpec((tm, c), lambda i: (i, 0)),
                  pl.BlockSpec((CONV_HALO, c), lambda i: (jnp.maximum(i * hb - 1, 0), 0)),
                  pl.BlockSpec((CF_KERNEL, c), lambda i: (0, 0)),
                  pl.BlockSpec((1, c), lambda i: (0, 0)),
                  pl.BlockSpec((1, c), lambda i: (0, 0)),
                  pl.BlockSpec((1, c), lambda i: (0, 0))],
        out_specs=pl.BlockSpec((tm, c), lambda i: (i, 0)),
        out_shape=jax.ShapeDtypeStruct((m, c), BF16),
        scratch_shapes=[pltpu.VMEM((tm + CONV_HALO, c), F32),
                        pltpu.VMEM((tm, c), F32)],
        compiler_params=_cparams(("parallel",), 32),
        name="dwconv_ln_silu",
    )(u, u, w, b.reshape(1, c), lg.reshape(1, c), lb.reshape(1, c))


def _router_kernel(x_ref, g_ref, rw_ref, h_ref, gates_ref, sel_ref, rank_ref, cnt_ref, carry_ref,
                   *, tm):
    @pl.when(pl.program_id(0) == 0)
    def _():
        carry_ref[...] = jnp.zeros_like(carry_ref)

    _rmsnorm_to(x_ref, g_ref, h_ref)
    logits = jnp.dot(h_ref[...], rw_ref[...], preferred_element_type=F32,
                     precision=lax.Precision.HIGHEST)
    lane = lax.broadcasted_iota(jnp.int32, logits.shape, 1).astype(F32)
    neg = jnp.float32(-jnp.inf)
    logits = jnp.where(lane < N_EXPERTS, logits, neg)
    m1 = jnp.max(logits, axis=-1, keepdims=True)
    i1 = jnp.min(jnp.where(logits == m1, lane, float(LANES)), axis=-1, keepdims=True)
    rest = jnp.where(lane == i1, neg, logits)
    m2 = jnp.max(rest, axis=-1, keepdims=True)
    i2 = jnp.min(jnp.where(rest == m2, lane, float(LANES)), axis=-1, keepdims=True)
    e2 = jnp.exp(m2 - m1)
    w1 = 1.0 / (1.0 + e2)
    w2 = e2 / (1.0 + e2)
    gates_ref[...] = jnp.where(lane == i1, w1, 0.0) + jnp.where(lane == i2, w2, 0.0)
    sel = jnp.where((lane == i1) | (lane == i2), 1.0, 0.0)
    sel_ref[...] = sel
    rr = lax.broadcasted_iota(jnp.int32, (tm, tm), 0)
    cc = lax.broadcasted_iota(jnp.int32, (tm, tm), 1)
    tri = jnp.where(cc < rr, 1.0, 0.0).astype(BF16)
    rank_ref[...] = _bdot(tri, sel.astype(BF16)) + carry_ref[...]
    carry_ref[...] += jnp.sum(sel, axis=0, keepdims=True)
    cnt_ref[...] = carry_ref[...]


def _router(x, g, router_w):
    m, d = x.shape
    tm = min(TM_ROUTE, m)
    rw = jnp.zeros((d, LANES), F32).at[:, :N_EXPERTS].set(router_w)
    kern = functools.partial(_router_kernel, tm=tm)
    wide = jax.ShapeDtypeStruct((m, LANES), F32)
    return pl.pallas_call(
        kern,
        grid=(m // tm,),
        in_specs=[pl.BlockSpec((tm, d), lambda i: (i, 0)),
                  pl.BlockSpec((1, d), lambda i: (0, 0)),
                  pl.BlockSpec((d, LANES), lambda i: (0, 0))],
        out_specs=[pl.BlockSpec((tm, d), lambda i: (i, 0)),
                   pl.BlockSpec((tm, LANES), lambda i: (i, 0)),
                   pl.BlockSpec((tm, LANES), lambda i: (i, 0)),
                   pl.BlockSpec((tm, LANES), lambda i: (i, 0)),
                   pl.BlockSpec((1, LANES), lambda i: (0, 0))],
        out_shape=[jax.ShapeDtypeStruct((m, d), F32), wide, wide, wide,
                   jax.ShapeDtypeStruct((1, LANES), F32)],
        scratch_shapes=[pltpu.VMEM((1, LANES), F32)],
        compiler_params=_cparams(("arbitrary",), 32),
        name="router_top2",
    )(x, g.reshape(1, d), rw)


def _start_row_gather(src_hbm, idx_ref, base, n, dst_ref, sem):
    def body(r, carry):
        t = idx_ref[base + r]
        pltpu.make_async_copy(src_hbm.at[pl.ds(t, 1), :], dst_ref.at[pl.ds(r, 1), :], sem).start()
        return carry

    lax.fori_loop(0, n, body, 0, unroll=8)


def _wait_row_gather(src_hbm, n, dst_ref, sem):
    def body(r, carry):
        pltpu.make_async_copy(src_hbm.at[pl.ds(0, 1), :], dst_ref.at[pl.ds(r, 1), :], sem).wait()
        return carry

    lax.fori_loop(0, n, body, 0, unroll=8)


def _group_rows_kernel(src_ref, h_hbm, o_ref, buf_ref, sem_ref, *, tg):
    i = pl.program_id(0)
    n = pl.num_programs(0)
    slot = i % 2

    @pl.when(i == 0)
    def _():
        _start_row_gather(h_hbm, src_ref, 0, tg, buf_ref.at[0], sem_ref.at[0])

    @pl.when(i + 1 < n)
    def _():
        _start_row_gather(h_hbm, src_ref, (i + 1) * tg, tg, buf_ref.at[1 - slot], sem_ref.at[1 - slot])

    _wait_row_gather(h_hbm, tg, buf_ref.at[slot], sem_ref.at[slot])
    o_ref[...] = buf_ref[slot].astype(o_ref.dtype)


def _group_rows(h, src):
    d = h.shape[1]
    r = src.shape[0]
    tg = min(TG_GATHER, r)
    kern = functools.partial(_group_rows_kernel, tg=tg)
    return pl.pallas_call(
        kern,
        grid_spec=pltpu.PrefetchScalarGridSpec(
            num_scalar_prefetch=1,
            grid=(r // tg,),
            in_specs=[pl.BlockSpec(memory_space=pl.ANY)],
            out_specs=pl.BlockSpec((tg, d), lambda i, s: (i, 0)),
            scratch_shapes=[pltpu.VMEM((2, tg, d), F32),
                            pltpu.SemaphoreType.DMA((2,))]),
        out_shape=jax.ShapeDtypeStruct((r, d), BF16),
        compiler_params=_cparams(("arbitrary",), 32),
        name="group_rows_by_expert",
    )(src, h)


def _combine_kernel(pos_ref, x_ref, g0_ref, g1_ref, fn_ref, y_hbm, o_ref, buf_ref, sem_ref,
                    *, tc, ntok):
    i = pl.program_id(0)
    n = pl.num_programs(0)
    slot = i % 2

    def start(step, s):
        _start_row_gather(y_hbm, pos_ref, step * tc, tc, buf_ref.at[s, 0], sem_ref.at[s, 0])
        _start_row_gather(y_hbm, pos_ref, ntok + step * tc, tc, buf_ref.at[s, 1], sem_ref.at[s, 1])

    @pl.when(i == 0)
    def _():
        start(0, 0)

    @pl.when(i + 1 < n)
    def _():
        start(i + 1, 1 - slot)

    _wait_row_gather(y_hbm, tc, buf_ref.at[slot, 0], sem_ref.at[slot, 0])
    _wait_row_gather(y_hbm, tc, buf_ref.at[slot, 1], sem_ref.at[slot, 1])
    x = x_ref[...] + g0_ref[...] * buf_ref[slot, 0] + g1_ref[...] * buf_ref[slot, 1]
    ms = jnp.mean(x * x, axis=-1, keepdims=True)
    o_ref[...] = x * lax.rsqrt(ms + NORM_EPS) * fn_ref[...]


def _combine_final_norm(x, y, pos, g0, g1, final_norm):
    m, d = x.shape
    tc = min(TC_COMBINE, m)
    kern = functools.partial(_combine_kernel, tc=tc, ntok=m)
    return pl.pallas_call(
        kern,
        grid_spec=pltpu.PrefetchScalarGridSpec(
            num_scalar_prefetch=1,
            grid=(m // tc,),
            in_specs=[pl.BlockSpec((tc, d), lambda i, s: (i, 0)),
                      pl.BlockSpec((tc, 1), lambda i, s: (i, 0)),
                      pl.BlockSpec((tc, 1), lambda i, s: (i, 0)),
                      pl.BlockSpec((1, d), lambda i, s: (0, 0)),
                      pl.BlockSpec(memory_space=pl.ANY)],
            out_specs=pl.BlockSpec((tc, d), lambda i, s: (i, 0)),
            scratch_shapes=[pltpu.VMEM((2, 2, tc, d), F32),
                            pltpu.SemaphoreType.DMA((2, 2))]),
        out_shape=jax.ShapeDtypeStruct((m, d), F32),
        compiler_params=_cparams(("arbitrary",), 32),
        name="moe_combine_final_norm",
    )(pos, x, g0, g1, final_norm.reshape(1, d), y)


def _tile_pipeline(off_ref, cnt_ref, n_col_tiles, n_tiles, in_copy, out_copy, zero_copy, compute):
    j = pl.program_id(0)
    e = pl.program_id(1)
    n_used = off_ref[N_EXPERTS]
    last_col = j == n_col_tiles - 1

    @pl.when((j == 0) & (e == 0))
    def _():
        cnt_ref[0] = 0
        for k in range(IN_RING):
            in_copy(k, k).start()

    def body(t, g):
        in_slot = g % IN_RING
        out_slot = g % OUT_RING
        in_copy(t, in_slot).wait()

        @pl.when(g >= OUT_RING)
        def _():
            out_copy(0, out_slot).wait()

        compute(in_slot, out_slot)
        out_copy(t, out_slot).start()
        ahead = t + IN_RING
        wraps = ahead >= n_used

        @pl.when(jnp.logical_not(wraps & last_col))
        def _():
            in_copy(jnp.where(wraps, ahead - n_used, ahead), in_slot).start()

        return g + 1

    g = lax.fori_loop(off_ref[e], off_ref[e + 1], body, cnt_ref[0])
    cnt_ref[0] = g

    @pl.when(e == N_EXPERTS - 1)
    def _():
        def tail(t, carry):
            zero_copy(t).start()
            zero_copy(t).wait()
            return carry

        lax.fori_loop(n_used, n_tiles, tail, 0)

        @pl.when(last_col)
        def _():
            for back in range(OUT_RING, 0, -1):
                @pl.when(g >= back)
                def _():
                    out_copy(0, (g - back) % OUT_RING).wait()


def _moe_up_kernel(off_ref, xs_hbm, wg_ref, wu_ref, act_hbm, wgb_ref, wub_ref, xbuf, obuf, zbuf,
                   cnt_ref, xsem, osem, zsem, *, tm, tf, n_tiles):
    j = pl.program_id(0)
    e = pl.program_id(1)

    @pl.when((j == 0) & (e == 0))
    def _():
        zbuf[...] = jnp.zeros_like(zbuf)

    @pl.when(off_ref[e + 1] > off_ref[e])
    def _():
        wgb_ref[...] = wg_ref[0].astype(BF16)
        wub_ref[...] = wu_ref[0].astype(BF16)

    def in_copy(t, slot):
        return pltpu.make_async_copy(xs_hbm.at[pl.ds(t * tm, tm), :], xbuf.at[slot], xsem.at[slot])

    def out_tile(t):
        return act_hbm.at[pl.ds(t * tm, tm), pl.ds(j * tf, tf)]

    def out_copy(t, slot):
        return pltpu.make_async_copy(obuf.at[slot], out_tile(t), osem.at[slot])

    def zero_copy(t):
        return pltpu.make_async_copy(zbuf, out_tile(t), zsem.at[0])

    def compute(in_slot, out_slot):
        x = xbuf[in_slot]
        gate = _bdot(x, wgb_ref[...])
        up = _bdot(x, wub_ref[...])
        obuf[out_slot] = (gate * jax.nn.sigmoid(gate) * up).astype(obuf.dtype)

    _tile_pipeline(off_ref, cnt_ref, pl.num_programs(0), n_tiles, in_copy, out_copy, zero_copy,
                   compute)


def _moe_up(xs, wg, wu, tile_off, tm):
    r, d = xs.shape
    f = wg.shape[2]
    tf = min(TF_MOE, f)
    kern = functools.partial(_moe_up_kernel, tm=tm, tf=tf, n_tiles=r // tm)
    return pl.pallas_call(
        kern,
        grid_spec=pltpu.PrefetchScalarGridSpec(
            num_scalar_prefetch=1,
            grid=(f // tf, N_EXPERTS),
            in_specs=[pl.BlockSpec(memory_space=pl.ANY),
                      pl.BlockSpec((1, d, tf), lambda j, e, off: (e, 0, j)),
                      pl.BlockSpec((1, d, tf), lambda j, e, off: (e, 0, j))],
            out_specs=pl.BlockSpec(memory_space=pl.ANY),
            scratch_shapes=[pltpu.VMEM((d, tf), BF16), pltpu.VMEM((d, tf), BF16),
                            pltpu.VMEM((IN_RING, tm, d), BF16),
                            pltpu.VMEM((OUT_RING, tm, tf), BF16),
                            pltpu.VMEM((tm, tf), BF16),
                            pltpu.SMEM((1,), jnp.int32),
                            pltpu.SemaphoreType.DMA((IN_RING,)),
                            pltpu.SemaphoreType.DMA((OUT_RING,)),
                            pltpu.SemaphoreType.DMA((1,))]),
        out_shape=jax.ShapeDtypeStruct((r, f), BF16),
        compiler_params=_cparams(("arbitrary", "arbitrary"), 56),
        name="moe_gate_up",
    )(tile_off, xs, wg, wu)


def _moe_down_kernel(off_ref, act_hbm, wd_ref, y_hbm, wdb_ref, abuf, obuf, zbuf, cnt_ref,
                     asem, osem, zsem, *, tm, tn, n_tiles):
    j = pl.program_id(0)
    e = pl.program_id(1)

    @pl.when((j == 0) & (e == 0))
    def _():
        zbuf[...] = jnp.zeros_like(zbuf)

    @pl.when(off_ref[e + 1] > off_ref[e])
    def _():
        wdb_ref[...] = wd_ref[0].astype(BF16)

    def in_copy(t, slot):
        return pltpu.make_async_copy(act_hbm.at[pl.ds(t * tm, tm), :], abuf.at[slot], asem.at[slot])

    def out_tile(t):
        return y_hbm.at[pl.ds(t * tm, tm), pl.ds(j * tn, tn)]

    def out_copy(t, slot):
        return pltpu.make_async_copy(obuf.at[slot], out_tile(t), osem.at[slot])

    def zero_copy(t):
        return pltpu.make_async_copy(zbuf, out_tile(t), zsem.at[0])

    def compute(in_slot, out_slot):
        obuf[out_slot] = _bdot(abuf[in_slot], wdb_ref[...])

    _tile_pipeline(off_ref, cnt_ref, pl.num_programs(0), n_tiles, in_copy, out_copy, zero_copy,
                   compute)


def _moe_down(act, wd, tile_off, tm):
    r, f = act.shape
    d = wd.shape[2]
    tn = min(TN_MOE, d)
    kern = functools.partial(_moe_down_kernel, tm=tm, tn=tn, n_tiles=r // tm)
    return pl.pallas_call(
        kern,
        grid_spec=pltpu.PrefetchScalarGridSpec(
            num_scalar_prefetch=1,
            grid=(d // tn, N_EXPERTS),
            in_specs=[pl.BlockSpec(memory_space=pl.ANY),
                      pl.BlockSpec((1, f, tn), lambda j, e, off: (e, 0, j))],
            out_specs=pl.BlockSpec(memory_space=pl.ANY),
            scratch_shapes=[pltpu.VMEM((f, tn), BF16),
                            pltpu.VMEM((IN_RING, tm, f), BF16),
                            pltpu.VMEM((OUT_RING, tm, tn), F32),
                            pltpu.VMEM((tm, tn), F32),
                            pltpu.SMEM((1,), jnp.int32),
                            pltpu.SemaphoreType.DMA((IN_RING,)),
                            pltpu.SemaphoreType.DMA((OUT_RING,)),
                            pltpu.SemaphoreType.DMA((1,))]),
        out_shape=jax.ShapeDtypeStruct((r, d), F32),
        compiler_params=_cparams(("arbitrary", "arbitrary"), 56),
        name="moe_down",
    )(tile_off, act, wd)


def _routing_tables(gates, sel, rank, counts, tm):
    ntok = gates.shape[0]
    n_tiles = (2 * ntok) // tm + N_EXPERTS
    cnt = counts[0, :N_EXPERTS].astype(jnp.int32)
    padded = ((cnt + tm - 1) // tm) * tm
    ends = jnp.cumsum(padded)
    offs = ends - padded
    selb = sel[:, :N_EXPERTS] > 0.5
    pos = offs[None, :] + rank[:, :N_EXPERTS].astype(jnp.int32)
    lane = jnp.arange(N_EXPERTS, dtype=jnp.int32)[None, :]
    lo = selb & (lane == jnp.min(jnp.where(selb, lane, N_EXPERTS), axis=1, keepdims=True))
    hi = selb & ~lo
    pick = lambda a, onehot: jnp.sum(jnp.where(onehot, a, 0), axis=1)
    pos0, pos1 = pick(pos, lo), pick(pos, hi)
    g8 = gates[:, :N_EXPERTS]
    g0, g1 = pick(g8, lo), pick(g8, hi)
    tok = jnp.arange(ntok, dtype=jnp.int32)
    pos_all = jnp.concatenate([pos0, pos1]).astype(jnp.int32)
    src = jnp.zeros((n_tiles * tm,), jnp.int32).at[pos_all].set(
        jnp.concatenate([tok, tok]), unique_indices=True)
    tile_off = jnp.concatenate([jnp.zeros((1,), jnp.int32), (ends // tm).astype(jnp.int32)])
    return src, pos_all, g0[:, None], g1[:, None], tile_off


def _moe_final(x, g, router_w, wg, wu, wd, final_norm):
    h, gates, sel, rank, counts = _router(x, g, router_w)
    tm = min(TM_MOE, x.shape[0])
    assert 2 * x.shape[0] // tm >= IN_RING, "the tile ring needs at least IN_RING used row tiles"
    src, pos_all, g0, g1, tile_off = _routing_tables(gates, sel, rank, counts, tm)
    xs = _group_rows(h, src)
    act = _moe_up(xs, wg, wu, tile_off, tm)
    y = _moe_down(act, wd, tile_off, tm)
    return _combine_final_norm(x, y, pos_all, g0, g1, final_norm)


def _forward(x, norm_mix_even, w_in_even, conv3_w, w_out_even, norm_ffn_even,
             ffn_w_gate, ffn_w_up, ffn_w_down, norm_mix_odd, cf_pw1_w, cf_pw1_b,
             cf_dw_w, cf_dw_b, cf_ln_g, cf_ln_b, cf_pw2_w, cf_pw2_b, norm_moe,
             router_w, moe_w_gate, moe_w_up, moe_w_down, final_norm):
    batch, seq, d = x.shape
    xt = x.reshape(batch * seq, d)
    ncol = SC_WIDTH // LANES
    q_lo, q_hi = 3 * SC_WIDTH, 3 * SC_WIDTH + SB_HEADS * SB_HEAD_DIM
    col = jnp.arange(w_in_even.shape[2])
    col_scale = jnp.where((col >= q_lo) & (col < q_hi), -(SB_HEAD_DIM ** -0.5) * LOG2E, 1.0)
    p = _norm_matmul(xt, norm_mix_even[0], w_in_even[0], col_scale.astype(F32), BF16)
    ysb = _stick_breaking(p, batch, seq, 3 * ncol, 3 * ncol + SB_HEADS, 3 * ncol + 2 * SB_HEADS)
    xt = _mix_out_proj(p, ysb, conv3_w[0], w_out_even[0], xt, seq)
    xt = _dense_swiglu(xt, norm_ffn_even[0], ffn_w_gate[0], ffn_w_up[0], ffn_w_down[0])
    u = _norm_glu(xt, norm_mix_odd[0], cf_pw1_w[0], cf_pw1_b[0])
    v = _dwconv_ln_silu(u, cf_dw_w[0], cf_dw_b[0], cf_ln_g[0], cf_ln_b[0], seq)
    xt = _matmul_bias_res(v, cf_pw2_w[0], cf_pw2_b[0], xt)
    out = _moe_final(xt, norm_moe[0], router_w[0], moe_w_gate[0], moe_w_up[0], moe_w_down[0],
                     final_norm)
    return out.reshape(batch, seq, d)


def kernel(x, norm_mix_even, w_in_even, conv3_w, w_out_even, norm_ffn_even, ffn_w_gate, ffn_w_up, ffn_w_down, norm_mix_odd, cf_pw1_w, cf_pw1_b, cf_dw_w, cf_dw_b, cf_ln_g, cf_ln_b, cf_pw2_w, cf_pw2_b, norm_moe, router_w, moe_w_gate, moe_w_up, moe_w_down, final_norm):
    return _forward(x, norm_mix_even, w_in_even, conv3_w, w_out_even, norm_ffn_even,
                    ffn_w_gate, ffn_w_up, ffn_w_down, norm_mix_odd, cf_pw1_w, cf_pw1_b,
                    cf_dw_w, cf_dw_b, cf_ln_g, cf_ln_b, cf_pw2_w, cf_pw2_b, norm_moe,
                    router_w, moe_w_gate, moe_w_up, moe_w_down, final_norm)
```

```python
import functools

import jax
import jax.numpy as jnp
from jax import lax
from jax.experimental import pallas as pl
from jax.experimental.pallas import tpu as pltpu

F32 = jnp.float32
BF16 = jnp.bfloat16

NORM_EPS = 1e-6
SC_WIDTH = 1024
SC_KERNEL = 3
SB_HEADS = 8
SB_HEAD_DIM = 128
CF_KERNEL = 31
N_EXPERTS = 8
LANES = 128
SUBLANES = 8
MIB = 1024 * 1024

TM_PROJ = 1024
TN_PROJ = 1024
TM_RESIDENT = 512
TN_GLU = 512
TM_FFN = 1024
TF_FFN = 256
TQ_ATTN = 512
ATTN_UNROLL = 6
LOG2E = 1.4426950408889634
TM_CONV = 256
CONV_HALO = 32
SC_HALO = 16
TM_ROUTE = 512
TM_MOE = 256
TF_MOE = 1024
TN_MOE = 512
IN_RING = 3
OUT_RING = 2
TG_GATHER = 256
TC_COMBINE = 128
GATHER_GROUP = 8


def _cparams(semantics, vmem_mib):
    return pltpu.CompilerParams(dimension_semantics=semantics,
                                vmem_limit_bytes=vmem_mib * MIB)


def _rmsnorm_to(x_ref, g_ref, h_ref, chunk=128):
    g = g_ref[...]

    def body(c, carry):
        r = pl.multiple_of(c * chunk, chunk)
        x = x_ref[pl.ds(r, chunk), :]
        ms = jnp.mean(x * x, axis=-1, keepdims=True)
        h_ref[pl.ds(r, chunk), :] = (x * lax.rsqrt(ms + NORM_EPS) * g).astype(h_ref.dtype)
        return carry

    lax.fori_loop(0, x_ref.shape[0] // chunk, body, 0)


def _bdot(a, b):
    return jnp.dot(a, b, preferred_element_type=F32)


def _norm_mm_kernel(x_ref, g_ref, w_ref, cs_ref, o_ref, h_ref):
    @pl.when(pl.program_id(1) == 0)
    def _():
        _rmsnorm_to(x_ref, g_ref, h_ref)

    o_ref[...] = (_bdot(h_ref[...], w_ref[...].astype(BF16)) * cs_ref[...]).astype(o_ref.dtype)


def _norm_matmul(x, g, w, col_scale, out_dtype):
    m, k = x.shape
    n = w.shape[1]
    tm, tn = min(TM_PROJ, m), min(TN_PROJ, n)
    return pl.pallas_call(
        _norm_mm_kernel,
        grid=(m // tm, n // tn),
        in_specs=[pl.BlockSpec((tm, k), lambda i, j: (i, 0)),
                  pl.BlockSpec((1, k), lambda i, j: (0, 0)),
                  pl.BlockSpec((k, tn), lambda i, j: (0, j)),
                  pl.BlockSpec((1, tn), lambda i, j: (0, j))],
        out_specs=pl.BlockSpec((tm, tn), lambda i, j: (i, j)),
        out_shape=jax.ShapeDtypeStruct((m, n), out_dtype),
        scratch_shapes=[pltpu.VMEM((tm, k), BF16)],
        compiler_params=_cparams(("parallel", "arbitrary"), 56),
        name="norm_in_proj",
    )(x, g.reshape(1, k), w, col_scale.reshape(1, n))


def _norm_glu_kernel(x_ref, g_ref, wa_ref, wg_ref, ba_ref, bg_ref, o_ref, h_ref):
    @pl.when(pl.program_id(1) == 0)
    def _():
        _rmsnorm_to(x_ref, g_ref, h_ref)

    h = h_ref[...]
    a = _bdot(h, wa_ref[...].astype(BF16)) + ba_ref[...]
    gate = _bdot(h, wg_ref[...].astype(BF16)) + bg_ref[...]
    o_ref[...] = (a * jax.nn.sigmoid(gate)).astype(o_ref.dtype)


def _norm_glu(x, g, w, b):
    m, k = x.shape
    n = w.shape[1] // 2
    tm, tn = min(TM_PROJ, m), min(TN_GLU, n)
    nb = n // tn
    b2 = b.reshape(1, 2 * n)
    return pl.pallas_call(
        _norm_glu_kernel,
        grid=(m // tm, nb),
        in_specs=[pl.BlockSpec((tm, k), lambda i, j: (i, 0)),
                  pl.BlockSpec((1, k), lambda i, j: (0, 0)),
                  pl.BlockSpec((k, tn), lambda i, j: (0, j)),
                  pl.BlockSpec((k, tn), lambda i, j: (0, j + nb)),
                  pl.BlockSpec((1, tn), lambda i, j: (0, j)),
                  pl.BlockSpec((1, tn), lambda i, j: (0, j + nb))],
        out_specs=pl.BlockSpec((tm, tn), lambda i, j: (i, j)),
        out_shape=jax.ShapeDtypeStruct((m, n), F32),
        scratch_shapes=[pltpu.VMEM((tm, k), BF16)],
        compiler_params=_cparams(("parallel", "arbitrary"), 56),
        name="norm_pw1_glu",
    )(x, g.reshape(1, k), w, w, b2, b2)


def _mm_bias_res_kernel(a_ref, w_ref, b_ref, r_ref, o_ref, wb_ref):
    @pl.when(pl.program_id(0) == 0)
    def _():
        wb_ref[...] = w_ref[...].astype(BF16)

    o_ref[...] = r_ref[...] + _bdot(a_ref[...], wb_ref[...]) + b_ref[...]


def _matmul_bias_res(a, w, b, res):
    m, k = a.shape
    n = w.shape[1]
    tm = min(TM_RESIDENT, m)
    return pl.pallas_call(
        _mm_bias_res_kernel,
        grid=(m // tm,),
        in_specs=[pl.BlockSpec((tm, k), lambda i: (i, 0)),
                  pl.BlockSpec((k, n), lambda i: (0, 0), pipeline_mode=pl.Buffered(1)),
                  pl.BlockSpec((1, n), lambda i: (0, 0)),
                  pl.BlockSpec((tm, n), lambda i: (i, 0))],
        out_specs=pl.BlockSpec((tm, n), lambda i: (i, 0)),
        out_shape=jax.ShapeDtypeStruct((m, n), F32),
        scratch_shapes=[pltpu.VMEM((k, n), BF16)],
        compiler_params=_cparams(("arbitrary",), 56),
        name="pw2_residual",
    )(a, w, b.reshape(1, n), res)


def _attn_kernel(q_ref, k_ref, v_ref, o_ref, u_ref, c_ref, acc_ref, nz_ref, cm_ref, *, tq):
    hk = LANES
    tk = 2 * hk
    q0 = pl.program_id(2) * tq
    q = q_ref[...]
    c_ref[...] = jnp.zeros_like(c_ref)
    acc_ref[...] = jnp.zeros_like(acc_ref)

    jj = lax.broadcasted_iota(jnp.int32, (2 * hk, 2 * hk), 0) & (hk - 1)
    ss = lax.broadcasted_iota(jnp.int32, (2 * hk, 2 * hk), 1)
    u_ref[...] = jnp.where((ss >= hk) | (jj >= ss), 1.0, 0.0).astype(BF16)

    def half_sums(lk):
        hi = lk.astype(BF16)
        lo = (lk - hi.astype(F32)).astype(BF16)
        return _bdot(jnp.concatenate([hi, lo], axis=1), u_ref[...])

    def scores(kstart, slot, mask, r0=0):
        kb = k_ref[pl.ds(kstart, tk), :]
        nz = lax.dot_general(q[r0:], kb, (((1,), (1,)), ((), ())), preferred_element_type=F32)
        neg_abs = lax.bitcast_convert_type(
            lax.bitcast_convert_type(nz, jnp.uint32) | jnp.uint32(0x80000000), F32)
        log_keep = jnp.minimum(nz, 0.0) - jnp.log2(1.0 + jnp.exp2(neg_abs))
        if mask is not None:
            log_keep = jnp.where(mask[r0:], log_keep, 0.0)
        nz_ref[slot, r0:] = nz
        cm_ref[slot, 1, r0:] = half_sums(log_keep[:, hk:])
        cm_ref[slot, 0, r0:] = half_sums(log_keep[:, :hk])

    def weights(kstart, slot, mask, r0=0):
        c = c_ref[r0:]
        nz = nz_ref[slot, r0:]
        cm1 = cm_ref[slot, 1, r0:]
        cm0 = cm_ref[slot, 0, r0:]
        logit1 = (c + cm1[:, :hk]) - nz[:, hk:]
        c = c + cm1[:, hk:]
        logit0 = (c + cm0[:, :hk]) - nz[:, :hk]
        c_ref[r0:] = c + cm0[:, hk:]
        a = jnp.exp2(jnp.concatenate([logit0, logit1], axis=1))
        if mask is not None:
            a = jnp.where(mask[r0:], a, 0.0)
        acc_ref[r0:] += _bdot(a.astype(BF16), v_ref[pl.ds(kstart, tk), :])

    rows = lax.broadcasted_iota(jnp.int32, (tq, tk), 0)
    cols = lax.broadcasted_iota(jnp.int32, (tq, tk), 1)
    nd = tq // tk
    assert nd == 2 and ATTN_UNROLL % 2 == 0
    diag = [(pl.multiple_of(q0 + d * tk, tk), (cols + d * tk) < rows, d * tk)
            for d in reversed(range(nd))]
    nb = q0 // tk

    def below(j):
        return pl.multiple_of(q0 - (j + 1) * tk, tk)

    scores(diag[0][0], 0, diag[0][1], diag[0][2])
    for i in range(1, nd):
        scores(diag[i][0], i % 2, diag[i][1], diag[i][2])
        weights(diag[i - 1][0], (i - 1) % 2, diag[i - 1][1], diag[i - 1][2])
    last_k, last_mask, last_r0 = diag[nd - 1]

    @pl.when(nb == 0)
    def _():
        weights(last_k, (nd - 1) % 2, last_mask, last_r0)

    @pl.when(nb > 0)
    def _():
        scores(below(0), nd % 2, None)
        weights(last_k, (nd - 1) % 2, last_mask, last_r0)

        def steps(j0, count):
            for k in range(count):
                weights(below(j0 + k), (nd + k) % 2, None)
                scores(below(j0 + k + 1), (nd + k + 1) % 2, None)

        def body(i, carry):
            steps(ATTN_UNROLL * i, ATTN_UNROLL)
            return carry

        trips = (nb - 1) // ATTN_UNROLL
        lax.fori_loop(0, trips, body, 0)
        rest = nb - 1 - ATTN_UNROLL * trips
        for r in range(1, ATTN_UNROLL, 2):
            @pl.when(rest == r)
            def _():
                steps(nb - 1 - r, r)
                weights(below(nb - 1), (nd + r) % 2, None)

    o_ref[...] = acc_ref[...].astype(o_ref.dtype)


def _stick_breaking(p, batch, seq, col_q, col_k, col_v):
    tq = min(TQ_ATTN, seq)
    nq = seq // tq
    dh = SB_HEAD_DIM
    kern = functools.partial(_attn_kernel, tq=tq)
    return pl.pallas_call(
        kern,
        grid=(batch, SB_HEADS, nq),
        in_specs=[pl.BlockSpec((tq, dh), lambda b, h, i: (b * nq + i, col_q + h)),
                  pl.BlockSpec((seq, dh), lambda b, h, i: (b, col_k + h)),
                  pl.BlockSpec((seq, dh), lambda b, h, i: (b, col_v + h))],
        out_specs=pl.BlockSpec((tq, dh), lambda b, h, i: (b * nq + i, h)),
        out_shape=jax.ShapeDtypeStruct((batch * seq, SB_HEADS * dh), BF16),
        scratch_shapes=[pltpu.VMEM((2 * LANES, 2 * LANES), BF16),
                        pltpu.VMEM((tq, LANES), F32),
                        pltpu.VMEM((tq, dh), F32),
                        pltpu.VMEM((2, tq, 2 * LANES), F32),
                        pltpu.VMEM((2, 2, tq, 2 * LANES), F32)],
        compiler_params=_cparams(("parallel", "parallel", "arbitrary"), 32),
        name="stick_breaking_attention",
    )(p, p, p)


def _mix_out_kernel(pb_ref, pc_ref, ph_ref, hc_ref, hh_ref, cw_ref, ysb_ref, w_ref, x_ref,
                    o_ref, mix_ref, u_ref, wb_ref, *, tm, seq, chunk):
    i = pl.program_id(0)

    @pl.when(i == 0)
    def _():
        wb_ref[...] = w_ref[...].astype(BF16)

    halo = hc_ref[...].astype(F32) * hh_ref[...].astype(F32)
    u_ref[0:SC_HALO, :] = jnp.where((i * tm) % seq == 0, 0.0, halo)
    for r in range(0, tm, chunk):
        u_ref[SC_HALO + r:SC_HALO + r + chunk, :] = (
            pc_ref[r:r + chunk, :].astype(F32) * ph_ref[r:r + chunk, :].astype(F32))
    w0, w1, w2 = cw_ref[0:1, :], cw_ref[1:2, :], cw_ref[2:3, :]
    for r in range(0, tm, chunk):
        base = SC_HALO + r
        conv = (w2 * u_ref[base:base + chunk, :]
                + w1 * u_ref[base - 1:base - 1 + chunk, :]
                + w0 * u_ref[base - 2:base - 2 + chunk, :])
        ysc = pb_ref[r:r + chunk, :].astype(F32) * conv
        mix_ref[r:r + chunk, 0:SC_WIDTH] = ysc.astype(BF16)
    mix_ref[:, SC_WIDTH:] = ysb_ref[...]

    o_ref[...] = x_ref[...] + _bdot(mix_ref[...], wb_ref[...])


def _mix_out_proj(p, ysb, conv_w, w_out, x, seq):
    m = x.shape[0]
    d = w_out.shape[1]
    kdim = w_out.shape[0]
    tm = min(TM_RESIDENT, seq)
    hb = tm // SC_HALO
    kern = functools.partial(_mix_out_kernel, tm=tm, seq=seq, chunk=min(128, tm))
    return pl.pallas_call(
        kern,
        grid=(m // tm,),
        in_specs=[pl.BlockSpec((tm, SC_WIDTH), lambda i: (i, 0)),
                  pl.BlockSpec((tm, SC_WIDTH), lambda i: (i, 1)),
                  pl.BlockSpec((tm, SC_WIDTH), lambda i: (i, 2)),
                  pl.BlockSpec((SC_HALO, SC_WIDTH), lambda i: (jnp.maximum(i * hb - 1, 0), 1)),
                  pl.BlockSpec((SC_HALO, SC_WIDTH), lambda i: (jnp.maximum(i * hb - 1, 0), 2)),
                  pl.BlockSpec((SC_KERNEL, SC_WIDTH), lambda i: (0, 0)),
                  pl.BlockSpec((tm, SC_WIDTH), lambda i: (i, 0)),
                  pl.BlockSpec((kdim, d), lambda i: (0, 0), pipeline_mode=pl.Buffered(1)),
                  pl.BlockSpec((tm, d), lambda i: (i, 0))],
        out_specs=pl.BlockSpec((tm, d), lambda i: (i, 0)),
        out_shape=jax.ShapeDtypeStruct((m, d), F32),
        scratch_shapes=[pltpu.VMEM((tm, kdim), BF16),
                        pltpu.VMEM((tm + SC_HALO, SC_WIDTH), F32),
                        pltpu.VMEM((kdim, d), BF16)],
        compiler_params=_cparams(("arbitrary",), 56),
        name="shortconv_out_proj",
    )(p, p, p, p, p, conv_w, ysb, w_out, x)


def _ffn_kernel(x_ref, g_ref, wg_ref, wu_ref, wd_ref, o_ref, h_ref):
    @pl.when(pl.program_id(1) == 0)
    def _():
        _rmsnorm_to(x_ref, g_ref, h_ref)
        o_ref[...] = x_ref[...]

    h = h_ref[...]
    gate = _bdot(h, wg_ref[...].astype(BF16))
    up = _bdot(h, wu_ref[...].astype(BF16))
    act = (gate * jax.nn.sigmoid(gate) * up).astype(BF16)
    o_ref[...] += _bdot(act, wd_ref[...].astype(BF16))


def _dense_swiglu(x, g, wg, wu, wd):
    m, d = x.shape
    f = wg.shape[1]
    tm, tf = min(TM_FFN, m), min(TF_FFN, f)
    return pl.pallas_call(
        _ffn_kernel,
        grid=(m // tm, f // tf),
        in_specs=[pl.BlockSpec((tm, d), lambda i, j: (i, 0), pipeline_mode=pl.Buffered(1)),
                  pl.BlockSpec((1, d), lambda i, j: (0, 0)),
                  pl.BlockSpec((d, tf), lambda i, j: (0, j)),
                  pl.BlockSpec((d, tf), lambda i, j: (0, j)),
                  pl.BlockSpec((tf, d), lambda i, j: (j, 0))],
        out_specs=pl.BlockSpec((tm, d), lambda i, j: (i, 0)),
        out_shape=jax.ShapeDtypeStruct((m, d), F32),
        scratch_shapes=[pltpu.VMEM((tm, d), BF16)],
        compiler_params=_cparams(("parallel", "arbitrary"), 60),
        name="dense_swiglu",
    )(x, g.reshape(1, d), wg, wu, wd)


def _dwconv_ln_kernel(u_ref, halo_ref, w_ref, b_ref, lg_ref, lb_ref, o_ref, buf_ref, conv_ref,
                      *, tm, seq, rchunk):
    i = pl.program_id(0)
    buf_ref[0:CONV_HALO, :] = jnp.where((i * tm) % seq == 0, 0.0, halo_ref[...])
    buf_ref[CONV_HALO:, :] = u_ref[...]
    nlt = u_ref.shape[1] // LANES
    first = CONV_HALO - (CF_KERNEL - 1)

    def lane_tile(c, carry):
        lanes = pl.ds(pl.multiple_of(c * LANES, LANES), LANES)
        for r in range(0, tm, rchunk):
            acc = None
            for m in range(SUBLANES):
                rows = rchunk + (SUBLANES if m else 0)
                g = None
                for k in range(CF_KERNEL):
                    if (first + k) % SUBLANES != m:
                        continue
                    base = r + first + k - m
                    term = w_ref[k:k + 1, lanes] * buf_ref[base:base + rows, lanes]
                    g = term if g is None else g + term
                if m:
                    g = pltpu.roll(g, rows - m, axis=0)[:rchunk]
                acc = g if acc is None else acc + g
            conv_ref[r:r + rchunk, lanes] = acc
        return carry

    lax.fori_loop(0, nlt, lane_tile, 0)

    bias, lg, lb = b_ref[...], lg_ref[...], lb_ref[...]
    for r in range(0, tm, rchunk):
        y = conv_ref[r:r + rchunk, :] + bias
        mu = jnp.mean(y, axis=-1, keepdims=True)
        yc = y - mu
        var = jnp.mean(yc * yc, axis=-1, keepdims=True)
        t = yc * lax.rsqrt(var + NORM_EPS) * lg + lb
        o_ref[r:r + rchunk, :] = (t * jax.nn.sigmoid(t)).astype(o_ref.dtype)


def _dwconv_ln_silu(u, w, b, lg, lb, seq):
    m, c = u.shape
    tm = min(TM_CONV, seq)
    hb = tm // CONV_HALO
    kern = functools.partial(_dwconv_ln_kernel, tm=tm, seq=seq, rchunk=min(128, tm))
    return pl.pallas_call(
        kern,
        grid=(m // tm,),
        in_specs=[pl.BlockSpec((tm, c), lambda i: (i, 0)),
                  pl.BlockSpec((CONV_HALO, c), lambda i: (jnp.maximum(i * hb - 1, 0), 0)),
                  pl.BlockSpec((CF_KERNEL, c), lambda i: (0, 0)),
                  pl.BlockSpec((1, c), lambda i: (0, 0)),
                  pl.BlockSpec((1, c), lambda i: (0, 0)),
                  pl.BlockSpec((1, c), lambda i: (0, 0))],
        out_specs=pl.BlockSpec((tm, c), lambda i: (i, 0)),
        out_shape=jax.ShapeDtypeStruct((m, c), BF16),
        scratch_shapes=[pltpu.VMEM((tm + CONV_HALO, c), F32),
                        pltpu.VMEM((tm, c), F32)],
        compiler_params=_cparams(("parallel",), 32),
        name="dwconv_ln_silu",
    )(u, u, w, b.reshape(1, c), lg.reshape(1, c), lb.reshape(1, c))


def _router_kernel(x_ref, g_ref, rw_ref, h_ref, gates_ref, sel_ref, rank_ref, cnt_ref, carry_ref,
                   *, tm):
    @pl.when(pl.program_id(0) == 0)
    def _():
        carry_ref[...] = jnp.zeros_like(carry_ref)

    _rmsnorm_to(x_ref, g_ref, h_ref)
    logits = jnp.dot(h_ref[...], rw_ref[...], preferred_element_type=F32,
                     precision=lax.Precision.HIGHEST)
    lane = lax.broadcasted_iota(jnp.int32, logits.shape, 1).astype(F32)
    neg = jnp.float32(-jnp.inf)
    logits = jnp.where(lane < N_EXPERTS, logits, neg)
    m1 = jnp.max(logits, axis=-1, keepdims=True)
    i1 = jnp.min(jnp.where(logits == m1, lane, float(LANES)), axis=-1, keepdims=True)
    rest = jnp.where(lane == i1, neg, logits)
    m2 = jnp.max(rest, axis=-1, keepdims=True)
    i2 = jnp.min(jnp.where(rest == m2, lane, float(LANES)), axis=-1, keepdims=True)
    e2 = jnp.exp(m2 - m1)
    w1 = 1.0 / (1.0 + e2)
    w2 = e2 / (1.0 + e2)
    gates_ref[...] = jnp.where(lane == i1, w1, 0.0) + jnp.where(lane == i2, w2, 0.0)
    sel = jnp.where((lane == i1) | (lane == i2), 1.0, 0.0)
    sel_ref[...] = sel
    rr = lax.broadcasted_iota(jnp.int32, (tm, tm), 0)
    cc = lax.broadcasted_iota(jnp.int32, (tm, tm), 1)
    tri = jnp.where(cc < rr, 1.0, 0.0).astype(BF16)
    rank_ref[...] = _bdot(tri, sel.astype(BF16)) + carry_ref[...]
    carry_ref[...] += jnp.sum(sel, axis=0, keepdims=True)
    cnt_ref[...] = carry_ref[...]


def _router(x, g, router_w):
    m, d = x.shape
    tm = min(TM_ROUTE, m)
    rw = jnp.zeros((d, LANES), F32).at[:, :N_EXPERTS].set(router_w)
    kern = functools.partial(_router_kernel, tm=tm)
    wide = jax.ShapeDtypeStruct((m, LANES), F32)
    return pl.pallas_call(
        kern,
        grid=(m // tm,),
        in_specs=[pl.BlockSpec((tm, d), lambda i: (i, 0)),
                  pl.BlockSpec((1, d), lambda i: (0, 0)),
                  pl.BlockSpec((d, LANES), lambda i: (0, 0))],
        out_specs=[pl.BlockSpec((tm, d), lambda i: (i, 0)),
                   pl.BlockSpec((tm, LANES), lambda i: (i, 0)),
                   pl.BlockSpec((tm, LANES), lambda i: (i, 0)),
                   pl.BlockSpec((tm, LANES), lambda i: (i, 0)),
                   pl.BlockSpec((1, LANES), lambda i: (0, 0))],
        out_shape=[jax.ShapeDtypeStruct((m, d), F32), wide, wide, wide,
                   jax.ShapeDtypeStruct((1, LANES), F32)],
        scratch_shapes=[pltpu.VMEM((1, LANES), F32)],
        compiler_params=_cparams(("arbitrary",), 32),
        name="router_top2",
    )(x, g.reshape(1, d), rw)


def _start_row_gather(src_hbm, idx_ref, base, n, dst_ref, sem):
    def body(g, carry):
        for k in range(GATHER_GROUP):
            r = g * GATHER_GROUP + k
            t = idx_ref[base + r]
            pltpu.make_async_copy(src_hbm.at[pl.ds(t, 1), :], dst_ref.at[pl.ds(r, 1), :],
                                  sem).start(priority=k % 2)
        return carry

    lax.fori_loop(0, n // GATHER_GROUP, body, 0)


def _wait_row_gather(src_hbm, n, dst_ref, sem):
    def body(r, carry):
        pltpu.make_async_copy(src_hbm.at[pl.ds(0, 1), :], dst_ref.at[pl.ds(r, 1), :], sem).wait()
        return carry

    lax.fori_loop(0, n, body, 0, unroll=8)


def _group_rows_kernel(src_ref, h_hbm, o_ref, buf_ref, sem_ref, *, tg):
    i = pl.program_id(0)
    n = pl.num_programs(0)
    slot = i % 2

    @pl.when(i == 0)
    def _():
        _start_row_gather(h_hbm, src_ref, 0, tg, buf_ref.at[0], sem_ref.at[0])

    @pl.when(i + 1 < n)
    def _():
        _start_row_gather(h_hbm, src_ref, (i + 1) * tg, tg, buf_ref.at[1 - slot], sem_ref.at[1 - slot])

    _wait_row_gather(h_hbm, tg, buf_ref.at[slot], sem_ref.at[slot])
    o_ref[...] = buf_ref[slot].astype(o_ref.dtype)


def _group_rows(h, src):
    d = h.shape[1]
    r = src.shape[0]
    tg = min(TG_GATHER, r)
    kern = functools.partial(_group_rows_kernel, tg=tg)
    return pl.pallas_call(
        kern,
        grid_spec=pltpu.PrefetchScalarGridSpec(
            num_scalar_prefetch=1,
            grid=(r // tg,),
            in_specs=[pl.BlockSpec(memory_space=pl.ANY)],
            out_specs=pl.BlockSpec((tg, d), lambda i, s: (i, 0)),
            scratch_shapes=[pltpu.VMEM((2, tg, d), F32),
                            pltpu.SemaphoreType.DMA((2,))]),
        out_shape=jax.ShapeDtypeStruct((r, d), BF16),
        compiler_params=_cparams(("arbitrary",), 32),
        name="group_rows_by_expert",
    )(src, h)


def _combine_kernel(pos_ref, x_ref, g0_ref, g1_ref, fn_ref, y_hbm, o_ref, buf_ref, sem_ref,
                    *, tc, ntok):
    i = pl.program_id(0)
    n = pl.num_programs(0)
    slot = i % 2

    def start(step, s):
        _start_row_gather(y_hbm, pos_ref, step * tc, tc, buf_ref.at[s, 0], sem_ref.at[s, 0])
        _start_row_gather(y_hbm, pos_ref, ntok + step * tc, tc, buf_ref.at[s, 1], sem_ref.at[s, 1])

    @pl.when(i == 0)
    def _():
        start(0, 0)

    @pl.when(i + 1 < n)
    def _():
        start(i + 1, 1 - slot)

    _wait_row_gather(y_hbm, tc, buf_ref.at[slot, 0], sem_ref.at[slot, 0])
    _wait_row_gather(y_hbm, tc, buf_ref.at[slot, 1], sem_ref.at[slot, 1])
    x = x_ref[...] + g0_ref[...] * buf_ref[slot, 0] + g1_ref[...] * buf_ref[slot, 1]
    ms = jnp.mean(x * x, axis=-1, keepdims=True)
    o_ref[...] = x * lax.rsqrt(ms + NORM_EPS) * fn_ref[...]


def _combine_final_norm(x, y, pos, g0, g1, final_norm):
    m, d = x.shape
    tc = min(TC_COMBINE, m)
    kern = functools.partial(_combine_kernel, tc=tc, ntok=m)
    return pl.pallas_call(
        kern,
        grid_spec=pltpu.PrefetchScalarGridSpec(
            num_scalar_prefetch=1,
            grid=(m // tc,),
            in_specs=[pl.BlockSpec((tc, d), lambda i, s: (i, 0)),
                      pl.BlockSpec((tc, 1), lambda i, s: (i, 0)),
                      pl.BlockSpec((tc, 1), lambda i, s: (i, 0)),
                      pl.BlockSpec((1, d), lambda i, s: (0, 0)),
                      pl.BlockSpec(memory_space=pl.ANY)],
            out_specs=pl.BlockSpec((tc, d), lambda i, s: (i, 0)),
            scratch_shapes=[pltpu.VMEM((2, 2, tc, d), F32),
                            pltpu.SemaphoreType.DMA((2, 2))]),
        out_shape=jax.ShapeDtypeStruct((m, d), F32),
        compiler_params=_cparams(("arbitrary",), 32),
        name="moe_combine_final_norm",
    )(pos, x, g0, g1, final_norm.reshape(1, d), y)


def _tile_pipeline(off_ref, cnt_ref, n_col_tiles, n_tiles, in_copy, out_copy, zero_copy, compute):
    j = pl.program_id(0)
    e = pl.program_id(1)
    n_used = off_ref[N_EXPERTS]
    last_col = j == n_col_tiles - 1

    @pl.when((j == 0) & (e == 0))
    def _():
        cnt_ref[0] = 0
        for k in range(IN_RING):
            in_copy(k, k).start()

    def body(t, g):
        in_slot = g % IN_RING
        out_slot = g % OUT_RING
        in_copy(t, in_slot).wait()

        @pl.when(g >= OUT_RING)
        def _():
            out_copy(0, out_slot).wait()

        compute(in_slot, out_slot)
        out_copy(t, out_slot).start()
        ahead = t + IN_RING
        wraps = ahead >= n_used

        @pl.when(jnp.logical_not(wraps & last_col))
        def _():
            in_copy(jnp.where(wraps, ahead - n_used, ahead), in_slot).start()

        return g + 1

    g = lax.fori_loop(off_ref[e], off_ref[e + 1], body, cnt_ref[0])
    cnt_ref[0] = g

    @pl.when(e == N_EXPERTS - 1)
    def _():
        def tail(t, carry):
            zero_copy(t).start()
            zero_copy(t).wait()
            return carry

        lax.fori_loop(n_used, n_tiles, tail, 0)

        @pl.when(last_col)
        def _():
            for back in range(OUT_RING, 0, -1):
                @pl.when(g >= back)
                def _():
                    out_copy(0, (g - back) % OUT_RING).wait()


def _moe_up_kernel(off_ref, xs_hbm, wg_ref, wu_ref, act_hbm, wgb_ref, wub_ref, xbuf, obuf, zbuf,
                   cnt_ref, xsem, osem, zsem, *, tm, tf, n_tiles):
    j = pl.program_id(0)
    e = pl.program_id(1)

    @pl.when((j == 0) & (e == 0))
    def _():
        zbuf[...] = jnp.zeros_like(zbuf)

    @pl.when(off_ref[e + 1] > off_ref[e])
    def _():
        wgb_ref[...] = wg_ref[0].astype(BF16)
        wub_ref[...] = wu_ref[0].astype(BF16)

    def in_copy(t, slot):
        return pltpu.make_async_copy(xs_hbm.at[pl.ds(t * tm, tm), :], xbuf.at[slot], xsem.at[slot])

    def out_tile(t):
        return act_hbm.at[pl.ds(t * tm, tm), pl.ds(j * tf, tf)]

    def out_copy(t, slot):
        return pltpu.make_async_copy(obuf.at[slot], out_tile(t), osem.at[slot])

    def zero_copy(t):
        return pltpu.make_async_copy(zbuf, out_tile(t), zsem.at[0])

    def compute(in_slot, out_slot):
        x = xbuf[in_slot]
        gate = _bdot(x, wgb_ref[...])
        up = _bdot(x, wub_ref[...])
        obuf[out_slot] = (gate * jax.nn.sigmoid(gate) * up).astype(obuf.dtype)

    _tile_pipeline(off_ref, cnt_ref, pl.num_programs(0), n_tiles, in_copy, out_copy, zero_copy,
                   compute)


def _moe_up(xs, wg, wu, tile_off, tm):
    r, d = xs.shape
    f = wg.shape[2]
    tf = min(TF_MOE, f)
    kern = functools.partial(_moe_up_kernel, tm=tm, tf=tf, n_tiles=r // tm)
    return pl.pallas_call(
        kern,
        grid_spec=pltpu.PrefetchScalarGridSpec(
            num_scalar_prefetch=1,
            grid=(f // tf, N_EXPERTS),
            in_specs=[pl.BlockSpec(memory_space=pl.ANY),
                      pl.BlockSpec((1, d, tf), lambda j, e, off: (e, 0, j)),
                      pl.BlockSpec((1, d, tf), lambda j, e, off: (e, 0, j))],
            out_specs=pl.BlockSpec(memory_space=pl.ANY),
            scratch_shapes=[pltpu.VMEM((d, tf), BF16), pltpu.VMEM((d, tf), BF16),
                            pltpu.VMEM((IN_RING, tm, d), BF16),
                            pltpu.VMEM((OUT_RING, tm, tf), BF16),
                            pltpu.VMEM((tm, tf), BF16),
                            pltpu.SMEM((1,), jnp.int32),
                            pltpu.SemaphoreType.DMA((IN_RING,)),
                            pltpu.SemaphoreType.DMA((OUT_RING,)),
                            pltpu.SemaphoreType.DMA((1,))]),
        out_shape=jax.ShapeDtypeStruct((r, f), BF16),
        compiler_params=_cparams(("arbitrary", "arbitrary"), 56),
        name="moe_gate_up",
    )(tile_off, xs, wg, wu)


def _moe_down_kernel(off_ref, act_hbm, wd_ref, y_hbm, wdb_ref, abuf, obuf, zbuf, cnt_ref,
                     asem, osem, zsem, *, tm, tn, n_tiles):
    j = pl.program_id(0)
    e = pl.program_id(1)

    @pl.when((j == 0) & (e == 0))
    def _():
        zbuf[...] = jnp.zeros_like(zbuf)

    @pl.when(off_ref[e + 1] > off_ref[e])
    def _():
        wdb_ref[...] = wd_ref[0].astype(BF16)

    def in_copy(t, slot):
        return pltpu.make_async_copy(act_hbm.at[pl.ds(t * tm, tm), :], abuf.at[slot], asem.at[slot])

    def out_tile(t):
        return y_hbm.at[pl.ds(t * tm, tm), pl.ds(j * tn, tn)]

    def out_copy(t, slot):
        return pltpu.make_async_copy(obuf.at[slot], out_tile(t), osem.at[slot])

    def zero_copy(t):
        return pltpu.make_async_copy(zbuf, out_tile(t), zsem.at[0])

    def compute(in_slot, out_slot):
        obuf[out_slot] = _bdot(abuf[in_slot], wdb_ref[...])

    _tile_pipeline(off_ref, cnt_ref, pl.num_programs(0), n_tiles, in_copy, out_copy, zero_copy,
                   compute)


def _moe_down(act, wd, tile_off, tm):
    r, f = act.shape
    d = wd.shape[2]
    tn = min(TN_MOE, d)
    kern = functools.partial(_moe_down_kernel, tm=tm, tn=tn, n_tiles=r // tm)
    return pl.pallas_call(
        kern,
        grid_spec=pltpu.PrefetchScalarGridSpec(
            num_scalar_prefetch=1,
            grid=(d // tn, N_EXPERTS),
            in_specs=[pl.BlockSpec(memory_space=pl.ANY),
                      pl.BlockSpec((1, f, tn), lambda j, e, off: (e, 0, j))],
            out_specs=pl.BlockSpec(memory_space=pl.ANY),
            scratch_shapes=[pltpu.VMEM((f, tn), BF16),
                            pltpu.VMEM((IN_RING, tm, f), BF16),
                            pltpu.VMEM((OUT_RING, tm, tn), F32),
                            pltpu.VMEM((tm, tn), F32),
                            pltpu.SMEM((1,), jnp.int32),
                            pltpu.SemaphoreType.DMA((IN_RING,)),
                            pltpu.SemaphoreType.DMA((OUT_RING,)),
                            pltpu.SemaphoreType.DMA((1,))]),
        out_shape=jax.ShapeDtypeStruct((r, d), F32),
        compiler_params=_cparams(("arbitrary", "arbitrary"), 56),
        name="moe_down",
    )(tile_off, act, wd)


def _routing_tables(gates, sel, rank, counts, tm):
    ntok = gates.shape[0]
    n_tiles = (2 * ntok) // tm + N_EXPERTS
    cnt = counts[0, :N_EXPERTS].astype(jnp.int32)
    padded = ((cnt + tm - 1) // tm) * tm
    ends = jnp.cumsum(padded)
    offs = ends - padded
    selb = sel[:, :N_EXPERTS] > 0.5
    pos = offs[None, :] + rank[:, :N_EXPERTS].astype(jnp.int32)
    lane = jnp.arange(N_EXPERTS, dtype=jnp.int32)[None, :]
    lo = selb & (lane == jnp.min(jnp.where(selb, lane, N_EXPERTS), axis=1, keepdims=True))
    hi = selb & ~lo
    pick = lambda a, onehot: jnp.sum(jnp.where(onehot, a, 0), axis=1)
    pos0, pos1 = pick(pos, lo), pick(pos, hi)
    g8 = gates[:, :N_EXPERTS]
    g0, g1 = pick(g8, lo), pick(g8, hi)
    tok = jnp.arange(ntok, dtype=jnp.int32)
    pos_all = jnp.concatenate([pos0, pos1]).astype(jnp.int32)
    src = jnp.zeros((n_tiles * tm,), jnp.int32).at[pos_all].set(
        jnp.concatenate([tok, tok]), unique_indices=True)
    tile_off = jnp.concatenate([jnp.zeros((1,), jnp.int32), (ends // tm).astype(jnp.int32)])
    return src, pos_all, g0[:, None], g1[:, None], tile_off


def _moe_final(x, g, router_w, wg, wu, wd, final_norm):
    h, gates, sel, rank, counts = _router(x, g, router_w)
    tm = min(TM_MOE, x.shape[0])
    assert 2 * x.shape[0] // tm >= IN_RING, "the tile ring needs at least IN_RING used row tiles"
    src, pos_all, g0, g1, tile_off = _routing_tables(gates, sel, rank, counts, tm)
    xs = _group_rows(h, src)
    act = _moe_up(xs, wg, wu, tile_off, tm)
    y = _moe_down(act, wd, tile_off, tm)
    return _combine_final_norm(x, y, pos_all, g0, g1, final_norm)


def _forward(x, norm_mix_even, w_in_even, conv3_w, w_out_even, norm_ffn_even,
             ffn_w_gate, ffn_w_up, ffn_w_down, norm_mix_odd, cf_pw1_w, cf_pw1_b,
             cf_dw_w, cf_dw_b, cf_ln_g, cf_ln_b, cf_pw2_w, cf_pw2_b, norm_moe,
             router_w, moe_w_gate, moe_w_up, moe_w_down, final_norm):
    batch, seq, d = x.shape
    xt = x.reshape(batch * seq, d)
    ncol = SC_WIDTH // LANES
    q_lo, q_hi = 3 * SC_WIDTH, 3 * SC_WIDTH + SB_HEADS * SB_HEAD_DIM
    col = jnp.arange(w_in_even.shape[2])
    col_scale = jnp.where((col >= q_lo) & (col < q_hi), -(SB_HEAD_DIM ** -0.5) * LOG2E, 1.0)
    p = _norm_matmul(xt, norm_mix_even[0], w_in_even[0], col_scale.astype(F32), BF16)
    ysb = _stick_breaking(p, batch, seq, 3 * ncol, 3 * ncol + SB_HEADS, 3 * ncol + 2 * SB_HEADS)
    xt = _mix_out_proj(p, ysb, conv3_w[0], w_out_even[0], xt, seq)
    xt = _dense_swiglu(xt, norm_ffn_even[0], ffn_w_gate[0], ffn_w_up[0], ffn_w_down[0])
    u = _norm_glu(xt, norm_mix_odd[0], cf_pw1_w[0], cf_pw1_b[0])
    v = _dwconv_ln_silu(u, cf_dw_w[0], cf_dw_b[0], cf_ln_g[0], cf_ln_b[0], seq)
    xt = _matmul_bias_res(v, cf_pw2_w[0], cf_pw2_b[0], xt)
    out = _moe_final(xt, norm_moe[0], router_w[0], moe_w_gate[0], moe_w_up[0], moe_w_down[0],
                     final_norm)
    return out.reshape(batch, seq, d)


def kernel(x, norm_mix_even, w_in_even, conv3_w, w_out_even, norm_ffn_even, ffn_w_gate, ffn_w_up, ffn_w_down, norm_mix_odd, cf_pw1_w, cf_pw1_b, cf_dw_w, cf_dw_b, cf_ln_g, cf_ln_b, cf_pw2_w, cf_pw2_b, norm_moe, router_w, moe_w_gate, moe_w_up, moe_w_down, final_norm):
    return _forward(x, norm_mix_even, w_in_even, conv3_w, w_out_even, norm_ffn_even,
                    ffn_w_gate, ffn_w_up, ffn_w_down, norm_mix_odd, cf_pw1_w, cf_pw1_b,
                    cf_dw_w, cf_dw_b, cf_ln_g, cf_ln_b, cf_pw2_w, cf_pw2_b, norm_moe,
                    router_w, moe_w_gate, moe_w_up, moe_w_down, final_norm)
```
